```python
import jax, jax.numpy as jnp
from jax import lax
import numpy as np

D_MODEL = 1024
BATCH = 8
SEQ = 2048
DEPTH = 2

D_CONV = D_MODEL // 2
D_POOL = D_MODEL // 2
D_SC = D_MODEL // 2
CONV_A_K = 31
SC_K = 3
POOL_WINDOWS = (2, 4, 8, 16)
N_POOL_GROUPS = len(POOL_WINDOWS)
POOL_GROUP_DIM = D_POOL // N_POOL_GROUPS
POOL_OUT_DIM = D_MODEL // N_POOL_GROUPS
MAX_WINDOW = max(POOL_WINDOWS)
N_BRANCH = 3
PROJ_SPLITS = (D_CONV, 2 * D_CONV, 2 * D_CONV + D_POOL, 2 * D_CONV + D_POOL + D_SC,
               2 * D_CONV + D_POOL + 2 * D_SC, 2 * D_CONV + D_POOL + 3 * D_SC)
D_IN_PROJ = 2 * D_CONV + D_POOL + 3 * D_SC + N_BRANCH * D_MODEL
N_MEM = 256
XATTN_HEADS = 4
XATTN_HEAD_DIM = D_MODEL // XATTN_HEADS
MOE_GROUPS = 4
EXPERTS_PER_GROUP = 4
N_EXPERTS = MOE_GROUPS * EXPERTS_PER_GROUP
EXPERT_TOP_K = 2
D_EXPERT = D_MODEL // 4
EPS = 1e-6

kernel_name = "hybrid_gated_conv_pool_shortconv_hiermoe"


def rmsnorm(x, g):
    xf = x.astype(jnp.float32)
    y = xf * lax.rsqrt(jnp.mean(xf * xf, axis=-1, keepdims=True) + EPS)
    return (y * g.astype(jnp.float32)).astype(x.dtype)


def layernorm(x, g, b):
    xf = x.astype(jnp.float32)
    mu = jnp.mean(xf, axis=-1, keepdims=True)
    var = jnp.mean(jnp.square(xf - mu), axis=-1, keepdims=True)
    y = (xf - mu) * lax.rsqrt(var + EPS)
    return (y * g.astype(jnp.float32) + b.astype(jnp.float32)).astype(x.dtype)


def causal_dwconv(u, w):
    k = w.shape[0]
    u_pad = jnp.pad(u, ((0, 0), (k - 1, 0), (0, 0)))
    return lax.conv_general_dilated(
        u_pad, w[:, None, :].astype(u.dtype), window_strides=(1,), padding='VALID',
        dimension_numbers=('NWC', 'WIO', 'NWC'), feature_group_count=u.shape[-1])


def multiscale_pool_minus_token(u):
    s = u.shape[1]
    uf = u.astype(jnp.float32)
    cs = jnp.cumsum(uf, axis=1)
    cs_pad = jnp.pad(cs, ((0, 0), (MAX_WINDOW, 0), (0, 0)))
    t = jnp.arange(s, dtype=jnp.int32)
    outs = []
    for i, w in enumerate(POOL_WINDOWS):
        c0, c1 = i * POOL_GROUP_DIM, (i + 1) * POOL_GROUP_DIM
        prev = cs_pad[:, MAX_WINDOW - w:MAX_WINDOW - w + s, c0:c1]
        cnt = jnp.minimum(t + 1, w).astype(jnp.float32)[None, :, None]
        outs.append((cs[:, :, c0:c1] - prev) / cnt - uf[:, :, c0:c1])
    return jnp.concatenate(outs, axis=-1).astype(u.dtype)


def gated_branch_mixer(h, w_in, conv_a_w, conv_a_b, ln_a_g, ln_a_b, w_a_out,
                       w_pool_grp, pool_scale, conv_c_w, w_c_out, w_o):
    bsz, s, _ = h.shape
    proj = h @ w_in
    a_val, a_gate, u_pool, c_x, c_b, c_c, g = jnp.split(proj, PROJ_SPLITS, axis=-1)
    a = a_val * jax.nn.sigmoid(a_gate)
    a = causal_dwconv(a, conv_a_w) + conv_a_b
    a = jax.nn.silu(layernorm(a, ln_a_g, ln_a_b)) @ w_a_out
    p = multiscale_pool_minus_token(u_pool).reshape(bsz, s, N_POOL_GROUPS, POOL_GROUP_DIM)
    p = jnp.einsum('bsgc,gco->bsgo', p, w_pool_grp).reshape(bsz, s, D_MODEL) * pool_scale
    c = (c_b * causal_dwconv(c_c * c_x, conv_c_w)) @ w_c_out
    gates = jax.nn.sigmoid(g).reshape(bsz, s, N_BRANCH, D_MODEL)
    merged = gates[:, :, 0] * a + gates[:, :, 1] * p + gates[:, :, 2] * c
    return merged @ w_o


def memory_cross_attention(h, mem_n, w_xq, w_xkv, w_xo):
    bsz, s, _ = h.shape
    q = (h @ w_xq).reshape(bsz, s, XATTN_HEADS, XATTN_HEAD_DIM)
    k, v = jnp.split(mem_n @ w_xkv, 2, axis=-1)
    k = k.reshape(bsz, N_MEM, XATTN_HEADS, XATTN_HEAD_DIM)
    v = v.reshape(bsz, N_MEM, XATTN_HEADS, XATTN_HEAD_DIM)
    scores = jnp.einsum('bshd,bmhd->bhsm', q, k).astype(jnp.float32) * (XATTN_HEAD_DIM ** -0.5)
    probs = jax.nn.softmax(scores, axis=-1).astype(v.dtype)
    o = jnp.einsum('bhsm,bmhd->bshd', probs, v).reshape(bsz, s, D_MODEL)
    return o @ w_xo


def hierarchical_moe(h, w_rg, b_rg, w_re, b_re, w_e_gate, w_e_up, w_e_down):
    bsz, s, d = h.shape
    t = h.reshape(-1, d)
    n = t.shape[0]
    p_grp = jax.nn.softmax((t @ w_rg + b_rg).astype(jnp.float32), axis=-1)
    g_val, g_idx = lax.top_k(p_grp, 1)
    le = (t @ w_re + b_re).astype(jnp.float32).reshape(n, MOE_GROUPS, EXPERTS_PER_GROUP)
    le_sel = jnp.take_along_axis(le, g_idx[:, :, None], axis=1)[:, 0]
    p_exp = jax.nn.softmax(le_sel, axis=-1)
    e_val, e_idx = lax.top_k(p_exp, EXPERT_TOP_K)
    e_val = e_val / jnp.sum(e_val, axis=-1, keepdims=True)
    weight = g_val * e_val
    eid = g_idx * EXPERTS_PER_GROUP + e_idx
    gate = jnp.sum(jax.nn.one_hot(eid, N_EXPERTS, dtype=jnp.float32) * weight[..., None], axis=1)
    act = jax.nn.silu(jnp.einsum('nd,edf->nef', t, w_e_gate)) * jnp.einsum('nd,edf->nef', t, w_e_up)
    act = act * gate[:, :, None].astype(act.dtype)
    out = jnp.einsum('nef,efd->nd', act, w_e_down)
    return out.reshape(bsz, s, d)


def setup_inputs(seed: int = 0) -> dict:
    key = jax.random.key(seed)
    ks = jax.random.split(key, 32)
    f32 = jnp.float32

    def nrm(k, shape, scale):
        return jax.random.normal(k, shape, f32) * scale

    def gain(k, shape):
        return 1.0 + 0.02 * jax.random.normal(k, shape, f32)

    L = DEPTH
    return {
        "x": nrm(ks[0], (BATCH, SEQ, D_MODEL), 1.0),
        "mem": nrm(ks[1], (BATCH, N_MEM, D_MODEL), 1.0),
        "norm_mix_g": gain(ks[2], (L, D_MODEL)),
        "w_in": nrm(ks[3], (L, D_MODEL, D_IN_PROJ), D_MODEL ** -0.5),
        "conv_a_w": nrm(ks[4], (L, CONV_A_K, D_CONV), CONV_A_K ** -0.5),
        "conv_a_b": nrm(ks[5], (L, D_CONV), 0.02),
        "ln_a_g": gain(ks[6], (L, D_CONV)),
        "ln_a_b": nrm(ks[7], (L, D_CONV), 0.02),
        "w_a_out": nrm(ks[8], (L, D_CONV, D_MODEL), D_CONV ** -0.5),
        "w_pool_grp": nrm(ks[9], (L, N_POOL_GROUPS, POOL_GROUP_DIM, POOL_OUT_DIM), POOL_GROUP_DIM ** -0.5),
        "pool_scale": gain(ks[10], (L, D_MODEL)),
        "conv_c_w": nrm(ks[11], (L, SC_K, D_SC), SC_K ** -0.5),
        "w_c_out": nrm(ks[12], (L, D_SC, D_MODEL), D_SC ** -0.5),
        "w_o": nrm(ks[13], (L, D_MODEL, D_MODEL), D_MODEL ** -0.5),
        "norm_x_g": gain(ks[14], (L, D_MODEL)),
        "norm_mem_g": gain(ks[15], (L, D_MODEL)),
        "w_xq": nrm(ks[16], (L, D_MODEL, D_MODEL), D_MODEL ** -0.5),
        "w_xkv": nrm(ks[17], (L, D_MODEL, 2 * D_MODEL), D_MODEL ** -0.5),
        "w_xo": nrm(ks[18], (L, D_MODEL, D_MODEL), D_MODEL ** -0.5),
        "norm_ffn_g": gain(ks[19], (L, D_MODEL)),
        "w_rg": nrm(ks[20], (L, D_MODEL, MOE_GROUPS), D_MODEL ** -0.5),
        "b_rg": nrm(ks[21], (L, MOE_GROUPS), 0.01),
        "w_re": nrm(ks[22], (L, D_MODEL, N_EXPERTS), D_MODEL ** -0.5),
        "b_re": nrm(ks[23], (L, N_EXPERTS), 0.01),
        "w_e_gate": nrm(ks[24], (L, N_EXPERTS, D_MODEL, D_EXPERT), D_MODEL ** -0.5),
        "w_e_up": nrm(ks[25], (L, N_EXPERTS, D_MODEL, D_EXPERT), D_MODEL ** -0.5),
        "w_e_down": nrm(ks[26], (L, N_EXPERTS, D_EXPERT, D_MODEL), D_EXPERT ** -0.5),
        "norm_f_g": gain(ks[27], (D_MODEL,)),
    }


def reference(x, mem, norm_mix_g, w_in, conv_a_w, conv_a_b, ln_a_g, ln_a_b, w_a_out,
              w_pool_grp, pool_scale, conv_c_w, w_c_out, w_o, norm_x_g, norm_mem_g,
              w_xq, w_xkv, w_xo, norm_ffn_g, w_rg, b_rg, w_re, b_re,
              w_e_gate, w_e_up, w_e_down, norm_f_g):
    for l in range(DEPTH):
        h = rmsnorm(x, norm_mix_g[l])
        x = x + gated_branch_mixer(h, w_in[l], conv_a_w[l], conv_a_b[l], ln_a_g[l], ln_a_b[l],
                                   w_a_out[l], w_pool_grp[l], pool_scale[l], conv_c_w[l],
                                   w_c_out[l], w_o[l])
        x = x + memory_cross_attention(rmsnorm(x, norm_x_g[l]), rmsnorm(mem, norm_mem_g[l]),
                                       w_xq[l], w_xkv[l], w_xo[l])
        x = x + hierarchical_moe(rmsnorm(x, norm_ffn_g[l]), w_rg[l], b_rg[l], w_re[l], b_re[l],
                                 w_e_gate[l], w_e_up[l], w_e_down[l])
    return rmsnorm(x, norm_f_g)
```

```python
import functools

import jax
import jax.numpy as jnp
from jax import lax
from jax.experimental import pallas as pl
from jax.experimental.pallas import tpu as pltpu

D_MODEL = 1024
D_CONV = D_MODEL // 2
D_POOL = D_MODEL // 2
D_SC = D_MODEL // 2
CONV_A_K = 31
SC_K = 3
POOL_WINDOWS = (2, 4, 8, 16)
POOL_GROUP_DIM = D_POOL // len(POOL_WINDOWS)
POOL_OUT_DIM = D_MODEL // len(POOL_WINDOWS)
N_MEM = 256
XATTN_HEADS = 4
XATTN_HEAD_DIM = D_MODEL // XATTN_HEADS
MOE_GROUPS = 4
EXPERTS_PER_GROUP = 4
N_EXPERTS = MOE_GROUPS * EXPERTS_PER_GROUP
D_EXPERT = D_MODEL // 4
EPS = 1e-6

OFF_A = 0
OFF_POOL = 2 * D_CONV
OFF_C = OFF_POOL + D_POOL
OFF_G = OFF_C + 3 * D_SC
D_IN_PROJ = OFF_G + 3 * D_MODEL

SUBLANES = 8
LANES = 128
VMEM_LIMIT_BYTES = 56 * 1024 * 1024

HIST_A = 32
HIST_POOL = 16
HIST_C = 8

SEQ_TILE = 256
MOE_TOKEN_TILE = 1024
ROUTER_LANES = LANES

BF16 = jnp.bfloat16
F32 = jnp.float32


def _sigmoid(v):
    return 0.5 * jnp.tanh(0.5 * v) + 0.5


def _rmsnorm(xf, g):
    return xf * lax.rsqrt(jnp.mean(xf * xf, axis=-1, keepdims=True) + EPS) * g


def _dot(a, b):
    return jnp.dot(a, b, preferred_element_type=F32)


def _const_spec(shape):
    nd = len(shape)
    return pl.BlockSpec(shape, lambda *_: (0,) * nd, pipeline_mode=pl.Buffered(1))


def _mixer_kernel(x_ref, g_ref, w_in_ref, caw_ref, cab_ref, lng_ref, lnb_ref, w_a_out_ref,
                  w_pool_ref, pool_scale_ref, ccw_ref, w_c_out_ref, w_o_ref,
                  o_ref, abuf, ubuf, vbuf):
    ts = x_ref.shape[1]
    s = pl.program_id(1)

    @pl.when(s == 0)
    def _():
        abuf[0:HIST_A, :] = jnp.zeros((HIST_A, D_CONV), F32)
        ubuf[0:HIST_POOL, :] = jnp.zeros((HIST_POOL, D_POOL), F32)
        vbuf[0:HIST_C, :] = jnp.zeros((HIST_C, D_SC), F32)

    x = x_ref[0]
    h = _rmsnorm(x, g_ref[...]).astype(BF16)

    pa = _dot(h, w_in_ref[:, OFF_A:OFF_A + 2 * D_CONV])
    abuf[HIST_A:HIST_A + ts, :] = pa[:, :D_CONV] * _sigmoid(pa[:, D_CONV:])
    acc = jnp.broadcast_to(cab_ref[...], (ts, D_CONV))
    for k in range(CONV_A_K):
        off = HIST_A - (CONV_A_K - 1) + k
        acc = acc + caw_ref[k:k + 1, :] * abuf[off:off + ts, :]
    abuf[0:HIST_A, :] = abuf[ts:ts + HIST_A, :]
    mu = jnp.mean(acc, axis=-1, keepdims=True)
    cen = acc - mu
    var = jnp.mean(cen * cen, axis=-1, keepdims=True)
    ln = cen * lax.rsqrt(var + EPS) * lng_ref[...] + lnb_ref[...]
    a_out = _dot((ln * _sigmoid(ln)).astype(BF16), w_a_out_ref[...])

    pu = _dot(h, w_in_ref[:, OFF_POOL:OFF_POOL + D_POOL])
    ubuf[HIST_POOL:HIST_POOL + ts, :] = pu
    t_glob = s * ts + lax.broadcasted_iota(jnp.int32, (ts, 1), 0)
    p_parts = []
    for i, w in enumerate(POOL_WINDOWS):
        c0, c1 = i * POOL_GROUP_DIM, (i + 1) * POOL_GROUP_DIM
        tok = pu[:, c0:c1]
        win = tok
        for j in range(1, w):
            win = win + ubuf[HIST_POOL - j:HIST_POOL - j + ts, c0:c1]
        cnt = jnp.minimum(t_glob + 1, w).astype(F32)
        pin = (win / cnt - tok).astype(BF16)
        p_parts.append(_dot(pin, w_pool_ref[i]))
    ubuf[0:HIST_POOL, :] = ubuf[ts:ts + HIST_POOL, :]
    p_out = jnp.concatenate(p_parts, axis=-1) * pool_scale_ref[...]

    pc = _dot(h, w_in_ref[:, OFF_C:OFF_C + 3 * D_SC])
    c_b = pc[:, D_SC:2 * D_SC]
    v = pc[:, 2 * D_SC:] * pc[:, :D_SC]
    vbuf[HIST_C:HIST_C + ts, :] = v
    conv_c = ccw_ref[SC_K - 1:SC_K, :] * v
    for k in range(SC_K - 1):
        off = HIST_C - (SC_K - 1) + k
        conv_c = conv_c + ccw_ref[k:k + 1, :] * vbuf[off:off + ts, :]
    vbuf[0:HIST_C, :] = vbuf[ts:ts + HIST_C, :]
    c_out = _dot((c_b * conv_c).astype(BF16), w_c_out_ref[...])

    merged = _sigmoid(_dot(h, w_in_ref[:, OFF_G:OFF_G + D_MODEL])) * a_out
    merged = merged + _sigmoid(_dot(h, w_in_ref[:, OFF_G + D_MODEL:OFF_G + 2 * D_MODEL])) * p_out
    merged = merged + _sigmoid(_dot(h, w_in_ref[:, OFF_G + 2 * D_MODEL:OFF_G + 3 * D_MODEL])) * c_out
    o_ref[0] = x + _dot(merged.astype(BF16), w_o_ref[...])


def _mixer(x, g, w_in, caw, cab, lng, lnb, w_a_out, w_pool, pool_scale, ccw, w_c_out, w_o):
    b, s, d = x.shape
    ts = SEQ_TILE
    consts = (g, w_in, caw, cab, lng, lnb, w_a_out, w_pool, pool_scale, ccw, w_c_out, w_o)
    return pl.pallas_call(
        _mixer_kernel,
        out_shape=jax.ShapeDtypeStruct(x.shape, F32),
        grid=(b, s // ts),
        in_specs=[pl.BlockSpec((1, ts, d), lambda i, j: (i, j, 0))]
        + [_const_spec(c.shape) for c in consts],
        out_specs=pl.BlockSpec((1, ts, d), lambda i, j: (i, j, 0)),
        scratch_shapes=[pltpu.VMEM((HIST_A + ts, D_CONV), F32),
                        pltpu.VMEM((HIST_POOL + ts, D_POOL), F32),
                        pltpu.VMEM((HIST_C + ts, D_SC), F32)],
        compiler_params=pltpu.CompilerParams(
            dimension_semantics=("arbitrary", "arbitrary"),
            vmem_limit_bytes=VMEM_LIMIT_BYTES),
        name="mixer",
    )(x, *consts)


def _kv_kernel(mem_ref, g_ref, w_kv_ref, kt_ref, v_ref):
    mn = _rmsnorm(mem_ref[0], g_ref[...]).astype(BF16)
    kv = _dot(mn, w_kv_ref[...])
    kt_ref[0] = kv[:, :D_MODEL].T.astype(BF16)
    v_ref[0] = kv[:, D_MODEL:].astype(BF16)


def _kv_proj(mem, g, w_kv):
    b, m, d = mem.shape
    return pl.pallas_call(
        _kv_kernel,
        out_shape=(jax.ShapeDtypeStruct((b, d, m), BF16), jax.ShapeDtypeStruct((b, m, d), BF16)),
        grid=(b,),
        in_specs=[pl.BlockSpec((1, m, d), lambda i: (i, 0, 0)),
                  _const_spec(g.shape), _const_spec(w_kv.shape)],
        out_specs=(pl.BlockSpec((1, d, m), lambda i: (i, 0, 0)),
                   pl.BlockSpec((1, m, d), lambda i: (i, 0, 0))),
        compiler_params=pltpu.CompilerParams(
            dimension_semantics=("arbitrary",), vmem_limit_bytes=VMEM_LIMIT_BYTES),
        name="kv_proj",
    )(mem, g, w_kv)


def _xattn_kernel(x_ref, g_ref, w_q_ref, kt_ref, v_ref, w_o_ref, o_ref):
    x = x_ref[0]
    h = _rmsnorm(x, g_ref[...]).astype(BF16)
    q = (_dot(h, w_q_ref[...]) * (XATTN_HEAD_DIM ** -0.5)).astype(BF16)
    heads = []
    for hd in range(XATTN_HEADS):
        c0, c1 = hd * XATTN_HEAD_DIM, (hd + 1) * XATTN_HEAD_DIM
        sc = _dot(q[:, c0:c1], kt_ref[0, c0:c1, :])
        e = jnp.exp(sc - jnp.max(sc, axis=-1, keepdims=True))
        denom = jnp.sum(e, axis=-1, keepdims=True)
        heads.append(_dot(e.astype(BF16), v_ref[0, :, c0:c1]) / denom)
    o = jnp.concatenate(heads, axis=-1).astype(BF16)
    o_ref[0] = x + _dot(o, w_o_ref[...])


def _xattn(x, g, w_q, kt, v, w_o):
    b, s, d = x.shape
    ts = SEQ_TILE
    return pl.pallas_call(
        _xattn_kernel,
        out_shape=jax.ShapeDtypeStruct(x.shape, F32),
        grid=(b, s // ts),
        in_specs=[pl.BlockSpec((1, ts, d), lambda i, j: (i, j, 0)),
                  _const_spec(g.shape), _const_spec(w_q.shape),
                  pl.BlockSpec((1, d, N_MEM), lambda i, j: (i, 0, 0)),
                  pl.BlockSpec((1, N_MEM, d), lambda i, j: (i, 0, 0)),
                  _const_spec(w_o.shape)],
        out_specs=pl.BlockSpec((1, ts, d), lambda i, j: (i, j, 0)),
        compiler_params=pltpu.CompilerParams(
            dimension_semantics=("arbitrary", "arbitrary"),
            vmem_limit_bytes=VMEM_LIMIT_BYTES),
        name="xattn",
    )(x, g, w_q, kt, v, w_o)


def _route(logits):
    lane = lax.broadcasted_iota(jnp.int32, logits.shape, 1)
    neg = jnp.float32(-jnp.inf)
    big = jnp.int32(ROUTER_LANES)
    gl = jnp.where(lane < MOE_GROUPS, logits, neg)
    gmax = jnp.max(gl, axis=-1, keepdims=True)
    g_idx = jnp.min(jnp.where(gl == gmax, lane, big), axis=-1, keepdims=True)
    g_val = 1.0 / jnp.sum(jnp.exp(gl - gmax), axis=-1, keepdims=True)
    lo = MOE_GROUPS + g_idx * EXPERTS_PER_GROUP
    el = jnp.where((lane >= lo) & (lane < lo + EXPERTS_PER_GROUP), logits, neg)
    m1 = jnp.max(el, axis=-1, keepdims=True)
    i1 = jnp.min(jnp.where(el == m1, lane, big), axis=-1, keepdims=True)
    el2 = jnp.where(lane == i1, neg, el)
    m2 = jnp.max(el2, axis=-1, keepdims=True)
    i2 = jnp.min(jnp.where(el2 == m2, lane, big), axis=-1, keepdims=True)
    esum = jnp.sum(jnp.exp(el - m1), axis=-1, keepdims=True)
    p1 = 1.0 / esum
    p2 = jnp.exp(m2 - m1) / esum
    w1 = g_val * (p1 / (p1 + p2))
    w2 = g_val * (p2 / (p1 + p2))
    return jnp.where(lane == i1, w1, 0.0) + jnp.where(lane == i2, w2, 0.0)


def _moe_kernel(x_ref, g_ref, w_r_ref, b_r_ref, wg_ref, wu_ref, wd_ref, o_ref, h_ref, gate_ref):
    e = pl.program_id(1)

    @pl.when(e == 0)
    def _():
        x = x_ref[...]
        hf = _rmsnorm(x, g_ref[...])
        h_ref[...] = hf.astype(BF16)
        logits = jnp.dot(hf, w_r_ref[...], preferred_element_type=F32,
                         precision=lax.Precision.HIGHEST) + b_r_ref[...]
        gate_ref[...] = _route(logits)
        o_ref[...] = x

    h = h_ref[...]
    lane = lax.broadcasted_iota(jnp.int32, gate_ref.shape, 1)
    ge = jnp.sum(jnp.where(lane == MOE_GROUPS + e, gate_ref[...], 0.0), axis=-1, keepdims=True)
    gt = _dot(h, wg_ref[0])
    act = gt * _sigmoid(gt) * _dot(h, wu_ref[0]) * ge
    o_ref[...] += _dot(act.astype(BF16), wd_ref[0])


def _moe(x2, g, w_r, b_r, wg, wu, wd):
    n, d = x2.shape
    tm = MOE_TOKEN_TILE
    return pl.pallas_call(
        _moe_kernel,
        out_shape=jax.ShapeDtypeStruct(x2.shape, F32),
        grid=(n // tm, N_EXPERTS),
        in_specs=[pl.BlockSpec((tm, d), lambda i, e: (i, 0)),
                  _const_spec(g.shape), _const_spec(w_r.shape), _const_spec(b_r.shape),
                  pl.BlockSpec((1, d, D_EXPERT), lambda i, e: (e, 0, 0)),
                  pl.BlockSpec((1, d, D_EXPERT), lambda i, e: (e, 0, 0)),
                  pl.BlockSpec((1, D_EXPERT, d), lambda i, e: (e, 0, 0))],
        out_specs=pl.BlockSpec((tm, d), lambda i, e: (i, 0)),
        scratch_shapes=[pltpu.VMEM((tm, d), BF16), pltpu.VMEM((tm, ROUTER_LANES), F32)],
        compiler_params=pltpu.CompilerParams(
            dimension_semantics=("arbitrary", "arbitrary"),
            vmem_limit_bytes=VMEM_LIMIT_BYTES),
        name="moe",
    )(x2, g, w_r, b_r, wg, wu, wd)


def _final_norm_kernel(x_ref, g_ref, o_ref):
    o_ref[...] = _rmsnorm(x_ref[...], g_ref[...])


def _final_norm(x2, g):
    n, d = x2.shape
    tm = MOE_TOKEN_TILE
    return pl.pallas_call(
        _final_norm_kernel,
        out_shape=jax.ShapeDtypeStruct(x2.shape, F32),
        grid=(n // tm,),
        in_specs=[pl.BlockSpec((tm, d), lambda i: (i, 0)), _const_spec(g.shape)],
        out_specs=pl.BlockSpec((tm, d), lambda i: (i, 0)),
        compiler_params=pltpu.CompilerParams(dimension_semantics=("arbitrary",)),
        name="final_norm",
    )(x2, g)


def _row(v):
    return v.reshape(1, -1)


def kernel(x, mem, norm_mix_g, w_in, conv_a_w, conv_a_b, ln_a_g, ln_a_b, w_a_out, w_pool_grp, pool_scale, conv_c_w, w_c_out, w_o, norm_x_g, norm_mem_g, w_xq, w_xkv, w_xo, norm_ffn_g, w_rg, b_rg, w_re, b_re, w_e_gate, w_e_up, w_e_down, norm_f_g):
    bsz, seq, d = x.shape
    depth = w_in.shape[0]
    pad = ROUTER_LANES - MOE_GROUPS - N_EXPERTS
    for l in range(depth):
        x = _mixer(x, _row(norm_mix_g[l]), w_in[l].astype(BF16), conv_a_w[l], _row(conv_a_b[l]),
                   _row(ln_a_g[l]), _row(ln_a_b[l]), w_a_out[l].astype(BF16),
                   w_pool_grp[l].astype(BF16), _row(pool_scale[l]), conv_c_w[l],
                   w_c_out[l].astype(BF16), w_o[l].astype(BF16))
        kt, v = _kv_proj(mem, _row(norm_mem_g[l]), w_xkv[l].astype(BF16))
        x = _xattn(x, _row(norm_x_g[l]), w_xq[l].astype(BF16), kt, v, w_xo[l].astype(BF16))
        w_r = jnp.pad(jnp.concatenate([w_rg[l], w_re[l]], axis=1), ((0, 0), (0, pad)))
        b_r = jnp.pad(jnp.concatenate([b_rg[l], b_re[l]]), (0, pad)).reshape(1, -1)
        x = _moe(x.reshape(bsz * seq, d), _row(norm_ffn_g[l]), w_r, b_r,
                 w_e_gate[l].astype(BF16), w_e_up[l].astype(BF16),
                 w_e_down[l].astype(BF16)).reshape(bsz, seq, d)
    return _final_norm(x.reshape(bsz * seq, d), _row(norm_f_g)).reshape(bsz, seq, d)
```

```python
import functools

import jax
import jax.numpy as jnp
from jax import lax
from jax.experimental import pallas as pl
from jax.experimental.pallas import tpu as pltpu

D_MODEL = 1024
D_CONV = D_MODEL // 2
D_POOL = D_MODEL // 2
D_SC = D_MODEL // 2
CONV_A_K = 31
SC_K = 3
POOL_WINDOWS = (2, 4, 8, 16)
POOL_GROUP_DIM = D_POOL // len(POOL_WINDOWS)
POOL_OUT_DIM = D_MODEL // len(POOL_WINDOWS)
N_MEM = 256
XATTN_HEADS = 4
XATTN_HEAD_DIM = D_MODEL // XATTN_HEADS
MOE_GROUPS = 4
EXPERTS_PER_GROUP = 4
N_EXPERTS = MOE_GROUPS * EXPERTS_PER_GROUP
D_EXPERT = D_MODEL // 4
EPS = 1e-6

OFF_A = 0
OFF_POOL = 2 * D_CONV
OFF_C = OFF_POOL + D_POOL
OFF_G = OFF_C + 3 * D_SC
D_IN_PROJ = OFF_G + 3 * D_MODEL

SUBLANES = 8
LANES = 128
VMEM_LIMIT_BYTES = 56 * 1024 * 1024

HIST_A = 32
HIST_POOL = 16
HIST_C = 8

SEQ_TILE = 256
ROUTER_LANES = LANES

PAIRS_PER_GROUP = EXPERTS_PER_GROUP * (EXPERTS_PER_GROUP - 1) // 2
N_CLASSES = MOE_GROUPS * PAIRS_PER_GROUP
EXPERT_ROW_TILE = 128
EXPERT_STEP_TILES = 4
PLAN_TILE = 512
ROW_MOVE_TILE = 512
VMEM_LIMIT_RESIDENT_BYTES = 58 * 1024 * 1024
INFO_CLS, INFO_WA, INFO_WB = 0, 1, 2

BF16 = jnp.bfloat16
F32 = jnp.float32


def _sorted_rows(n_tokens):
    return n_tokens + N_CLASSES * EXPERT_ROW_TILE


def _sigmoid(v):
    return 0.5 * jnp.tanh(0.5 * v) + 0.5


def _rmsnorm(xf, g):
    return xf * lax.rsqrt(jnp.mean(xf * xf, axis=-1, keepdims=True) + EPS) * g


def _dot(a, b):
    return jnp.dot(a, b, preferred_element_type=F32)


def _const_spec(shape):
    nd = len(shape)
    return pl.BlockSpec(shape, lambda *_: (0,) * nd, pipeline_mode=pl.Buffered(1))


def _unpack_pair(q):
    return jnp.concatenate(
        [pltpu.unpack_elementwise(q, index=0, packed_dtype=BF16, unpacked_dtype=F32),
         pltpu.unpack_elementwise(q, index=1, packed_dtype=BF16, unpacked_dtype=F32)], axis=-1)


def _pack_pair(v):
    half = v.shape[-1] // 2
    return pltpu.pack_elementwise([v[:, :half], v[:, half:]], packed_dtype=BF16)


def _mixer_kernel(x_ref, g_ref, w_in_ref, caw_ref, cab_ref, lng_ref, lnb_ref, w_a_out_ref,
                  w_pool_ref, pool_scale_ref, ccw_ref, w_c_out_ref, w_o_ref,
                  o_ref, abuf, ubuf, vbuf):
    ts = x_ref.shape[1]
    s = pl.program_id(1)

    @pl.when(s == 0)
    def _():
        abuf[0:HIST_A, :] = jnp.zeros((HIST_A, D_CONV), F32)
        ubuf[0:HIST_POOL, :] = jnp.zeros((HIST_POOL, D_POOL), F32)
        vbuf[0:HIST_C, :] = jnp.zeros((HIST_C, D_SC), F32)

    x = x_ref[0]
    h = _rmsnorm(x, g_ref[...]).astype(BF16)

    pa = _dot(h, w_in_ref[:, OFF_A:OFF_A + 2 * D_CONV])
    abuf[HIST_A:HIST_A + ts, :] = pa[:, :D_CONV] * _sigmoid(pa[:, D_CONV:])
    acc = jnp.broadcast_to(cab_ref[...], (ts, D_CONV))
    for k in range(CONV_A_K):
        off = HIST_A - (CONV_A_K - 1) + k
        acc = acc + caw_ref[k:k + 1, :] * abuf[off:off + ts, :]
    abuf[0:HIST_A, :] = abuf[ts:ts + HIST_A, :]
    mu = jnp.mean(acc, axis=-1, keepdims=True)
    cen = acc - mu
    var = jnp.mean(cen * cen, axis=-1, keepdims=True)
    ln = cen * lax.rsqrt(var + EPS) * lng_ref[...] + lnb_ref[...]
    a_out = _dot((ln * _sigmoid(ln)).astype(BF16), w_a_out_ref[...])

    pu = _dot(h, w_in_ref[:, OFF_POOL:OFF_POOL + D_POOL])
    ubuf[HIST_POOL:HIST_POOL + ts, :] = pu
    t_glob = s * ts + lax.broadcasted_iota(jnp.int32, (ts, 1), 0)
    p_parts = []
    for i, w in enumerate(POOL_WINDOWS):
        c0, c1 = i * POOL_GROUP_DIM, (i + 1) * POOL_GROUP_DIM
        tok = pu[:, c0:c1]
        win = tok
        for j in range(1, w):
            win = win + ubuf[HIST_POOL - j:HIST_POOL - j + ts, c0:c1]
        cnt = jnp.minimum(t_glob + 1, w).astype(F32)
        pin = (win / cnt - tok).astype(BF16)
        p_parts.append(_dot(pin, w_pool_ref[i]))
    ubuf[0:HIST_POOL, :] = ubuf[ts:ts + HIST_POOL, :]
    p_out = jnp.concatenate(p_parts, axis=-1) * pool_scale_ref[...]

    pc = _dot(h, w_in_ref[:, OFF_C:OFF_C + 3 * D_SC])
    c_b = pc[:, D_SC:2 * D_SC]
    v = pc[:, 2 * D_SC:] * pc[:, :D_SC]
    vbuf[HIST_C:HIST_C + ts, :] = v
    conv_c = ccw_ref[SC_K - 1:SC_K, :] * v
    for k in range(SC_K - 1):
        off = HIST_C - (SC_K - 1) + k
        conv_c = conv_c + ccw_ref[k:k + 1, :] * vbuf[off:off + ts, :]
    vbuf[0:HIST_C, :] = vbuf[ts:ts + HIST_C, :]
    c_out = _dot((c_b * conv_c).astype(BF16), w_c_out_ref[...])

    merged = _sigmoid(_dot(h, w_in_ref[:, OFF_G:OFF_G + D_MODEL])) * a_out
    merged = merged + _sigmoid(_dot(h, w_in_ref[:, OFF_G + D_MODEL:OFF_G + 2 * D_MODEL])) * p_out
    merged = merged + _sigmoid(_dot(h, w_in_ref[:, OFF_G + 2 * D_MODEL:OFF_G + 3 * D_MODEL])) * c_out
    o_ref[0] = x + _dot(merged.astype(BF16), w_o_ref[...])


def _mixer(x, g, w_in, caw, cab, lng, lnb, w_a_out, w_pool, pool_scale, ccw, w_c_out, w_o):
    b, s, d = x.shape
    ts = SEQ_TILE
    consts = (g, w_in, caw, cab, lng, lnb, w_a_out, w_pool, pool_scale, ccw, w_c_out, w_o)
    return pl.pallas_call(
        _mixer_kernel,
        out_shape=jax.ShapeDtypeStruct(x.shape, F32),
        grid=(b, s // ts),
        in_specs=[pl.BlockSpec((1, ts, d), lambda i, j: (i, j, 0))]
        + [_const_spec(c.shape) for c in consts],
        out_specs=pl.BlockSpec((1, ts, d), lambda i, j: (i, j, 0)),
        scratch_shapes=[pltpu.VMEM((HIST_A + ts, D_CONV), F32),
                        pltpu.VMEM((HIST_POOL + ts, D_POOL), F32),
                        pltpu.VMEM((HIST_C + ts, D_SC), F32)],
        compiler_params=pltpu.CompilerParams(
            dimension_semantics=("arbitrary", "arbitrary"),
            vmem_limit_bytes=VMEM_LIMIT_BYTES),
        name="mixer",
    )(x, *consts)


def _kv_kernel(mem_ref, g_ref, w_kv_ref, kt_ref, v_ref):
    mn = _rmsnorm(mem_ref[0], g_ref[...]).astype(BF16)
    kv = _dot(mn, w_kv_ref[...])
    kt_ref[0] = kv[:, :D_MODEL].T.astype(BF16)
    v_ref[0] = kv[:, D_MODEL:].astype(BF16)


def _kv_proj(mem, g, w_kv):
    b, m, d = mem.shape
    return pl.pallas_call(
        _kv_kernel,
        out_shape=(jax.ShapeDtypeStruct((b, d, m), BF16), jax.ShapeDtypeStruct((b, m, d), BF16)),
        grid=(b,),
        in_specs=[pl.BlockSpec((1, m, d), lambda i: (i, 0, 0)),
                  _const_spec(g.shape), _const_spec(w_kv.shape)],
        out_specs=(pl.BlockSpec((1, d, m), lambda i: (i, 0, 0)),
                   pl.BlockSpec((1, m, d), lambda i: (i, 0, 0))),
        compiler_params=pltpu.CompilerParams(
            dimension_semantics=("arbitrary",), vmem_limit_bytes=VMEM_LIMIT_BYTES),
        name="kv_proj",
    )(mem, g, w_kv)


def _route(logits):
    lane = lax.broadcasted_iota(jnp.int32, logits.shape, 1)
    neg = jnp.float32(-jnp.inf)
    big = jnp.int32(ROUTER_LANES)
    gl = jnp.where(lane < MOE_GROUPS, logits, neg)
    gmax = jnp.max(gl, axis=-1, keepdims=True)
    g_idx = jnp.min(jnp.where(gl == gmax, lane, big), axis=-1, keepdims=True)
    g_val = 1.0 / jnp.sum(jnp.exp(gl - gmax), axis=-1, keepdims=True)
    base = MOE_GROUPS + g_idx * EXPERTS_PER_GROUP
    el = jnp.where((lane >= base) & (lane < base + EXPERTS_PER_GROUP), logits, neg)
    m1 = jnp.max(el, axis=-1, keepdims=True)
    i1 = jnp.min(jnp.where(el == m1, lane, big), axis=-1, keepdims=True)
    el2 = jnp.where(lane == i1, neg, el)
    m2 = jnp.max(el2, axis=-1, keepdims=True)
    i2 = jnp.min(jnp.where(el2 == m2, lane, big), axis=-1, keepdims=True)
    esum = jnp.sum(jnp.exp(el - m1), axis=-1, keepdims=True)
    p1 = 1.0 / esum
    p2 = jnp.exp(m2 - m1) / esum
    w1 = g_val * (p1 / (p1 + p2))
    w2 = g_val * (p2 / (p1 + p2))
    lo = jnp.minimum(i1, i2) - base
    hi = jnp.maximum(i1, i2) - base
    pair = ((lo * (2 * EXPERTS_PER_GROUP - 1 - lo)) >> 1) + (hi - lo - 1)
    cls = g_idx * PAIRS_PER_GROUP + pair
    first_is_low = i1 < i2
    return cls, jnp.where(first_is_low, w1, w2), jnp.where(first_is_low, w2, w1)


def _xattn_kernel(x_ref, g_ref, w_q_ref, kt_ref, v_ref, w_o_ref, gf_ref, w_r_ref, b_r_ref,
                  o_ref, hp_ref, info_ref, cnt_ref):
    x = x_ref[0]
    h = _rmsnorm(x, g_ref[...]).astype(BF16)
    q = (_dot(h, w_q_ref[...]) * (XATTN_HEAD_DIM ** -0.5)).astype(BF16)
    heads = []
    for hd in range(XATTN_HEADS):
        c0, c1 = hd * XATTN_HEAD_DIM, (hd + 1) * XATTN_HEAD_DIM
        sc = _dot(q[:, c0:c1], kt_ref[0, c0:c1, :])
        e = jnp.exp(sc - jnp.max(sc, axis=-1, keepdims=True))
        denom = jnp.sum(e, axis=-1, keepdims=True)
        heads.append(_dot(e.astype(BF16), v_ref[0, :, c0:c1]) / denom)
    o = jnp.concatenate(heads, axis=-1).astype(BF16)
    xn = x + _dot(o, w_o_ref[...])
    o_ref[0] = xn

    hf = _rmsnorm(xn, gf_ref[...])
    hp_ref[...] = _pack_pair(hf)
    logits = jnp.dot(hf, w_r_ref[...], preferred_element_type=F32,
                     precision=lax.Precision.HIGHEST) + b_r_ref[...]
    cls, wa, wb = _route(logits)
    lane = lax.broadcasted_iota(jnp.int32, logits.shape, 1)
    info_ref[...] = jnp.where(lane == INFO_CLS, cls.astype(F32),
                              jnp.where(lane == INFO_WA, wa, jnp.where(lane == INFO_WB, wb, 0.0)))
    onehot = (lane == cls).astype(F32)

    @pl.when((pl.program_id(0) == 0) & (pl.program_id(1) == 0))
    def _():
        cnt_ref[...] = jnp.zeros(cnt_ref.shape, F32)

    cnt_ref[...] += jnp.broadcast_to(jnp.sum(onehot, axis=0, keepdims=True), cnt_ref.shape)


def _xattn(x, g, w_q, kt, v, w_o, g_ffn, w_r, b_r):
    b, s, d = x.shape
    ts = SEQ_TILE
    nt = s // ts
    n = b * s
    return pl.pallas_call(
        _xattn_kernel,
        out_shape=(jax.ShapeDtypeStruct(x.shape, F32),
                   jax.ShapeDtypeStruct((n, d // 2), jnp.uint32),
                   jax.ShapeDtypeStruct((n, ROUTER_LANES), F32),
                   jax.ShapeDtypeStruct((SUBLANES, ROUTER_LANES), F32)),
        grid=(b, nt),
        in_specs=[pl.BlockSpec((1, ts, d), lambda i, j: (i, j, 0)),
                  _const_spec(g.shape), _const_spec(w_q.shape),
                  pl.BlockSpec((1, d, N_MEM), lambda i, j: (i, 0, 0)),
                  pl.BlockSpec((1, N_MEM, d), lambda i, j: (i, 0, 0)),
                  _const_spec(w_o.shape), _const_spec(g_ffn.shape),
                  _const_spec(w_r.shape), _const_spec(b_r.shape)],
        out_specs=(pl.BlockSpec((1, ts, d), lambda i, j: (i, j, 0)),
                   pl.BlockSpec((ts, d // 2), lambda i, j: (i * nt + j, 0)),
                   pl.BlockSpec((ts, ROUTER_LANES), lambda i, j: (i * nt + j, 0)),
                   pl.BlockSpec((SUBLANES, ROUTER_LANES), lambda i, j: (0, 0))),
        compiler_params=pltpu.CompilerParams(
            dimension_semantics=("arbitrary", "arbitrary"),
            vmem_limit_bytes=VMEM_LIMIT_BYTES),
        name="xattn",
    )(x, g, w_q, kt, v, w_o, g_ffn, w_r, b_r)


def _plan_kernel(info_ref, offs_ref, pos_ref, carry):
    t = info_ref.shape[0]

    @pl.when(pl.program_id(0) == 0)
    def _():
        carry[...] = jnp.zeros(carry.shape, F32)

    lane = lax.broadcasted_iota(jnp.int32, info_ref.shape, 1)
    cls = info_ref[:, INFO_CLS:INFO_CLS + 1].astype(jnp.int32)
    onehot = lane == cls
    r = lax.broadcasted_iota(jnp.int32, (t, t), 0)
    k = lax.broadcasted_iota(jnp.int32, (t, t), 1)
    before = (k < r).astype(BF16)
    rank = _dot(before, onehot.astype(BF16))
    dest = jnp.sum(jnp.where(onehot, rank + carry[...] + offs_ref[...], 0.0), axis=-1, keepdims=True)
    pos_ref[...] = jnp.broadcast_to(dest.astype(jnp.int32), pos_ref.shape)
    carry[...] += jnp.sum(onehot.astype(F32), axis=0, keepdims=True)


def _plan(info, offs):
    n = info.shape[0]
    t = PLAN_TILE
    return pl.pallas_call(
        _plan_kernel,
        out_shape=jax.ShapeDtypeStruct((n, ROUTER_LANES), jnp.int32),
        grid=(n // t,),
        in_specs=[pl.BlockSpec((t, ROUTER_LANES), lambda i: (i, 0)), _const_spec(offs.shape)],
        out_specs=pl.BlockSpec((t, ROUTER_LANES), lambda i: (i, 0)),
        scratch_shapes=[pltpu.VMEM((1, ROUTER_LANES), F32)],
        compiler_params=pltpu.CompilerParams(dimension_semantics=("arbitrary",)),
        name="moe_plan",
    )(info, offs)


def _scatter_kernel(pos_ref, hp_ref, info_ref, s_ref, w_ref):
    i = pl.program_id(0)
    tn = hp_ref.shape[0]

    @pl.when(i == 0)
    def _():
        zero_rows = _pack_pair(jnp.zeros((tn, 2 * s_ref.shape[1]), F32))

        def fill(c, carry):
            s_ref[pl.ds(pl.multiple_of(c * tn, tn), tn), :] = zero_rows
            return carry

        lax.fori_loop(0, s_ref.shape[0] // tn, fill, 0)
        w_ref[...] = jnp.zeros(w_ref.shape, w_ref.dtype)

    def body(j, carry):
        base = pl.multiple_of(j * SUBLANES, SUBLANES)
        hb = hp_ref[pl.ds(base, SUBLANES), :]
        ib = info_ref[pl.ds(base, SUBLANES), :]
        for k in range(SUBLANES):
            p = pos_ref[i * tn + base + k]
            s_ref[pl.ds(p, 1), :] = hb[k:k + 1, :]
            w_ref[pl.ds(p, 1), :] = ib[k:k + 1, :]
        return carry

    lax.fori_loop(0, tn // SUBLANES, body, 0)


def _scatter(pos, hp, info):
    n, hw = hp.shape
    tn = ROW_MOVE_TILE
    rows = _sorted_rows(n)
    grid_spec = pltpu.PrefetchScalarGridSpec(
        num_scalar_prefetch=1,
        grid=(n // tn,),
        in_specs=[pl.BlockSpec((tn, hw), lambda i, pos: (i, 0)),
                  pl.BlockSpec((tn, ROUTER_LANES), lambda i, pos: (i, 0))],
        out_specs=(pl.BlockSpec((rows, hw), lambda i, pos: (0, 0),
                                pipeline_mode=pl.Buffered(1)),
                   pl.BlockSpec((rows, ROUTER_LANES), lambda i, pos: (0, 0),
                                pipeline_mode=pl.Buffered(1))),
    )
    return pl.pallas_call(
        _scatter_kernel,
        out_shape=(jax.ShapeDtypeStruct((rows, hw), jnp.uint32),
                   jax.ShapeDtypeStruct((rows, ROUTER_LANES), F32)),
        grid_spec=grid_spec,
        compiler_params=pltpu.CompilerParams(
            dimension_semantics=("arbitrary",), vmem_limit_bytes=VMEM_LIMIT_RESIDENT_BYTES),
        name="moe_scatter",
    )(pos, hp, info)


def _expert_kernel(ea_ref, eb_ref, nu_ref, s_ref, w_ref, wg_ref, wu_ref, wd_ref, y_ref):
    i = pl.program_id(0)
    for j in range(EXPERT_STEP_TILES):
        t = i * EXPERT_STEP_TILES + j
        rows = slice(j * EXPERT_ROW_TILE, (j + 1) * EXPERT_ROW_TILE)

        @pl.when(t < nu_ref[0])
        def _():
            hb = _unpack_pair(s_ref[rows, :]).astype(BF16)
            wts = w_ref[rows, :]
            y = None
            for e_ref, col in ((ea_ref, INFO_WA), (eb_ref, INFO_WB)):
                e = e_ref[t]
                gt = _dot(hb, wg_ref[e])
                act = gt * _sigmoid(gt) * _dot(hb, wu_ref[e]) * wts[:, col:col + 1]
                part = _dot(act.astype(BF16), wd_ref[e])
                y = part if y is None else y + part
            y_ref[rows, :] = _pack_pair(y)

        @pl.when(t >= nu_ref[0])
        def _():
            y_ref[rows, :] = _pack_pair(jnp.zeros((EXPERT_ROW_TILE, 2 * y_ref.shape[1]), F32))


def _experts(tile_ea, tile_eb, n_used, s, w, wg, wu, wd):
    rows, hw = s.shape
    tb = EXPERT_ROW_TILE * EXPERT_STEP_TILES
    grid_spec = pltpu.PrefetchScalarGridSpec(
        num_scalar_prefetch=3,
        grid=(rows // tb,),
        in_specs=[pl.BlockSpec((tb, hw), lambda i, *_: (i, 0)),
                  pl.BlockSpec((tb, ROUTER_LANES), lambda i, *_: (i, 0)),
                  _const_spec(wg.shape), _const_spec(wu.shape), _const_spec(wd.shape)],
        out_specs=pl.BlockSpec((tb, hw), lambda i, *_: (i, 0)),
    )
    return pl.pallas_call(
        _expert_kernel,
        out_shape=jax.ShapeDtypeStruct((rows, hw), jnp.uint32),
        grid_spec=grid_spec,
        compiler_params=pltpu.CompilerParams(
            dimension_semantics=("arbitrary",), vmem_limit_bytes=VMEM_LIMIT_BYTES),
        name="moe_experts",
    )(tile_ea, tile_eb, n_used, s, w, wg, wu, wd)


def _gather_kernel(pos_ref, x_ref, y_ref, gn_ref, o_ref, rows_ref, *, final_norm):
    i = pl.program_id(0)
    tn = x_ref.shape[0]

    def body(j, carry):
        base = pl.multiple_of(j * SUBLANES, SUBLANES)
        for k in range(SUBLANES):
            p = pos_ref[i * tn + base + k]
            rows_ref[pl.ds(base + k, 1), :] = y_ref[pl.ds(p, 1), :]
        return carry

    lax.fori_loop(0, tn // SUBLANES, body, 0)
    out = x_ref[...] + _unpack_pair(rows_ref[...])
    if final_norm:
        out = _rmsnorm(out, gn_ref[...])
    o_ref[...] = out


def _gather(pos, x2, y, g_final, final_norm):
    n, d = x2.shape
    tn = ROW_MOVE_TILE
    grid_spec = pltpu.PrefetchScalarGridSpec(
        num_scalar_prefetch=1,
        grid=(n // tn,),
        in_specs=[pl.BlockSpec((tn, d), lambda i, pos: (i, 0)),
                  pl.BlockSpec(y.shape, lambda i, pos: (0, 0), pipeline_mode=pl.Buffered(1)),
                  pl.BlockSpec(g_final.shape, lambda i, pos: (0, 0), pipeline_mode=pl.Buffered(1))],
        out_specs=pl.BlockSpec((tn, d), lambda i, pos: (i, 0)),
        scratch_shapes=[pltpu.VMEM((tn, y.shape[1]), jnp.uint32)],
    )
    return pl.pallas_call(
        functools.partial(_gather_kernel, final_norm=final_norm),
        out_shape=jax.ShapeDtypeStruct(x2.shape, F32),
        grid_spec=grid_spec,
        compiler_params=pltpu.CompilerParams(
            dimension_semantics=("arbitrary",), vmem_limit_bytes=VMEM_LIMIT_RESIDENT_BYTES),
        name="moe_gather",
    )(pos, x2, y, g_final)


def _moe_tables(counts, n_tokens):
    tiles = jnp.floor((counts[0, :N_CLASSES] + (EXPERT_ROW_TILE - 1)) / EXPERT_ROW_TILE)
    ci = jnp.arange(N_CLASSES, dtype=jnp.int32)
    end_tile_f = jnp.sum(jnp.where(ci[None, :] <= ci[:, None], tiles[None, :], 0.0), axis=1)
    offs = jnp.pad((end_tile_f - tiles) * EXPERT_ROW_TILE,
                   (0, ROUTER_LANES - N_CLASSES)).reshape(1, ROUTER_LANES)
    end_tile = end_tile_f.astype(jnp.int32)
    n_tiles = _sorted_rows(n_tokens) // EXPERT_ROW_TILE
    ti = jnp.arange(n_tiles, dtype=jnp.int32)
    tile_cls = jnp.minimum(
        jnp.sum((end_tile[None, :] <= ti[:, None]).astype(jnp.int32), axis=1), N_CLASSES - 1)
    grp = sum((tile_cls >= g * PAIRS_PER_GROUP).astype(jnp.int32) for g in range(1, MOE_GROUPS))
    pair = tile_cls - grp * PAIRS_PER_GROUP
    pair_lo = (pair >= 3).astype(jnp.int32) + (pair >= 5).astype(jnp.int32)
    pair_hi = pair + 1 - jnp.where(pair >= 5, 3, jnp.where(pair >= 3, 2, 0))
    tile_ea = grp * EXPERTS_PER_GROUP + pair_lo
    tile_eb = grp * EXPERTS_PER_GROUP + pair_hi
    return offs, tile_ea, tile_eb, end_tile[N_CLASSES - 1:].astype(jnp.int32)


def _row(v):
    return v.reshape(1, -1)


def kernel(x, mem, norm_mix_g, w_in, conv_a_w, conv_a_b, ln_a_g, ln_a_b, w_a_out, w_pool_grp, pool_scale, conv_c_w, w_c_out, w_o, norm_x_g, norm_mem_g, w_xq, w_xkv, w_xo, norm_ffn_g, w_rg, b_rg, w_re, b_re, w_e_gate, w_e_up, w_e_down, norm_f_g):
    bsz, seq, d = x.shape
    n = bsz * seq
    depth = w_in.shape[0]
    pad = ROUTER_LANES - MOE_GROUPS - N_EXPERTS
    for l in range(depth):
        x = _mixer(x, _row(norm_mix_g[l]), w_in[l].astype(BF16), conv_a_w[l], _row(conv_a_b[l]),
                   _row(ln_a_g[l]), _row(ln_a_b[l]), w_a_out[l].astype(BF16),
                   w_pool_grp[l].astype(BF16), _row(pool_scale[l]), conv_c_w[l],
                   w_c_out[l].astype(BF16), w_o[l].astype(BF16))
        kt, v = _kv_proj(mem, _row(norm_mem_g[l]), w_xkv[l].astype(BF16))
        w_r = jnp.pad(jnp.concatenate([w_rg[l], w_re[l]], axis=1), ((0, 0), (0, pad)))
        b_r = jnp.pad(jnp.concatenate([b_rg[l], b_re[l]]), (0, pad)).reshape(1, -1)
        x, hp, info, counts = _xattn(x, _row(norm_x_g[l]), w_xq[l].astype(BF16), kt, v,
                                     w_xo[l].astype(BF16), _row(norm_ffn_g[l]), w_r, b_r)
        offs, tile_ea, tile_eb, n_used = _moe_tables(counts, n)
        pos = _plan(info, offs)[:, 0]
        s_rows, s_wts = _scatter(pos, hp, info)
        y = _experts(tile_ea, tile_eb, n_used, s_rows, s_wts, w_e_gate[l].astype(BF16),
                     w_e_up[l].astype(BF16), w_e_down[l].astype(BF16))
        x = _gather(pos, x.reshape(n, d), y, _row(norm_f_g),
                    final_norm=(l == depth - 1)).reshape(bsz, seq, d)
    return x
```

```python
import functools

import jax
import jax.numpy as jnp
from jax import lax
from jax.experimental import pallas as pl
from jax.experimental.pallas import tpu as pltpu

D_MODEL = 1024
D_CONV = D_MODEL // 2
D_POOL = D_MODEL // 2
D_SC = D_MODEL // 2
CONV_A_K = 31
SC_K = 3
POOL_WINDOWS = (2, 4, 8, 16)
POOL_GROUP_DIM = D_POOL // len(POOL_WINDOWS)
POOL_OUT_DIM = D_MODEL // len(POOL_WINDOWS)
N_MEM = 256
XATTN_HEADS = 4
XATTN_HEAD_DIM = D_MODEL // XATTN_HEADS
MOE_GROUPS = 4
EXPERTS_PER_GROUP = 4
N_EXPERTS = MOE_GROUPS * EXPERTS_PER_GROUP
D_EXPERT = D_MODEL // 4
EPS = 1e-6

OFF_A = 0
OFF_POOL = 2 * D_CONV
OFF_C = OFF_POOL + D_POOL
OFF_G = OFF_C + 3 * D_SC
D_IN_PROJ = OFF_G + 3 * D_MODEL

SUBLANES = 8
LANES = 128
VMEM_LIMIT_BYTES = 56 * 1024 * 1024

HIST_A = 32
HIST_POOL = 16
HIST_C = 8

SEQ_TILE = 256
ROUTER_LANES = LANES

PAIRS_PER_GROUP = EXPERTS_PER_GROUP * (EXPERTS_PER_GROUP - 1) // 2
N_CLASSES = MOE_GROUPS * PAIRS_PER_GROUP
EXPERT_ROW_TILE = 128
EXPERT_STEP_TILES = 4
ROUTER_ROWS = 32
ROW_MOVE_TILE = 512
VMEM_LIMIT_RESIDENT_BYTES = 58 * 1024 * 1024
INFO_WA, INFO_WB = 0, 1

BF16 = jnp.bfloat16
F32 = jnp.float32


def _sorted_rows(n_tokens):
    return n_tokens + N_CLASSES * EXPERT_ROW_TILE


def _sigmoid(v):
    return 0.5 * jnp.tanh(0.5 * v) + 0.5


def _rmsnorm(xf, g):
    return xf * lax.rsqrt(jnp.mean(xf * xf, axis=-1, keepdims=True) + EPS) * g


def _dot(a, b):
    return jnp.dot(a, b, preferred_element_type=F32)


def _const_spec(shape):
    nd = len(shape)
    return pl.BlockSpec(shape, lambda *_: (0,) * nd, pipeline_mode=pl.Buffered(1))


def _unpack_pair(q):
    return jnp.concatenate(
        [pltpu.unpack_elementwise(q, index=0, packed_dtype=BF16, unpacked_dtype=F32),
         pltpu.unpack_elementwise(q, index=1, packed_dtype=BF16, unpacked_dtype=F32)], axis=-1)


def _pack_pair(v):
    half = v.shape[-1] // 2
    return pltpu.pack_elementwise([v[:, :half], v[:, half:]], packed_dtype=BF16)


def _mixer_kernel(x_ref, g_ref, w_in_ref, caw_ref, cab_ref, lng_ref, lnb_ref, w_a_out_ref,
                  w_pool_ref, pool_scale_ref, ccw_ref, w_c_out_ref, w_o_ref,
                  o_ref, abuf, ubuf, vbuf):
    ts = x_ref.shape[1]
    s = pl.program_id(1)

    @pl.when(s == 0)
    def _():
        abuf[0:HIST_A, :] = jnp.zeros((HIST_A, D_CONV), F32)
        ubuf[0:HIST_POOL, :] = jnp.zeros((HIST_POOL, D_POOL), F32)
        vbuf[0:HIST_C, :] = jnp.zeros((HIST_C, D_SC), F32)

    x = x_ref[0]
    h = _rmsnorm(x, g_ref[...]).astype(BF16)

    pa = _dot(h, w_in_ref[:, OFF_A:OFF_A + 2 * D_CONV])
    abuf[HIST_A:HIST_A + ts, :] = pa[:, :D_CONV] * _sigmoid(pa[:, D_CONV:])
    acc = jnp.broadcast_to(cab_ref[...], (ts, D_CONV))
    for k in range(CONV_A_K):
        off = HIST_A - (CONV_A_K - 1) + k
        acc = acc + caw_ref[k:k + 1, :] * abuf[off:off + ts, :]
    abuf[0:HIST_A, :] = abuf[ts:ts + HIST_A, :]
    mu = jnp.mean(acc, axis=-1, keepdims=True)
    cen = acc - mu
    var = jnp.mean(cen * cen, axis=-1, keepdims=True)
    ln = cen * lax.rsqrt(var + EPS) * lng_ref[...] + lnb_ref[...]
    a_out = _dot((ln * _sigmoid(ln)).astype(BF16), w_a_out_ref[...])

    pu = _dot(h, w_in_ref[:, OFF_POOL:OFF_POOL + D_POOL])
    ubuf[HIST_POOL:HIST_POOL + ts, :] = pu
    t_glob = s * ts + lax.broadcasted_iota(jnp.int32, (ts, 1), 0)
    p_parts = []
    for i, w in enumerate(POOL_WINDOWS):
        c0, c1 = i * POOL_GROUP_DIM, (i + 1) * POOL_GROUP_DIM
        tok = pu[:, c0:c1]
        win = tok
        for j in range(1, w):
            win = win + ubuf[HIST_POOL - j:HIST_POOL - j + ts, c0:c1]
        cnt = jnp.minimum(t_glob + 1, w).astype(F32)
        pin = (win / cnt - tok).astype(BF16)
        p_parts.append(_dot(pin, w_pool_ref[i]))
    ubuf[0:HIST_POOL, :] = ubuf[ts:ts + HIST_POOL, :]
    p_out = jnp.concatenate(p_parts, axis=-1) * pool_scale_ref[...]

    pc = _dot(h, w_in_ref[:, OFF_C:OFF_C + 3 * D_SC])
    c_b = pc[:, D_SC:2 * D_SC]
    v = pc[:, 2 * D_SC:] * pc[:, :D_SC]
    vbuf[HIST_C:HIST_C + ts, :] = v
    conv_c = ccw_ref[SC_K - 1:SC_K, :] * v
    for k in range(SC_K - 1):
        off = HIST_C - (SC_K - 1) + k
        conv_c = conv_c + ccw_ref[k:k + 1, :] * vbuf[off:off + ts, :]
    vbuf[0:HIST_C, :] = vbuf[ts:ts + HIST_C, :]
    c_out = _dot((c_b * conv_c).astype(BF16), w_c_out_ref[...])

    merged = _sigmoid(_dot(h, w_in_ref[:, OFF_G:OFF_G + D_MODEL])) * a_out
    merged = merged + _sigmoid(_dot(h, w_in_ref[:, OFF_G + D_MODEL:OFF_G + 2 * D_MODEL])) * p_out
    merged = merged + _sigmoid(_dot(h, w_in_ref[:, OFF_G + 2 * D_MODEL:OFF_G + 3 * D_MODEL])) * c_out
    o_ref[0] = x + _dot(merged.astype(BF16), w_o_ref[...])


def _mixer(x, g, w_in, caw, cab, lng, lnb, w_a_out, w_pool, pool_scale, ccw, w_c_out, w_o):
    b, s, d = x.shape
    ts = SEQ_TILE
    consts = (g, w_in, caw, cab, lng, lnb, w_a_out, w_pool, pool_scale, ccw, w_c_out, w_o)
    return pl.pallas_call(
        _mixer_kernel,
        out_shape=jax.ShapeDtypeStruct(x.shape, F32),
        grid=(b, s // ts),
        in_specs=[pl.BlockSpec((1, ts, d), lambda i, j: (i, j, 0))]
        + [_const_spec(c.shape) for c in consts],
        out_specs=pl.BlockSpec((1, ts, d), lambda i, j: (i, j, 0)),
        scratch_shapes=[pltpu.VMEM((HIST_A + ts, D_CONV), F32),
                        pltpu.VMEM((HIST_POOL + ts, D_POOL), F32),
                        pltpu.VMEM((HIST_C + ts, D_SC), F32)],
        compiler_params=pltpu.CompilerParams(
            dimension_semantics=("arbitrary", "arbitrary"),
            vmem_limit_bytes=VMEM_LIMIT_BYTES),
        name="mixer",
    )(x, *consts)


def _kv_kernel(mem_ref, g_ref, w_kv_ref, kt_ref, v_ref):
    mn = _rmsnorm(mem_ref[0], g_ref[...]).astype(BF16)
    kv = _dot(mn, w_kv_ref[...])
    kt_ref[0] = kv[:, :D_MODEL].T.astype(BF16)
    v_ref[0] = kv[:, D_MODEL:].astype(BF16)


def _kv_proj(mem, g, w_kv):
    b, m, d = mem.shape
    return pl.pallas_call(
        _kv_kernel,
        out_shape=(jax.ShapeDtypeStruct((b, d, m), BF16), jax.ShapeDtypeStruct((b, m, d), BF16)),
        grid=(b,),
        in_specs=[pl.BlockSpec((1, m, d), lambda i: (i, 0, 0)),
                  _const_spec(g.shape), _const_spec(w_kv.shape)],
        out_specs=(pl.BlockSpec((1, d, m), lambda i: (i, 0, 0)),
                   pl.BlockSpec((1, m, d), lambda i: (i, 0, 0))),
        compiler_params=pltpu.CompilerParams(
            dimension_semantics=("arbitrary",), vmem_limit_bytes=VMEM_LIMIT_BYTES),
        name="kv_proj",
    )(mem, g, w_kv)


def _route_rows(lt):
    def first_argmax(vals, vmax):
        idx = jnp.full(vmax.shape, len(vals) - 1, jnp.int32)
        for k in range(len(vals) - 2, -1, -1):
            idx = jnp.where(vals[k] == vmax, k, idx)
        return idx

    g = [lt[k:k + 1, :] for k in range(MOE_GROUPS)]
    gmax = functools.reduce(jnp.maximum, g)
    g_idx = first_argmax(g, gmax)
    g_val = 1.0 / sum(jnp.exp(v - gmax) for v in g)
    e = []
    for j in range(EXPERTS_PER_GROUP):
        ej = lt[MOE_GROUPS + j:MOE_GROUPS + j + 1, :]
        for grp in range(1, MOE_GROUPS):
            r = MOE_GROUPS + grp * EXPERTS_PER_GROUP + j
            ej = jnp.where(g_idx == grp, lt[r:r + 1, :], ej)
        e.append(ej)
    m1 = functools.reduce(jnp.maximum, e)
    i1 = first_argmax(e, m1)
    e2 = [jnp.where(i1 == j, -jnp.inf, e[j]) for j in range(EXPERTS_PER_GROUP)]
    m2 = functools.reduce(jnp.maximum, e2)
    i2 = first_argmax(e2, m2)
    esum = sum(jnp.exp(v - m1) for v in e)
    p1 = 1.0 / esum
    p2 = jnp.exp(m2 - m1) / esum
    w1 = g_val * (p1 / (p1 + p2))
    w2 = g_val * (p2 / (p1 + p2))
    lo = jnp.minimum(i1, i2)
    hi = jnp.maximum(i1, i2)
    pair = ((lo * (2 * EXPERTS_PER_GROUP - 1 - lo)) >> 1) + (hi - lo - 1)
    cls = g_idx * PAIRS_PER_GROUP + pair
    first_is_low = i1 < i2
    return cls, jnp.where(first_is_low, w1, w2), jnp.where(first_is_low, w2, w1)


def _xattn_kernel(x_ref, g_ref, w_q_ref, kt_ref, v_ref, w_o_ref, gf_ref, w_r_ref, b_r_ref,
                  o_ref, hp_ref, info_ref, cls_ref):
    x = x_ref[0]
    h = _rmsnorm(x, g_ref[...]).astype(BF16)
    q = (_dot(h, w_q_ref[...]) * (XATTN_HEAD_DIM ** -0.5)).astype(BF16)
    heads = []
    for hd in range(XATTN_HEADS):
        c0, c1 = hd * XATTN_HEAD_DIM, (hd + 1) * XATTN_HEAD_DIM
        sc = _dot(q[:, c0:c1], kt_ref[0, c0:c1, :])
        e = jnp.exp(sc - jnp.max(sc, axis=-1, keepdims=True))
        denom = jnp.sum(e, axis=-1, keepdims=True)
        heads.append(_dot(e.astype(BF16), v_ref[0, :, c0:c1]) / denom)
    o = jnp.concatenate(heads, axis=-1).astype(BF16)
    xn = x + _dot(o, w_o_ref[...])
    o_ref[0] = xn

    hf = _rmsnorm(xn, gf_ref[...])
    hp_ref[...] = _pack_pair(hf)
    h_hi = hf.astype(BF16)
    h_lo = (hf - h_hi.astype(F32)).astype(BF16)
    nt_dims = (((1,), (1,)), ((), ()))
    part = lax.dot_general(w_r_ref[...], h_hi, nt_dims, preferred_element_type=F32)
    lt = (part[:ROUTER_ROWS] + part[ROUTER_ROWS:]
          + lax.dot_general(w_r_ref[:ROUTER_ROWS, :], h_lo, nt_dims, preferred_element_type=F32)
          + b_r_ref[...])
    cls, wa, wb = _route_rows(lt)
    ts = cls.shape[1]
    for c in range(ts // LANES):
        cls_ref[0, c:c + 1, :] = cls[:, c * LANES:(c + 1) * LANES]
    row = lax.broadcasted_iota(jnp.int32, (ROUTER_LANES, ts), 0)
    info_ref[...] = jnp.where(row == INFO_WA, wa, jnp.where(row == INFO_WB, wb, 0.0)).T


def _xattn(x, g, w_q, kt, v, w_o, g_ffn, w_r, b_r):
    b, s, d = x.shape
    ts = SEQ_TILE
    nt = s // ts
    n = b * s
    return pl.pallas_call(
        _xattn_kernel,
        out_shape=(jax.ShapeDtypeStruct(x.shape, F32),
                   jax.ShapeDtypeStruct((n, d // 2), jnp.uint32),
                   jax.ShapeDtypeStruct((n, ROUTER_LANES), F32),
                   jax.ShapeDtypeStruct((n // ts, ts // LANES, LANES), jnp.int32)),
        grid=(b, nt),
        in_specs=[pl.BlockSpec((1, ts, d), lambda i, j: (i, j, 0)),
                  _const_spec(g.shape), _const_spec(w_q.shape),
                  pl.BlockSpec((1, d, N_MEM), lambda i, j: (i, 0, 0)),
                  pl.BlockSpec((1, N_MEM, d), lambda i, j: (i, 0, 0)),
                  _const_spec(w_o.shape), _const_spec(g_ffn.shape),
                  _const_spec(w_r.shape), _const_spec(b_r.shape)],
        out_specs=(pl.BlockSpec((1, ts, d), lambda i, j: (i, j, 0)),
                   pl.BlockSpec((ts, d // 2), lambda i, j: (i * nt + j, 0)),
                   pl.BlockSpec((ts, ROUTER_LANES), lambda i, j: (i * nt + j, 0)),
                   pl.BlockSpec((1, ts // LANES, LANES), lambda i, j: (i * nt + j, 0, 0))),
        compiler_params=pltpu.CompilerParams(
            dimension_semantics=("arbitrary", "arbitrary"),
            vmem_limit_bytes=VMEM_LIMIT_BYTES),
        name="xattn",
    )(x, g, w_q, kt, v, w_o, g_ffn, w_r, b_r)


def _plan_kernel(cls_ref, pos_ref, ea_ref, eb_ref, nu_ref):
    cls = cls_ref[...]
    nr, nl = cls.shape
    r = lax.broadcasted_iota(jnp.int32, (nl, nl), 0)
    c = lax.broadcasted_iota(jnp.int32, (nl, nl), 1)
    lanes_before = (r < c).astype(BF16)
    rr = lax.broadcasted_iota(jnp.int32, (nr, nr), 0)
    rc = lax.broadcasted_iota(jnp.int32, (nr, nr), 1)
    rows_before = (rc < rr).astype(BF16)
    ones_l = jnp.ones((nl, nl), BF16)
    ones_r = jnp.ones((nr, nr), BF16)
    n_tab = ea_ref.shape[1]
    tile_i = lax.broadcasted_iota(jnp.int32, (SUBLANES, n_tab), 1).astype(F32)
    run = jnp.zeros((nr, nl), F32)
    pos = jnp.zeros((nr, nl), F32)
    tile_cls = jnp.zeros((SUBLANES, n_tab), F32)
    for k in range(N_CLASSES):
        m = cls == k
        mb = jnp.where(m, 1.0, 0.0).astype(BF16)
        rowtot = _dot(mb, ones_l).astype(BF16)
        rank = _dot(mb, lanes_before) + _dot(rows_before, rowtot)
        total = _dot(ones_r, rowtot)
        pos = jnp.where(m, run * EXPERT_ROW_TILE + rank, pos)
        run = run + jnp.floor((total + (EXPERT_ROW_TILE - 1)) / EXPERT_ROW_TILE)
        end = jnp.concatenate([run[:SUBLANES, :]] * (n_tab // nl), axis=1)
        tile_cls = tile_cls + jnp.where(end <= tile_i, 1.0, 0.0)
    pos_ref[...] = pos.astype(jnp.int32)
    tc = jnp.minimum(tile_cls, N_CLASSES - 1.0).astype(jnp.int32)
    grp = sum(jnp.where(tc >= g * PAIRS_PER_GROUP, 1, 0) for g in range(1, MOE_GROUPS))
    pair = tc - grp * PAIRS_PER_GROUP
    pair_lo = jnp.where(pair >= 3, 1, 0) + jnp.where(pair >= 5, 1, 0)
    pair_hi = pair + 1 - jnp.where(pair >= 5, 3, jnp.where(pair >= 3, 2, 0))
    ea_ref[...] = grp * EXPERTS_PER_GROUP + pair_lo
    eb_ref[...] = grp * EXPERTS_PER_GROUP + pair_hi
    nu_ref[...] = run[:SUBLANES, :].astype(jnp.int32)


def _plan(cls2):
    nr, nl = cls2.shape
    n_tiles = _sorted_rows(nr * nl) // EXPERT_ROW_TILE
    n_tab = -(-n_tiles // nl) * nl
    vmem = pl.BlockSpec(memory_space=pltpu.VMEM)
    pos, ea, eb, nu = pl.pallas_call(
        _plan_kernel,
        out_shape=(jax.ShapeDtypeStruct((nr, nl), jnp.int32),
                   jax.ShapeDtypeStruct((SUBLANES, n_tab), jnp.int32),
                   jax.ShapeDtypeStruct((SUBLANES, n_tab), jnp.int32),
                   jax.ShapeDtypeStruct((SUBLANES, nl), jnp.int32)),
        in_specs=[vmem],
        out_specs=(vmem, vmem, vmem, vmem),
        name="moe_plan",
    )(cls2)
    return pos.reshape(nr * nl), ea[0, :n_tiles], eb[0, :n_tiles], nu[0, :1]


def _scatter_kernel(pos_ref, hp_ref, info_ref, s_ref, w_ref):
    i = pl.program_id(0)
    tn = hp_ref.shape[0]

    @pl.when(i == 0)
    def _():
        zero_rows = _pack_pair(jnp.zeros((tn, 2 * s_ref.shape[1]), F32))

        def fill(c, carry):
            s_ref[pl.ds(pl.multiple_of(c * tn, tn), tn), :] = zero_rows
            return carry

        lax.fori_loop(0, s_ref.shape[0] // tn, fill, 0)
        w_ref[...] = jnp.zeros(w_ref.shape, w_ref.dtype)

    def body(j, carry):
        base = pl.multiple_of(j * SUBLANES, SUBLANES)
        hb = hp_ref[pl.ds(base, SUBLANES), :]
        ib = info_ref[pl.ds(base, SUBLANES), :]
        for k in range(SUBLANES):
            p = pos_ref[i * tn + base + k]
            s_ref[pl.ds(p, 1), :] = hb[k:k + 1, :]
            w_ref[pl.ds(p, 1), :] = ib[k:k + 1, :]
        return carry

    lax.fori_loop(0, tn // SUBLANES, body, 0)


def _scatter(pos, hp, info):
    n, hw = hp.shape
    tn = ROW_MOVE_TILE
    rows = _sorted_rows(n)
    grid_spec = pltpu.PrefetchScalarGridSpec(
        num_scalar_prefetch=1,
        grid=(n // tn,),
        in_specs=[pl.BlockSpec((tn, hw), lambda i, pos: (i, 0)),
                  pl.BlockSpec((tn, ROUTER_LANES), lambda i, pos: (i, 0))],
        out_specs=(pl.BlockSpec((rows, hw), lambda i, pos: (0, 0),
                                pipeline_mode=pl.Buffered(1)),
                   pl.BlockSpec((rows, ROUTER_LANES), lambda i, pos: (0, 0),
                                pipeline_mode=pl.Buffered(1))),
    )
    return pl.pallas_call(
        _scatter_kernel,
        out_shape=(jax.ShapeDtypeStruct((rows, hw), jnp.uint32),
                   jax.ShapeDtypeStruct((rows, ROUTER_LANES), F32)),
        grid_spec=grid_spec,
        compiler_params=pltpu.CompilerParams(
            dimension_semantics=("arbitrary",), vmem_limit_bytes=VMEM_LIMIT_RESIDENT_BYTES),
        name="moe_scatter",
    )(pos, hp, info)


def _expert_kernel(ea_ref, eb_ref, nu_ref, s_ref, w_ref, wg_ref, wu_ref, wd_ref, y_ref):
    i = pl.program_id(0)
    for j in range(EXPERT_STEP_TILES):
        t = i * EXPERT_STEP_TILES + j
        rows = slice(j * EXPERT_ROW_TILE, (j + 1) * EXPERT_ROW_TILE)

        @pl.when(t < nu_ref[0])
        def _():
            hb = _unpack_pair(s_ref[rows, :]).astype(BF16)
            wts = w_ref[rows, :]
            y = None
            for e_ref, col in ((ea_ref, INFO_WA), (eb_ref, INFO_WB)):
                e = e_ref[t]
                gt = _dot(hb, wg_ref[e])
                act = gt * _sigmoid(gt) * _dot(hb, wu_ref[e]) * wts[:, col:col + 1]
                part = _dot(act.astype(BF16), wd_ref[e])
                y = part if y is None else y + part
            y_ref[rows, :] = _pack_pair(y)

        @pl.when(t >= nu_ref[0])
        def _():
            y_ref[rows, :] = _pack_pair(jnp.zeros((EXPERT_ROW_TILE, 2 * y_ref.shape[1]), F32))


def _experts(tile_ea, tile_eb, n_used, s, w, wg, wu, wd):
    rows, hw = s.shape
    tb = EXPERT_ROW_TILE * EXPERT_STEP_TILES
    grid_spec = pltpu.PrefetchScalarGridSpec(
        num_scalar_prefetch=3,
        grid=(rows // tb,),
        in_specs=[pl.BlockSpec((tb, hw), lambda i, *_: (i, 0)),
                  pl.BlockSpec((tb, ROUTER_LANES), lambda i, *_: (i, 0)),
                  _const_spec(wg.shape), _const_spec(wu.shape), _const_spec(wd.shape)],
        out_specs=pl.BlockSpec((tb, hw), lambda i, *_: (i, 0)),
    )
    return pl.pallas_call(
        _expert_kernel,
        out_shape=jax.ShapeDtypeStruct((rows, hw), jnp.uint32),
        grid_spec=grid_spec,
        compiler_params=pltpu.CompilerParams(
            dimension_semantics=("arbitrary",), vmem_limit_bytes=VMEM_LIMIT_BYTES),
        name="moe_experts",
    )(tile_ea, tile_eb, n_used, s, w, wg, wu, wd)


def _gather_kernel(pos_ref, x_ref, y_ref, gn_ref, o_ref, rows_ref, *, final_norm):
    i = pl.program_id(0)
    tn = x_ref.shape[0]

    def body(j, carry):
        base = pl.multiple_of(j * SUBLANES, SUBLANES)
        for k in range(SUBLANES):
            p = pos_ref[i * tn + base + k]
            rows_ref[pl.ds(base + k, 1), :] = y_ref[pl.ds(p, 1), :]
        return carry

    lax.fori_loop(0, tn // SUBLANES, body, 0)
    out = x_ref[...] + _unpack_pair(rows_ref[...])
    if final_norm:
        out = _rmsnorm(out, gn_ref[...])
    o_ref[...] = out


def _gather(pos, x2, y, g_final, final_norm):
    n, d = x2.shape
    tn = ROW_MOVE_TILE
    grid_spec = pltpu.PrefetchScalarGridSpec(
        num_scalar_prefetch=1,
        grid=(n // tn,),
        in_specs=[pl.BlockSpec((tn, d), lambda i, pos: (i, 0)),
                  pl.BlockSpec(y.shape, lambda i, pos: (0, 0), pipeline_mode=pl.Buffered(1)),
                  pl.BlockSpec(g_final.shape, lambda i, pos: (0, 0), pipeline_mode=pl.Buffered(1))],
        out_specs=pl.BlockSpec((tn, d), lambda i, pos: (i, 0)),
        scratch_shapes=[pltpu.VMEM((tn, y.shape[1]), jnp.uint32)],
    )
    return pl.pallas_call(
        functools.partial(_gather_kernel, final_norm=final_norm),
        out_shape=jax.ShapeDtypeStruct(x2.shape, F32),
        grid_spec=grid_spec,
        compiler_params=pltpu.CompilerParams(
            dimension_semantics=("arbitrary",), vmem_limit_bytes=VMEM_LIMIT_RESIDENT_BYTES),
        name="moe_gather",
    )(pos, x2, y, g_final)


def _row(v):
    return v.reshape(1, -1)


def _router_params(w_rg, b_rg, w_re, b_re):
    pad = ROUTER_ROWS - MOE_GROUPS - N_EXPERTS
    wt = jnp.pad(jnp.concatenate([w_rg, w_re], axis=1).T, ((0, pad), (0, 0)))
    w_hi = wt.astype(BF16)
    w_lo = (wt - w_hi.astype(F32)).astype(BF16)
    b_col = jnp.pad(jnp.concatenate([b_rg, b_re]), (0, pad)).reshape(ROUTER_ROWS, 1)
    return jnp.concatenate([w_hi, w_lo], axis=0), b_col


def kernel(x, mem, norm_mix_g, w_in, conv_a_w, conv_a_b, ln_a_g, ln_a_b, w_a_out, w_pool_grp, pool_scale, conv_c_w, w_c_out, w_o, norm_x_g, norm_mem_g, w_xq, w_xkv, w_xo, norm_ffn_g, w_rg, b_rg, w_re, b_re, w_e_gate, w_e_up, w_e_down, norm_f_g):
    bsz, seq, d = x.shape
    n = bsz * seq
    depth = w_in.shape[0]
    for l in range(depth):
        x = _mixer(x, _row(norm_mix_g[l]), w_in[l].astype(BF16), conv_a_w[l], _row(conv_a_b[l]),
                   _row(ln_a_g[l]), _row(ln_a_b[l]), w_a_out[l].astype(BF16),
                   w_pool_grp[l].astype(BF16), _row(pool_scale[l]), conv_c_w[l],
                   w_c_out[l].astype(BF16), w_o[l].astype(BF16))
        kt, v = _kv_proj(mem, _row(norm_mem_g[l]), w_xkv[l].astype(BF16))
        w_r, b_r = _router_params(w_rg[l], b_rg[l], w_re[l], b_re[l])
        x, hp, info, cls = _xattn(x, _row(norm_x_g[l]), w_xq[l].astype(BF16), kt, v,
                                  w_xo[l].astype(BF16), _row(norm_ffn_g[l]), w_r, b_r)
        pos, tile_ea, tile_eb, n_used = _plan(cls.reshape(n // LANES, LANES))
        s_rows, s_wts = _scatter(pos, hp, info)
        y = _experts(tile_ea, tile_eb, n_used, s_rows, s_wts, w_e_gate[l].astype(BF16),
                     w_e_up[l].astype(BF16), w_e_down[l].astype(BF16))
        x = _gather(pos, x.reshape(n, d), y, _row(norm_f_g),
                    final_norm=(l == depth - 1)).reshape(bsz, seq, d)
    return x
```

```python
import functools

import jax
import jax.numpy as jnp
from jax import lax
from jax.experimental import pallas as pl
from jax.experimental.pallas import tpu as pltpu

D_MODEL = 1024
D_CONV = D_MODEL // 2
D_POOL = D_MODEL // 2
D_SC = D_MODEL // 2
CONV_A_K = 31
SC_K = 3
POOL_WINDOWS = (2, 4, 8, 16)
POOL_GROUP_DIM = D_POOL // len(POOL_WINDOWS)
POOL_OUT_DIM = D_MODEL // len(POOL_WINDOWS)
N_MEM = 256
XATTN_HEADS = 4
XATTN_HEAD_DIM = D_MODEL // XATTN_HEADS
MOE_GROUPS = 4
EXPERTS_PER_GROUP = 4
N_EXPERTS = MOE_GROUPS * EXPERTS_PER_GROUP
D_EXPERT = D_MODEL // 4
EPS = 1e-6

OFF_A = 0
OFF_POOL = 2 * D_CONV
OFF_C = OFF_POOL + D_POOL
OFF_G = OFF_C + 3 * D_SC
D_IN_PROJ = OFF_G + 3 * D_MODEL

SUBLANES = 8
LANES = 128
VMEM_LIMIT_BYTES = 56 * 1024 * 1024

HIST_A = 32
HIST_POOL = 16
HIST_C = 8

SEQ_TILE = 256
CONV_ROW_CHUNK = 16
PROJ_SLICE = 512
ROUTER_LANES = LANES

PAIRS_PER_GROUP = EXPERTS_PER_GROUP * (EXPERTS_PER_GROUP - 1) // 2
N_CLASSES = MOE_GROUPS * PAIRS_PER_GROUP
EXPERT_ROW_TILE = 128
EXPERT_STEP_TILES = 4
ROUTER_ROWS = 32
ROW_MOVE_TILE = 512
VMEM_LIMIT_RESIDENT_BYTES = 58 * 1024 * 1024
INFO_WA, INFO_WB = 0, 1

BF16 = jnp.bfloat16
F32 = jnp.float32


def _sorted_rows(n_tokens):
    return n_tokens + N_CLASSES * EXPERT_ROW_TILE


def _sigmoid(v):
    return 0.5 * jnp.tanh(0.5 * v) + 0.5


def _rmsnorm(xf, g):
    return xf * lax.rsqrt(jnp.mean(xf * xf, axis=-1, keepdims=True) + EPS) * g


def _dot(a, b):
    return jnp.dot(a, b, preferred_element_type=F32)


def _const_spec(shape):
    nd = len(shape)
    return pl.BlockSpec(shape, lambda *_: (0,) * nd, pipeline_mode=pl.Buffered(1))


def _unpack_pair(q):
    return jnp.concatenate(
        [pltpu.unpack_elementwise(q, index=0, packed_dtype=BF16, unpacked_dtype=F32),
         pltpu.unpack_elementwise(q, index=1, packed_dtype=BF16, unpacked_dtype=F32)], axis=-1)


def _pack_pair(v):
    half = v.shape[-1] // 2
    return pltpu.pack_elementwise([v[:, :half], v[:, half:]], packed_dtype=BF16)


def _mixer_kernel(x_ref, g_ref, w_in_ref, caw_ref, cab_ref, lng_ref, lnb_ref, w_a_out_ref,
                  w_pool_ref, pool_scale_ref, ccw_ref, w_c_out_ref, w_o_ref,
                  o_ref, abuf, ubuf, vbuf, actbuf, pbuf):
    ts = x_ref.shape[1]
    s = pl.program_id(1)

    @pl.when(s == 0)
    def _():
        abuf[0, 0:HIST_A, :] = jnp.zeros((HIST_A, D_CONV), F32)
        ubuf[0:HIST_POOL, :] = jnp.zeros((HIST_POOL, D_POOL), F32)
        vbuf[0:HIST_C, :] = jnp.zeros((HIST_C, D_SC), F32)

    x = x_ref[0]
    h = _rmsnorm(x, g_ref[...]).astype(BF16)

    def proj(off, width):
        return _dot(h, w_in_ref[:, off:off + width])

    pa = proj(OFF_A, 2 * D_CONV)
    abuf[0, HIST_A:HIST_A + ts, :] = pa[:, :D_CONV] * _sigmoid(pa[:, D_CONV:])
    shifted_rows = HIST_A + ts - SUBLANES
    for r in range(1, SUBLANES):
        abuf[r, 0:shifted_rows, :] = abuf[0, r:r + shifted_rows, :]
    chunk3 = (CONV_ROW_CHUNK // SUBLANES, SUBLANES, D_CONV)

    def conv_a_chunk(c0):
        acc = jnp.broadcast_to(cab_ref[...], chunk3)
        for k in range(CONV_A_K):
            q, r = divmod(HIST_A - (CONV_A_K - 1) + k, SUBLANES)
            lo = c0 + q * SUBLANES
            acc = acc + caw_ref[k] * abuf[r, lo:lo + CONV_ROW_CHUNK, :].reshape(chunk3)
        acc = acc.reshape(CONV_ROW_CHUNK, D_CONV)
        mu = jnp.mean(acc, axis=-1, keepdims=True)
        cen = acc - mu
        var = jnp.mean(cen * cen, axis=-1, keepdims=True)
        ln = cen * lax.rsqrt(var + EPS) * lng_ref[...] + lnb_ref[...]
        actbuf[c0:c0 + CONV_ROW_CHUNK, :] = (ln * _sigmoid(ln)).astype(BF16)

    chunks = list(range(0, ts, CONV_ROW_CHUNK))
    slices = list(range(OFF_POOL, D_IN_PROJ, PROJ_SLICE))
    for i in range(max(len(chunks), len(slices))):
        if i < len(chunks):
            conv_a_chunk(chunks[i])
        if i < len(slices):
            val = proj(slices[i], PROJ_SLICE)
            if slices[i] >= OFF_G:
                val = _sigmoid(val)
            pbuf[:, slices[i] - OFF_POOL:slices[i] - OFF_POOL + PROJ_SLICE] = val
    abuf[0, 0:HIST_A, :] = abuf[0, ts:ts + HIST_A, :]

    def pcol(off, width):
        return pbuf[:, off - OFF_POOL:off - OFF_POOL + width]

    pu = pcol(OFF_POOL, D_POOL)
    ubuf[HIST_POOL:HIST_POOL + ts, :] = pu
    gate_a = pcol(OFF_G, D_MODEL)
    gate_p = pcol(OFF_G + D_MODEL, D_MODEL)
    gate_c = pcol(OFF_G + 2 * D_MODEL, D_MODEL)
    merged = gate_a * _dot(actbuf[...], w_a_out_ref[...])

    t_glob = s * ts + lax.broadcasted_iota(jnp.int32, (ts, 1), 0)
    p_parts = []
    for i, w in enumerate(POOL_WINDOWS):
        c0, c1 = i * POOL_GROUP_DIM, (i + 1) * POOL_GROUP_DIM
        tok = pu[:, c0:c1]
        win = tok
        for j in range(1, w):
            win = win + ubuf[HIST_POOL - j:HIST_POOL - j + ts, c0:c1]
        cnt = jnp.minimum(t_glob + 1, w).astype(F32)
        pin = (win / cnt - tok).astype(BF16)
        p_parts.append(_dot(pin, w_pool_ref[i]))
    ubuf[0:HIST_POOL, :] = ubuf[ts:ts + HIST_POOL, :]
    merged = merged + gate_p * (jnp.concatenate(p_parts, axis=-1) * pool_scale_ref[...])

    c_b = pcol(OFF_C + D_SC, D_SC)
    v = pcol(OFF_C + 2 * D_SC, D_SC) * pcol(OFF_C, D_SC)
    vbuf[HIST_C:HIST_C + ts, :] = v
    conv_c = ccw_ref[SC_K - 1:SC_K, :] * v
    for k in range(SC_K - 1):
        off = HIST_C - (SC_K - 1) + k
        conv_c = conv_c + ccw_ref[k:k + 1, :] * vbuf[off:off + ts, :]
    vbuf[0:HIST_C, :] = vbuf[ts:ts + HIST_C, :]
    merged = merged + gate_c * _dot((c_b * conv_c).astype(BF16), w_c_out_ref[...])

    o_ref[0] = x + _dot(merged.astype(BF16), w_o_ref[...])


def _mixer(x, g, w_in, caw, cab, lng, lnb, w_a_out, w_pool, pool_scale, ccw, w_c_out, w_o):
    b, s, d = x.shape
    ts = SEQ_TILE
    consts = (g, w_in, caw, cab, lng, lnb, w_a_out, w_pool, pool_scale, ccw, w_c_out, w_o)
    return pl.pallas_call(
        _mixer_kernel,
        out_shape=jax.ShapeDtypeStruct(x.shape, F32),
        grid=(b, s // ts),
        in_specs=[pl.BlockSpec((1, ts, d), lambda i, j: (i, j, 0))]
        + [_const_spec(c.shape) for c in consts],
        out_specs=pl.BlockSpec((1, ts, d), lambda i, j: (i, j, 0)),
        scratch_shapes=[pltpu.VMEM((SUBLANES, HIST_A + ts, D_CONV), F32),
                        pltpu.VMEM((HIST_POOL + ts, D_POOL), F32),
                        pltpu.VMEM((HIST_C + ts, D_SC), F32),
                        pltpu.VMEM((ts, D_CONV), BF16),
                        pltpu.VMEM((ts, D_IN_PROJ - OFF_POOL), F32)],
        compiler_params=pltpu.CompilerParams(
            dimension_semantics=("arbitrary", "arbitrary"),
            vmem_limit_bytes=VMEM_LIMIT_BYTES),
        name="mixer",
    )(x, *consts)


def _kv_kernel(mem_ref, g_ref, w_kv_ref, kt_ref, v_ref):
    mn = _rmsnorm(mem_ref[0], g_ref[...]).astype(BF16)
    kv = _dot(mn, w_kv_ref[...])
    kt_ref[0] = kv[:, :D_MODEL].T.astype(BF16)
    v_ref[0] = kv[:, D_MODEL:].astype(BF16)


def _kv_proj(mem, g, w_kv):
    b, m, d = mem.shape
    return pl.pallas_call(
        _kv_kernel,
        out_shape=(jax.ShapeDtypeStruct((b, d, m), BF16), jax.ShapeDtypeStruct((b, m, d), BF16)),
        grid=(b,),
        in_specs=[pl.BlockSpec((1, m, d), lambda i: (i, 0, 0)),
                  _const_spec(g.shape), _const_spec(w_kv.shape)],
        out_specs=(pl.BlockSpec((1, d, m), lambda i: (i, 0, 0)),
                   pl.BlockSpec((1, m, d), lambda i: (i, 0, 0))),
        compiler_params=pltpu.CompilerParams(
            dimension_semantics=("arbitrary",), vmem_limit_bytes=VMEM_LIMIT_BYTES),
        name="kv_proj",
    )(mem, g, w_kv)


def _route_rows(lt):
    def first_argmax(vals, vmax):
        idx = jnp.full(vmax.shape, len(vals) - 1, jnp.int32)
        for k in range(len(vals) - 2, -1, -1):
            idx = jnp.where(vals[k] == vmax, k, idx)
        return idx

    g = [lt[k:k + 1, :] for k in range(MOE_GROUPS)]
    gmax = functools.reduce(jnp.maximum, g)
    g_idx = first_argmax(g, gmax)
    g_val = 1.0 / sum(jnp.exp(v - gmax) for v in g)
    e = []
    for j in range(EXPERTS_PER_GROUP):
        ej = lt[MOE_GROUPS + j:MOE_GROUPS + j + 1, :]
        for grp in range(1, MOE_GROUPS):
            r = MOE_GROUPS + grp * EXPERTS_PER_GROUP + j
            ej = jnp.where(g_idx == grp, lt[r:r + 1, :], ej)
        e.append(ej)
    m1 = functools.reduce(jnp.maximum, e)
    i1 = first_argmax(e, m1)
    e2 = [jnp.where(i1 == j, -jnp.inf, e[j]) for j in range(EXPERTS_PER_GROUP)]
    m2 = functools.reduce(jnp.maximum, e2)
    i2 = first_argmax(e2, m2)
    esum = sum(jnp.exp(v - m1) for v in e)
    p1 = 1.0 / esum
    p2 = jnp.exp(m2 - m1) / esum
    w1 = g_val * (p1 / (p1 + p2))
    w2 = g_val * (p2 / (p1 + p2))
    lo = jnp.minimum(i1, i2)
    hi = jnp.maximum(i1, i2)
    pair = ((lo * (2 * EXPERTS_PER_GROUP - 1 - lo)) >> 1) + (hi - lo - 1)
    cls = g_idx * PAIRS_PER_GROUP + pair
    first_is_low = i1 < i2
    return cls, jnp.where(first_is_low, w1, w2), jnp.where(first_is_low, w2, w1)


def _xattn_kernel(x_ref, g_ref, w_q_ref, kt_ref, v_ref, w_o_ref, gf_ref, w_r_ref, b_r_ref,
                  o_ref, hp_ref, info_ref, cls_ref):
    x = x_ref[0]
    h = _rmsnorm(x, g_ref[...]).astype(BF16)
    q = (_dot(h, w_q_ref[...]) * (XATTN_HEAD_DIM ** -0.5)).astype(BF16)
    heads = []
    for hd in range(XATTN_HEADS):
        c0, c1 = hd * XATTN_HEAD_DIM, (hd + 1) * XATTN_HEAD_DIM
        sc = _dot(q[:, c0:c1], kt_ref[0, c0:c1, :])
        e = jnp.exp(sc - jnp.max(sc, axis=-1, keepdims=True))
        denom = jnp.sum(e, axis=-1, keepdims=True)
        heads.append(_dot(e.astype(BF16), v_ref[0, :, c0:c1]) / denom)
    o = jnp.concatenate(heads, axis=-1).astype(BF16)
    xn = x + _dot(o, w_o_ref[...])
    o_ref[0] = xn

    hf = _rmsnorm(xn, gf_ref[...])
    hp_ref[...] = _pack_pair(hf)
    h_hi = hf.astype(BF16)
    h_lo = (hf - h_hi.astype(F32)).astype(BF16)
    nt_dims = (((1,), (1,)), ((), ()))
    part = lax.dot_general(w_r_ref[...], h_hi, nt_dims, preferred_element_type=F32)
    lt = (part[:ROUTER_ROWS] + part[ROUTER_ROWS:]
          + lax.dot_general(w_r_ref[:ROUTER_ROWS, :], h_lo, nt_dims, preferred_element_type=F32)
          + b_r_ref[...])
    cls, wa, wb = _route_rows(lt)
    ts = cls.shape[1]
    for c in range(ts // LANES):
        cls_ref[0, c:c + 1, :] = cls[:, c * LANES:(c + 1) * LANES]
    row = lax.broadcasted_iota(jnp.int32, (ROUTER_LANES, ts), 0)
    info_ref[...] = jnp.where(row == INFO_WA, wa, jnp.where(row == INFO_WB, wb, 0.0)).T


def _xattn(x, g, w_q, kt, v, w_o, g_ffn, w_r, b_r):
    b, s, d = x.shape
    ts = SEQ_TILE
    nt = s // ts
    n = b * s
    return pl.pallas_call(
        _xattn_kernel,
        out_shape=(jax.ShapeDtypeStruct(x.shape, F32),
                   jax.ShapeDtypeStruct((n, d // 2), jnp.uint32),
                   jax.ShapeDtypeStruct((n, ROUTER_LANES), F32),
                   jax.ShapeDtypeStruct((n // ts, ts // LANES, LANES), jnp.int32)),
        grid=(b, nt),
        in_specs=[pl.BlockSpec((1, ts, d), lambda i, j: (i, j, 0)),
                  _const_spec(g.shape), _const_spec(w_q.shape),
                  pl.BlockSpec((1, d, N_MEM), lambda i, j: (i, 0, 0)),
                  pl.BlockSpec((1, N_MEM, d), lambda i, j: (i, 0, 0)),
                  _const_spec(w_o.shape), _const_spec(g_ffn.shape),
                  _const_spec(w_r.shape), _const_spec(b_r.shape)],
        out_specs=(pl.BlockSpec((1, ts, d), lambda i, j: (i, j, 0)),
                   pl.BlockSpec((ts, d // 2), lambda i, j: (i * nt + j, 0)),
                   pl.BlockSpec((ts, ROUTER_LANES), lambda i, j: (i * nt + j, 0)),
                   pl.BlockSpec((1, ts // LANES, LANES), lambda i, j: (i * nt + j, 0, 0))),
        compiler_params=pltpu.CompilerParams(
            dimension_semantics=("arbitrary", "arbitrary"),
            vmem_limit_bytes=VMEM_LIMIT_BYTES),
        name="xattn",
    )(x, g, w_q, kt, v, w_o, g_ffn, w_r, b_r)


def _plan_kernel(cls_ref, pos_ref, ea_ref, eb_ref, nu_ref):
    cls = cls_ref[...]
    nr, nl = cls.shape
    r = lax.broadcasted_iota(jnp.int32, (nl, nl), 0)
    c = lax.broadcasted_iota(jnp.int32, (nl, nl), 1)
    lanes_before = (r < c).astype(BF16)
    rr = lax.broadcasted_iota(jnp.int32, (nr, nr), 0)
    rc = lax.broadcasted_iota(jnp.int32, (nr, nr), 1)
    rows_before = (rc < rr).astype(BF16)
    ones_l = jnp.ones((nl, nl), BF16)
    ones_r = jnp.ones((nr, nr), BF16)
    n_tab = ea_ref.shape[1]
    tile_i = lax.broadcasted_iota(jnp.int32, (SUBLANES, n_tab), 1).astype(F32)
    run = jnp.zeros((nr, nl), F32)
    pos = jnp.zeros((nr, nl), F32)
    tile_cls = jnp.zeros((SUBLANES, n_tab), F32)
    for k in range(N_CLASSES):
        m = cls == k
        mb = jnp.where(m, 1.0, 0.0).astype(BF16)
        rowtot = _dot(mb, ones_l).astype(BF16)
        rank = _dot(mb, lanes_before) + _dot(rows_before, rowtot)
        total = _dot(ones_r, rowtot)
        pos = jnp.where(m, run * EXPERT_ROW_TILE + rank, pos)
        run = run + jnp.floor((total + (EXPERT_ROW_TILE - 1)) / EXPERT_ROW_TILE)
        end = jnp.concatenate([run[:SUBLANES, :]] * (n_tab // nl), axis=1)
        tile_cls = tile_cls + jnp.where(end <= tile_i, 1.0, 0.0)
    pos_ref[...] = pos.astype(jnp.int32)
    tc = jnp.minimum(tile_cls, N_CLASSES - 1.0).astype(jnp.int32)
    grp = sum(jnp.where(tc >= g * PAIRS_PER_GROUP, 1, 0) for g in range(1, MOE_GROUPS))
    pair = tc - grp * PAIRS_PER_GROUP
    pair_lo = jnp.where(pair >= 3, 1, 0) + jnp.where(pair >= 5, 1, 0)
    pair_hi = pair + 1 - jnp.where(pair >= 5, 3, jnp.where(pair >= 3, 2, 0))
    ea_ref[...] = grp * EXPERTS_PER_GROUP + pair_lo
    eb_ref[...] = grp * EXPERTS_PER_GROUP + pair_hi
    nu_ref[...] = run[:SUBLANES, :].astype(jnp.int32)


def _plan(cls2):
    nr, nl = cls2.shape
    n_tiles = _sorted_rows(nr * nl) // EXPERT_ROW_TILE
    n_tab = -(-n_tiles // nl) * nl
    vmem = pl.BlockSpec(memory_space=pltpu.VMEM)
    pos, ea, eb, nu = pl.pallas_call(
        _plan_kernel,
        out_shape=(jax.ShapeDtypeStruct((nr, nl), jnp.int32),
                   jax.ShapeDtypeStruct((SUBLANES, n_tab), jnp.int32),
                   jax.ShapeDtypeStruct((SUBLANES, n_tab), jnp.int32),
                   jax.ShapeDtypeStruct((SUBLANES, nl), jnp.int32)),
        in_specs=[vmem],
        out_specs=(vmem, vmem, vmem, vmem),
        name="moe_plan",
    )(cls2)
    return pos.reshape(nr * nl), ea[0, :n_tiles], eb[0, :n_tiles], nu[0, :1]


def _scatter_kernel(pos_ref, hp_ref, info_ref, s_ref, w_ref):
    i = pl.program_id(0)
    tn = hp_ref.shape[0]

    @pl.when(i == 0)
    def _():
        zero_rows = _pack_pair(jnp.zeros((tn, 2 * s_ref.shape[1]), F32))

        def fill(c, carry):
            s_ref[pl.ds(pl.multiple_of(c * tn, tn), tn), :] = zero_rows
            return carry

        lax.fori_loop(0, s_ref.shape[0] // tn, fill, 0)
        w_ref[...] = jnp.zeros(w_ref.shape, w_ref.dtype)

    def body(j, carry):
        base = pl.multiple_of(j * SUBLANES, SUBLANES)
        hb = hp_ref[pl.ds(base, SUBLANES), :]
        ib = info_ref[pl.ds(base, SUBLANES), :]
        for k in range(SUBLANES):
            p = pos_ref[i * tn + base + k]
            s_ref[pl.ds(p, 1), :] = hb[k:k + 1, :]
            w_ref[pl.ds(p, 1), :] = ib[k:k + 1, :]
        return carry

    lax.fori_loop(0, tn // SUBLANES, body, 0)


def _scatter(pos, hp, info):
    n, hw = hp.shape
    tn = ROW_MOVE_TILE
    rows = _sorted_rows(n)
    grid_spec = pltpu.PrefetchScalarGridSpec(
        num_scalar_prefetch=1,
        grid=(n // tn,),
        in_specs=[pl.BlockSpec((tn, hw), lambda i, pos: (i, 0)),
                  pl.BlockSpec((tn, ROUTER_LANES), lambda i, pos: (i, 0))],
        out_specs=(pl.BlockSpec((rows, hw), lambda i, pos: (0, 0),
                                pipeline_mode=pl.Buffered(1)),
                   pl.BlockSpec((rows, ROUTER_LANES), lambda i, pos: (0, 0),
                                pipeline_mode=pl.Buffered(1))),
    )
    return pl.pallas_call(
        _scatter_kernel,
        out_shape=(jax.ShapeDtypeStruct((rows, hw), jnp.uint32),
                   jax.ShapeDtypeStruct((rows, ROUTER_LANES), F32)),
        grid_spec=grid_spec,
        compiler_params=pltpu.CompilerParams(
            dimension_semantics=("arbitrary",), vmem_limit_bytes=VMEM_LIMIT_RESIDENT_BYTES),
        name="moe_scatter",
    )(pos, hp, info)


def _expert_kernel(ea_ref, eb_ref, nu_ref, s_ref, w_ref, wg_ref, wu_ref, wd_ref, y_ref):
    i = pl.program_id(0)
    for j in range(EXPERT_STEP_TILES):
        t = i * EXPERT_STEP_TILES + j
        rows = slice(j * EXPERT_ROW_TILE, (j + 1) * EXPERT_ROW_TILE)

        @pl.when(t < nu_ref[0])
        def _():
            hb = _unpack_pair(s_ref[rows, :]).astype(BF16)
            wts = w_ref[rows, :]
            y = None
            for e_ref, col in ((ea_ref, INFO_WA), (eb_ref, INFO_WB)):
                e = e_ref[t]
                gt = _dot(hb, wg_ref[e])
                act = gt * _sigmoid(gt) * _dot(hb, wu_ref[e]) * wts[:, col:col + 1]
                part = _dot(act.astype(BF16), wd_ref[e])
                y = part if y is None else y + part
            y_ref[rows, :] = _pack_pair(y)

        @pl.when(t >= nu_ref[0])
        def _():
            y_ref[rows, :] = _pack_pair(jnp.zeros((EXPERT_ROW_TILE, 2 * y_ref.shape[1]), F32))


def _experts(tile_ea, tile_eb, n_used, s, w, wg, wu, wd):
    rows, hw = s.shape
    tb = EXPERT_ROW_TILE * EXPERT_STEP_TILES
    grid_spec = pltpu.PrefetchScalarGridSpec(
        num_scalar_prefetch=3,
        grid=(rows // tb,),
        in_specs=[pl.BlockSpec((tb, hw), lambda i, *_: (i, 0)),
                  pl.BlockSpec((tb, ROUTER_LANES), lambda i, *_: (i, 0)),
                  _const_spec(wg.shape), _const_spec(wu.shape), _const_spec(wd.shape)],
        out_specs=pl.BlockSpec((tb, hw), lambda i, *_: (i, 0)),
    )
    return pl.pallas_call(
        _expert_kernel,
        out_shape=jax.ShapeDtypeStruct((rows, hw), jnp.uint32),
        grid_spec=grid_spec,
        compiler_params=pltpu.CompilerParams(
            dimension_semantics=("arbitrary",), vmem_limit_bytes=VMEM_LIMIT_BYTES),
        name="moe_experts",
    )(tile_ea, tile_eb, n_used, s, w, wg, wu, wd)


def _gather_kernel(pos_ref, x_ref, y_ref, gn_ref, o_ref, rows_ref, *, final_norm):
    i = pl.program_id(0)
    tn = x_ref.shape[0]

    def body(j, carry):
        base = pl.multiple_of(j * SUBLANES, SUBLANES)
        for k in range(SUBLANES):
            p = pos_ref[i * tn + base + k]
            rows_ref[pl.ds(base + k, 1), :] = y_ref[pl.ds(p, 1), :]
        return carry

    lax.fori_loop(0, tn // SUBLANES, body, 0)
    out = x_ref[...] + _unpack_pair(rows_ref[...])
    if final_norm:
        out = _rmsnorm(out, gn_ref[...])
    o_ref[...] = out


def _gather(pos, x2, y, g_final, final_norm):
    n, d = x2.shape
    tn = ROW_MOVE_TILE
    grid_spec = pltpu.PrefetchScalarGridSpec(
        num_scalar_prefetch=1,
        grid=(n // tn,),
        in_specs=[pl.BlockSpec((tn, d), lambda i, pos: (i, 0)),
                  pl.BlockSpec(y.shape, lambda i, pos: (0, 0), pipeline_mode=pl.Buffered(1)),
                  pl.BlockSpec(g_final.shape, lambda i, pos: (0, 0), pipeline_mode=pl.Buffered(1))],
        out_specs=pl.BlockSpec((tn, d), lambda i, pos: (i, 0)),
        scratch_shapes=[pltpu.VMEM((tn, y.shape[1]), jnp.uint32)],
    )
    return pl.pallas_call(
        functools.partial(_gather_kernel, final_norm=final_norm),
        out_shape=jax.ShapeDtypeStruct(x2.shape, F32),
        grid_spec=grid_spec,
        compiler_params=pltpu.CompilerParams(
            dimension_semantics=("arbitrary",), vmem_limit_bytes=VMEM_LIMIT_RESIDENT_BYTES),
        name="moe_gather",
    )(pos, x2, y, g_final)


def _row(v):
    return v.reshape(1, -1)


def _router_params(w_rg, b_rg, w_re, b_re):
    pad = ROUTER_ROWS - MOE_GROUPS - N_EXPERTS
    wt = jnp.pad(jnp.concatenate([w_rg, w_re], axis=1).T, ((0, pad), (0, 0)))
    w_hi = wt.astype(BF16)
    w_lo = (wt - w_hi.astype(F32)).astype(BF16)
    b_col = jnp.pad(jnp.concatenate([b_rg, b_re]), (0, pad)).reshape(ROUTER_ROWS, 1)
    return jnp.concatenate([w_hi, w_lo], axis=0), b_col


def kernel(x, mem, norm_mix_g, w_in, conv_a_w, conv_a_b, ln_a_g, ln_a_b, w_a_out, w_pool_grp, pool_scale, conv_c_w, w_c_out, w_o, norm_x_g, norm_mem_g, w_xq, w_xkv, w_xo, norm_ffn_g, w_rg, b_rg, w_re, b_re, w_e_gate, w_e_up, w_e_down, norm_f_g):
    bsz, seq, d = x.shape
    n = bsz * seq
    depth = w_in.shape[0]
    for l in range(depth):
        caw = jnp.broadcast_to(conv_a_w[l][:, None, :], (CONV_A_K, SUBLANES, D_CONV))
        x = _mixer(x, _row(norm_mix_g[l]), w_in[l].astype(BF16), caw, _row(conv_a_b[l]),
                   _row(ln_a_g[l]), _row(ln_a_b[l]), w_a_out[l].astype(BF16),
                   w_pool_grp[l].astype(BF16), _row(pool_scale[l]), conv_c_w[l],
                   w_c_out[l].astype(BF16), w_o[l].astype(BF16))
        kt, v = _kv_proj(mem, _row(norm_mem_g[l]), w_xkv[l].astype(BF16))
        w_r, b_r = _router_params(w_rg[l], b_rg[l], w_re[l], b_re[l])
        x, hp, info, cls = _xattn(x, _row(norm_x_g[l]), w_xq[l].astype(BF16), kt, v,
                                  w_xo[l].astype(BF16), _row(norm_ffn_g[l]), w_r, b_r)
        pos, tile_ea, tile_eb, n_used = _plan(cls.reshape(n // LANES, LANES))
        s_rows, s_wts = _scatter(pos, hp, info)
        y = _experts(tile_ea, tile_eb, n_used, s_rows, s_wts, w_e_gate[l].astype(BF16),
                     w_e_up[l].astype(BF16), w_e_down[l].astype(BF16))
        x = _gather(pos, x.reshape(n, d), y, _row(norm_f_g),
                    final_norm=(l == depth - 1)).reshape(bsz, seq, d)
    return x
```

```python
import functools

import jax
import jax.numpy as jnp
from jax import lax
from jax.experimental import pallas as pl
from jax.experimental.pallas import tpu as pltpu

D_MODEL = 1024
D_CONV = D_MODEL // 2
D_POOL = D_MODEL // 2
D_SC = D_MODEL // 2
CONV_A_K = 31
SC_K = 3
POOL_WINDOWS = (2, 4, 8, 16)
POOL_GROUP_DIM = D_POOL // len(POOL_WINDOWS)
POOL_OUT_DIM = D_MODEL // len(POOL_WINDOWS)
N_MEM = 256
XATTN_HEADS = 4
XATTN_HEAD_DIM = D_MODEL // XATTN_HEADS
MOE_GROUPS = 4
EXPERTS_PER_GROUP = 4
N_EXPERTS = MOE_GROUPS * EXPERTS_PER_GROUP
D_EXPERT = D_MODEL // 4
EPS = 1e-6

OFF_A = 0
OFF_POOL = 2 * D_CONV
OFF_C = OFF_POOL + D_POOL
OFF_G = OFF_C + 3 * D_SC
D_IN_PROJ = OFF_G + 3 * D_MODEL

SUBLANES = 8
LANES = 128
VMEM_LIMIT_BYTES = 56 * 1024 * 1024

HIST_A = 32
HIST_POOL = 16
HIST_C = 8

SEQ_TILE = 256
CONV_ROW_CHUNK = 16
PROJ_SLICE = 512
ROUTER_LANES = LANES

PAIRS_PER_GROUP = EXPERTS_PER_GROUP * (EXPERTS_PER_GROUP - 1) // 2
N_CLASSES = MOE_GROUPS * PAIRS_PER_GROUP
EXPERT_ROW_TILE = 128
EXPERT_STEP_TILES = 4
ROUTER_ROWS = 32
ROW_MOVE_TILE = 512
VMEM_LIMIT_RESIDENT_BYTES = 58 * 1024 * 1024
H_TILES = D_MODEL // 2 // LANES
ROW_TILES = H_TILES + 1

BF16 = jnp.bfloat16
F32 = jnp.float32


def _sorted_rows(n_tokens):
    return n_tokens + N_CLASSES * EXPERT_ROW_TILE


def _sigmoid(v):
    return 0.5 * jnp.tanh(0.5 * v) + 0.5


def _rmsnorm(xf, g):
    return xf * lax.rsqrt(jnp.mean(xf * xf, axis=-1, keepdims=True) + EPS) * g


def _dot(a, b):
    return jnp.dot(a, b, preferred_element_type=F32)


def _const_spec(shape):
    nd = len(shape)
    return pl.BlockSpec(shape, lambda *_: (0,) * nd, pipeline_mode=pl.Buffered(1))


def _unpack_pair(q):
    return jnp.concatenate(
        [pltpu.unpack_elementwise(q, index=0, packed_dtype=BF16, unpacked_dtype=F32),
         pltpu.unpack_elementwise(q, index=1, packed_dtype=BF16, unpacked_dtype=F32)], axis=-1)


def _pack_pair(v):
    half = v.shape[-1] // 2
    return pltpu.pack_elementwise([v[:, :half], v[:, half:]], packed_dtype=BF16)


def _to_tile_rows(v):
    t, w = v.shape
    nj = w // LANES
    parts = [v[:, j * LANES:(j + 1) * LANES].reshape(t // SUBLANES, 1, SUBLANES, LANES)
             for j in range(nj)]
    return jnp.concatenate(parts, axis=1).reshape(nj * t, LANES)


def _from_tile_rows(v2, nj):
    t = v2.shape[0] // nj
    v4 = v2.reshape(t // SUBLANES, nj, SUBLANES, LANES)
    return [v4[:, j].reshape(t, LANES) for j in range(nj)]


def _tile_row_offset(r, nj):
    hi = jnp.floor(r / SUBLANES)
    return hi * (SUBLANES * nj) + (r - hi * SUBLANES)


def _mixer_kernel(x_ref, g_ref, w_in_ref, caw_ref, cab_ref, lng_ref, lnb_ref, w_a_out_ref,
                  w_pool_ref, pool_scale_ref, ccw_ref, w_c_out_ref, w_o_ref,
                  o_ref, abuf, ubuf, vbuf, actbuf, pbuf):
    ts = x_ref.shape[1]
    s = pl.program_id(1)

    @pl.when(s == 0)
    def _():
        abuf[0, 0:HIST_A, :] = jnp.zeros((HIST_A, D_CONV), F32)
        ubuf[0:HIST_POOL, :] = jnp.zeros((HIST_POOL, D_POOL), F32)
        vbuf[0:HIST_C, :] = jnp.zeros((HIST_C, D_SC), F32)

    x = x_ref[0]
    h = _rmsnorm(x, g_ref[...]).astype(BF16)

    def proj(off, width):
        return _dot(h, w_in_ref[:, off:off + width])

    pa = proj(OFF_A, 2 * D_CONV)
    abuf[0, HIST_A:HIST_A + ts, :] = pa[:, :D_CONV] * _sigmoid(pa[:, D_CONV:])
    shifted_rows = HIST_A + ts - SUBLANES
    for r in range(1, SUBLANES):
        abuf[r, 0:shifted_rows, :] = abuf[0, r:r + shifted_rows, :]
    chunk3 = (CONV_ROW_CHUNK // SUBLANES, SUBLANES, D_CONV)

    def conv_a_chunk(c0):
        acc = jnp.broadcast_to(cab_ref[...], chunk3)
        for k in range(CONV_A_K):
            q, r = divmod(HIST_A - (CONV_A_K - 1) + k, SUBLANES)
            lo = c0 + q * SUBLANES
            acc = acc + caw_ref[k] * abuf[r, lo:lo + CONV_ROW_CHUNK, :].reshape(chunk3)
        acc = acc.reshape(CONV_ROW_CHUNK, D_CONV)
        mu = jnp.mean(acc, axis=-1, keepdims=True)
        cen = acc - mu
        var = jnp.mean(cen * cen, axis=-1, keepdims=True)
        ln = cen * lax.rsqrt(var + EPS) * lng_ref[...] + lnb_ref[...]
        actbuf[c0:c0 + CONV_ROW_CHUNK, :] = (ln * _sigmoid(ln)).astype(BF16)

    chunks = list(range(0, ts, CONV_ROW_CHUNK))
    slices = list(range(OFF_POOL, D_IN_PROJ, PROJ_SLICE))
    for i in range(max(len(chunks), len(slices))):
        if i < len(chunks):
            conv_a_chunk(chunks[i])
        if i < len(slices):
            val = proj(slices[i], PROJ_SLICE)
            if slices[i] >= OFF_G:
                val = _sigmoid(val)
            pbuf[:, slices[i] - OFF_POOL:slices[i] - OFF_POOL + PROJ_SLICE] = val
    abuf[0, 0:HIST_A, :] = abuf[0, ts:ts + HIST_A, :]

    def pcol(off, width):
        return pbuf[:, off - OFF_POOL:off - OFF_POOL + width]

    pu = pcol(OFF_POOL, D_POOL)
    ubuf[HIST_POOL:HIST_POOL + ts, :] = pu
    gate_a = pcol(OFF_G, D_MODEL)
    gate_p = pcol(OFF_G + D_MODEL, D_MODEL)
    gate_c = pcol(OFF_G + 2 * D_MODEL, D_MODEL)
    merged = gate_a * _dot(actbuf[...], w_a_out_ref[...])

    t_glob = s * ts + lax.broadcasted_iota(jnp.int32, (ts, 1), 0)
    p_parts = []
    for i, w in enumerate(POOL_WINDOWS):
        c0, c1 = i * POOL_GROUP_DIM, (i + 1) * POOL_GROUP_DIM
        tok = pu[:, c0:c1]
        win = tok
        for j in range(1, w):
            win = win + ubuf[HIST_POOL - j:HIST_POOL - j + ts, c0:c1]
        cnt = jnp.minimum(t_glob + 1, w).astype(F32)
        pin = (win / cnt - tok).astype(BF16)
        p_parts.append(_dot(pin, w_pool_ref[i]))
    ubuf[0:HIST_POOL, :] = ubuf[ts:ts + HIST_POOL, :]
    merged = merged + gate_p * (jnp.concatenate(p_parts, axis=-1) * pool_scale_ref[...])

    c_b = pcol(OFF_C + D_SC, D_SC)
    v = pcol(OFF_C + 2 * D_SC, D_SC) * pcol(OFF_C, D_SC)
    vbuf[HIST_C:HIST_C + ts, :] = v
    conv_c = ccw_ref[SC_K - 1:SC_K, :] * v
    for k in range(SC_K - 1):
        off = HIST_C - (SC_K - 1) + k
        conv_c = conv_c + ccw_ref[k:k + 1, :] * vbuf[off:off + ts, :]
    vbuf[0:HIST_C, :] = vbuf[ts:ts + HIST_C, :]
    merged = merged + gate_c * _dot((c_b * conv_c).astype(BF16), w_c_out_ref[...])

    o_ref[0] = x + _dot(merged.astype(BF16), w_o_ref[...])


def _mixer(x, g, w_in, caw, cab, lng, lnb, w_a_out, w_pool, pool_scale, ccw, w_c_out, w_o):
    b, s, d = x.shape
    ts = SEQ_TILE
    consts = (g, w_in, caw, cab, lng, lnb, w_a_out, w_pool, pool_scale, ccw, w_c_out, w_o)
    return pl.pallas_call(
        _mixer_kernel,
        out_shape=jax.ShapeDtypeStruct(x.shape, F32),
        grid=(b, s // ts),
        in_specs=[pl.BlockSpec((1, ts, d), lambda i, j: (i, j, 0))]
        + [_const_spec(c.shape) for c in consts],
        out_specs=pl.BlockSpec((1, ts, d), lambda i, j: (i, j, 0)),
        scratch_shapes=[pltpu.VMEM((SUBLANES, HIST_A + ts, D_CONV), F32),
                        pltpu.VMEM((HIST_POOL + ts, D_POOL), F32),
                        pltpu.VMEM((HIST_C + ts, D_SC), F32),
                        pltpu.VMEM((ts, D_CONV), BF16),
                        pltpu.VMEM((ts, D_IN_PROJ - OFF_POOL), F32)],
        compiler_params=pltpu.CompilerParams(
            dimension_semantics=("arbitrary", "arbitrary"),
            vmem_limit_bytes=VMEM_LIMIT_BYTES),
        name="mixer",
    )(x, *consts)


def _kv_kernel(mem_ref, g_ref, w_kv_ref, kt_ref, v_ref):
    mn = _rmsnorm(mem_ref[0], g_ref[...]).astype(BF16)
    kv = _dot(mn, w_kv_ref[...])
    kt_ref[0] = kv[:, :D_MODEL].T.astype(BF16)
    v_ref[0] = kv[:, D_MODEL:].astype(BF16)


def _kv_proj(mem, g, w_kv):
    b, m, d = mem.shape
    return pl.pallas_call(
        _kv_kernel,
        out_shape=(jax.ShapeDtypeStruct((b, d, m), BF16), jax.ShapeDtypeStruct((b, m, d), BF16)),
        grid=(b,),
        in_specs=[pl.BlockSpec((1, m, d), lambda i: (i, 0, 0)),
                  _const_spec(g.shape), _const_spec(w_kv.shape)],
        out_specs=(pl.BlockSpec((1, d, m), lambda i: (i, 0, 0)),
                   pl.BlockSpec((1, m, d), lambda i: (i, 0, 0))),
        compiler_params=pltpu.CompilerParams(
            dimension_semantics=("arbitrary",), vmem_limit_bytes=VMEM_LIMIT_BYTES),
        name="kv_proj",
    )(mem, g, w_kv)


def _route_rows(lt):
    def first_argmax(vals, vmax):
        idx = jnp.full(vmax.shape, len(vals) - 1, jnp.int32)
        for k in range(len(vals) - 2, -1, -1):
            idx = jnp.where(vals[k] == vmax, k, idx)
        return idx

    g = [lt[k:k + 1, :] for k in range(MOE_GROUPS)]
    gmax = functools.reduce(jnp.maximum, g)
    g_idx = first_argmax(g, gmax)
    g_val = 1.0 / sum(jnp.exp(v - gmax) for v in g)
    e = []
    for j in range(EXPERTS_PER_GROUP):
        ej = lt[MOE_GROUPS + j:MOE_GROUPS + j + 1, :]
        for grp in range(1, MOE_GROUPS):
            r = MOE_GROUPS + grp * EXPERTS_PER_GROUP + j
            ej = jnp.where(g_idx == grp, lt[r:r + 1, :], ej)
        e.append(ej)
    m1 = functools.reduce(jnp.maximum, e)
    i1 = first_argmax(e, m1)
    e2 = [jnp.where(i1 == j, -jnp.inf, e[j]) for j in range(EXPERTS_PER_GROUP)]
    m2 = functools.reduce(jnp.maximum, e2)
    i2 = first_argmax(e2, m2)
    esum = sum(jnp.exp(v - m1) for v in e)
    p1 = 1.0 / esum
    p2 = jnp.exp(m2 - m1) / esum
    w1 = g_val * (p1 / (p1 + p2))
    w2 = g_val * (p2 / (p1 + p2))
    lo = jnp.minimum(i1, i2)
    hi = jnp.maximum(i1, i2)
    pair = ((lo * (2 * EXPERTS_PER_GROUP - 1 - lo)) >> 1) + (hi - lo - 1)
    cls = g_idx * PAIRS_PER_GROUP + pair
    first_is_low = i1 < i2
    return cls, jnp.where(first_is_low, w1, w2), jnp.where(first_is_low, w2, w1)


def _xattn_kernel(x_ref, g_ref, w_q_ref, kt_ref, v_ref, w_o_ref, gf_ref, w_r_ref, b_r_ref,
                  o_ref, hp_ref, cls_ref):
    x = x_ref[0]
    h = _rmsnorm(x, g_ref[...]).astype(BF16)
    q = (_dot(h, w_q_ref[...]) * (XATTN_HEAD_DIM ** -0.5)).astype(BF16)
    heads = []
    for hd in range(XATTN_HEADS):
        c0, c1 = hd * XATTN_HEAD_DIM, (hd + 1) * XATTN_HEAD_DIM
        sc = _dot(q[:, c0:c1], kt_ref[0, c0:c1, :])
        e = jnp.exp(sc - jnp.max(sc, axis=-1, keepdims=True))
        denom = jnp.sum(e, axis=-1, keepdims=True)
        heads.append(_dot(e.astype(BF16), v_ref[0, :, c0:c1]) / denom)
    o = jnp.concatenate(heads, axis=-1).astype(BF16)
    xn = x + _dot(o, w_o_ref[...])
    o_ref[0] = xn

    hf = _rmsnorm(xn, gf_ref[...])
    h_hi = hf.astype(BF16)
    h_lo = (hf - h_hi.astype(F32)).astype(BF16)
    nt_dims = (((1,), (1,)), ((), ()))
    part = lax.dot_general(w_r_ref[...], h_hi, nt_dims, preferred_element_type=F32)
    lt = (part[:ROUTER_ROWS] + part[ROUTER_ROWS:]
          + lax.dot_general(w_r_ref[:ROUTER_ROWS, :], h_lo, nt_dims, preferred_element_type=F32)
          + b_r_ref[...])
    cls, wa, wb = _route_rows(lt)
    ts = cls.shape[1]
    for c in range(ts // LANES):
        cls_ref[0, c:c + 1, :] = cls[:, c * LANES:(c + 1) * LANES]
    row = lax.broadcasted_iota(jnp.int32, (LANES, ts), 0)

    def weight_lanes(w):
        w_hi = w.astype(BF16).astype(F32)
        return jnp.where(row == 0, w_hi, jnp.where(row == 1, w - w_hi, 0.0)).T

    wt_tile = pltpu.pack_elementwise([weight_lanes(wa), weight_lanes(wb)], packed_dtype=BF16)
    hp_ref[...] = _to_tile_rows(jnp.concatenate([_pack_pair(hf), wt_tile], axis=-1))


def _xattn(x, g, w_q, kt, v, w_o, g_ffn, w_r, b_r):
    b, s, d = x.shape
    ts = SEQ_TILE
    nt = s // ts
    n = b * s
    return pl.pallas_call(
        _xattn_kernel,
        out_shape=(jax.ShapeDtypeStruct(x.shape, F32),
                   jax.ShapeDtypeStruct((ROW_TILES * n, LANES), jnp.uint32),
                   jax.ShapeDtypeStruct((n // ts, ts // LANES, LANES), jnp.int32)),
        grid=(b, nt),
        in_specs=[pl.BlockSpec((1, ts, d), lambda i, j: (i, j, 0)),
                  _const_spec(g.shape), _const_spec(w_q.shape),
                  pl.BlockSpec((1, d, N_MEM), lambda i, j: (i, 0, 0)),
                  pl.BlockSpec((1, N_MEM, d), lambda i, j: (i, 0, 0)),
                  _const_spec(w_o.shape), _const_spec(g_ffn.shape),
                  _const_spec(w_r.shape), _const_spec(b_r.shape)],
        out_specs=(pl.BlockSpec((1, ts, d), lambda i, j: (i, j, 0)),
                   pl.BlockSpec((ROW_TILES * ts, LANES), lambda i, j: (i * nt + j, 0)),
                   pl.BlockSpec((1, ts // LANES, LANES), lambda i, j: (i * nt + j, 0, 0))),
        compiler_params=pltpu.CompilerParams(
            dimension_semantics=("arbitrary", "arbitrary"),
            vmem_limit_bytes=VMEM_LIMIT_BYTES),
        name="xattn",
    )(x, g, w_q, kt, v, w_o, g_ffn, w_r, b_r)


def _plan_kernel(cls_ref, qs_ref, qy_ref, ea_ref, eb_ref, nu_ref):
    cls = cls_ref[...]
    nr, nl = cls.shape
    r = lax.broadcasted_iota(jnp.int32, (nl, nl), 0)
    c = lax.broadcasted_iota(jnp.int32, (nl, nl), 1)
    lanes_before = (r < c).astype(BF16)
    rr = lax.broadcasted_iota(jnp.int32, (nr, nr), 0)
    rc = lax.broadcasted_iota(jnp.int32, (nr, nr), 1)
    rows_before = (rc < rr).astype(BF16)
    ones_l = jnp.ones((nl, nl), BF16)
    ones_r = jnp.ones((nr, nr), BF16)
    n_tab = ea_ref.shape[1]
    tile_i = lax.broadcasted_iota(jnp.int32, (SUBLANES, n_tab), 1).astype(F32)
    run = jnp.zeros((nr, nl), F32)
    pos = jnp.zeros((nr, nl), F32)
    tile_cls = jnp.zeros((SUBLANES, n_tab), F32)
    for k in range(N_CLASSES):
        m = cls == k
        mb = jnp.where(m, 1.0, 0.0).astype(BF16)
        rowtot = _dot(mb, ones_l).astype(BF16)
        rank = _dot(mb, lanes_before) + _dot(rows_before, rowtot)
        total = _dot(ones_r, rowtot)
        pos = jnp.where(m, run * EXPERT_ROW_TILE + rank, pos)
        run = run + jnp.floor((total + (EXPERT_ROW_TILE - 1)) / EXPERT_ROW_TILE)
        end = jnp.concatenate([run[:SUBLANES, :]] * (n_tab // nl), axis=1)
        tile_cls = tile_cls + jnp.where(end <= tile_i, 1.0, 0.0)
    qs_ref[...] = _tile_row_offset(pos, ROW_TILES).astype(jnp.int32)
    qy_ref[...] = _tile_row_offset(pos, H_TILES).astype(jnp.int32)
    tc = jnp.minimum(tile_cls, N_CLASSES - 1.0).astype(jnp.int32)
    grp = sum(jnp.where(tc >= g * PAIRS_PER_GROUP, 1, 0) for g in range(1, MOE_GROUPS))
    pair = tc - grp * PAIRS_PER_GROUP
    pair_lo = jnp.where(pair >= 3, 1, 0) + jnp.where(pair >= 5, 1, 0)
    pair_hi = pair + 1 - jnp.where(pair >= 5, 3, jnp.where(pair >= 3, 2, 0))
    ea_ref[...] = grp * EXPERTS_PER_GROUP + pair_lo
    eb_ref[...] = grp * EXPERTS_PER_GROUP + pair_hi
    nu_ref[...] = run[:SUBLANES, :].astype(jnp.int32)


def _plan(cls2):
    nr, nl = cls2.shape
    n_tiles = _sorted_rows(nr * nl) // EXPERT_ROW_TILE
    n_tab = -(-n_tiles // nl) * nl
    vmem = pl.BlockSpec(memory_space=pltpu.VMEM)
    qs, qy, ea, eb, nu = pl.pallas_call(
        _plan_kernel,
        out_shape=(jax.ShapeDtypeStruct((nr, nl), jnp.int32),
                   jax.ShapeDtypeStruct((nr, nl), jnp.int32),
                   jax.ShapeDtypeStruct((SUBLANES, n_tab), jnp.int32),
                   jax.ShapeDtypeStruct((SUBLANES, n_tab), jnp.int32),
                   jax.ShapeDtypeStruct((SUBLANES, nl), jnp.int32)),
        in_specs=[vmem],
        out_specs=(vmem, vmem, vmem, vmem, vmem),
        name="moe_plan",
    )(cls2)
    return (qs.reshape(nr * nl), qy.reshape(nr * nl),
            ea[0, :n_tiles], eb[0, :n_tiles], nu[0, :1])


def _scatter_kernel(qs_ref, hp_ref, s_ref):
    i = pl.program_id(0)
    tn = hp_ref.shape[0] // ROW_TILES
    group = SUBLANES * ROW_TILES

    @pl.when(i == 0)
    def _():
        zero_rows = _pack_pair(jnp.zeros((hp_ref.shape[0], 2 * LANES), F32))

        def fill(c, carry):
            start = pl.multiple_of(c * hp_ref.shape[0], SUBLANES)
            s_ref[pl.ds(start, hp_ref.shape[0]), :] = zero_rows
            return carry

        lax.fori_loop(0, s_ref.shape[0] // hp_ref.shape[0], fill, 0)

    def body(j, carry):
        src = pl.multiple_of(j * group, SUBLANES)
        for k in range(SUBLANES):
            q = qs_ref[i * tn + j * SUBLANES + k]
            s_ref[pl.ds(q, ROW_TILES, stride=SUBLANES), :] = (
                hp_ref[pl.ds(src + k, ROW_TILES, stride=SUBLANES), :])
        return carry

    lax.fori_loop(0, tn // SUBLANES, body, 0)


def _scatter(qs, hp):
    n = qs.shape[0]
    tn = ROW_MOVE_TILE
    rows = ROW_TILES * _sorted_rows(n)
    grid_spec = pltpu.PrefetchScalarGridSpec(
        num_scalar_prefetch=1,
        grid=(n // tn,),
        in_specs=[pl.BlockSpec((ROW_TILES * tn, LANES), lambda i, qs: (i, 0))],
        out_specs=pl.BlockSpec((rows, LANES), lambda i, qs: (0, 0), pipeline_mode=pl.Buffered(1)),
    )
    return pl.pallas_call(
        _scatter_kernel,
        out_shape=jax.ShapeDtypeStruct((rows, LANES), jnp.uint32),
        grid_spec=grid_spec,
        compiler_params=pltpu.CompilerParams(
            dimension_semantics=("arbitrary",), vmem_limit_bytes=VMEM_LIMIT_RESIDENT_BYTES),
        name="moe_scatter",
    )(qs, hp)


def _expert_kernel(ea_ref, eb_ref, nu_ref, s_ref, wg_ref, wu_ref, wd_ref, y_ref):
    i = pl.program_id(0)
    first = i * EXPERT_STEP_TILES
    rec = EXPERT_ROW_TILE * ROW_TILES
    out = EXPERT_ROW_TILE * H_TILES

    @pl.when(first < nu_ref[0])
    def _():
        for j in range(EXPERT_STEP_TILES):
            t = first + j
            tiles = _from_tile_rows(s_ref[j * rec:(j + 1) * rec, :], ROW_TILES)
            hb = _unpack_pair(jnp.concatenate(tiles[:H_TILES], axis=-1)).astype(BF16)
            y = None
            for e_ref, idx in ((ea_ref, 0), (eb_ref, 1)):
                e = e_ref[t]
                wt = pltpu.unpack_elementwise(tiles[H_TILES], index=idx, packed_dtype=BF16,
                                              unpacked_dtype=F32)
                wt = wt[:, 0:1] + wt[:, 1:2]
                gt = _dot(hb, wg_ref[e])
                act = gt * _sigmoid(gt) * _dot(hb, wu_ref[e]) * wt
                part = _dot(act.astype(BF16), wd_ref[e])
                y = part if y is None else y + part
            y_ref[j * out:(j + 1) * out, :] = _to_tile_rows(_pack_pair(y))

    @pl.when(first >= nu_ref[0])
    def _():
        y_ref[...] = _pack_pair(jnp.zeros((y_ref.shape[0], 2 * LANES), F32))


def _experts(tile_ea, tile_eb, n_used, s, wg, wu, wd):
    rows = s.shape[0] // ROW_TILES
    tb = EXPERT_ROW_TILE * EXPERT_STEP_TILES
    grid_spec = pltpu.PrefetchScalarGridSpec(
        num_scalar_prefetch=3,
        grid=(rows // tb,),
        in_specs=[pl.BlockSpec((ROW_TILES * tb, LANES), lambda i, *_: (i, 0)),
                  _const_spec(wg.shape), _const_spec(wu.shape), _const_spec(wd.shape)],
        out_specs=pl.BlockSpec((H_TILES * tb, LANES), lambda i, *_: (i, 0)),
    )
    return pl.pallas_call(
        _expert_kernel,
        out_shape=jax.ShapeDtypeStruct((H_TILES * rows, LANES), jnp.uint32),
        grid_spec=grid_spec,
        compiler_params=pltpu.CompilerParams(
            dimension_semantics=("arbitrary",), vmem_limit_bytes=VMEM_LIMIT_BYTES),
        name="moe_experts",
    )(tile_ea, tile_eb, n_used, s, wg, wu, wd)


def _gather_kernel(qy_ref, x_ref, y_ref, gn_ref, o_ref, rows_ref, *, final_norm):
    i = pl.program_id(0)
    tn = x_ref.shape[0]
    group = SUBLANES * H_TILES

    def body(j, carry):
        dst = pl.multiple_of(j * group, SUBLANES)
        for k in range(SUBLANES):
            q = qy_ref[i * tn + j * SUBLANES + k]
            rows_ref[pl.ds(dst + k, H_TILES, stride=SUBLANES), :] = (
                y_ref[pl.ds(q, H_TILES, stride=SUBLANES), :])
        return carry

    lax.fori_loop(0, tn // SUBLANES, body, 0)
    packed = jnp.concatenate(_from_tile_rows(rows_ref[...], H_TILES), axis=-1)
    out = x_ref[...] + _unpack_pair(packed)
    if final_norm:
        out = _rmsnorm(out, gn_ref[...])
    o_ref[...] = out


def _gather(qy, x2, y, g_final, final_norm):
    n, d = x2.shape
    tn = ROW_MOVE_TILE
    grid_spec = pltpu.PrefetchScalarGridSpec(
        num_scalar_prefetch=1,
        grid=(n // tn,),
        in_specs=[pl.BlockSpec((tn, d), lambda i, qy: (i, 0)),
                  pl.BlockSpec(y.shape, lambda i, qy: (0, 0), pipeline_mode=pl.Buffered(1)),
                  pl.BlockSpec(g_final.shape, lambda i, qy: (0, 0), pipeline_mode=pl.Buffered(1))],
        out_specs=pl.BlockSpec((tn, d), lambda i, qy: (i, 0)),
        scratch_shapes=[pltpu.VMEM((H_TILES * tn, LANES), jnp.uint32)],
    )
    return pl.pallas_call(
        functools.partial(_gather_kernel, final_norm=final_norm),
        out_shape=jax.ShapeDtypeStruct(x2.shape, F32),
        grid_spec=grid_spec,
        compiler_params=pltpu.CompilerParams(
            dimension_semantics=("arbitrary",), vmem_limit_bytes=VMEM_LIMIT_RESIDENT_BYTES),
        name="moe_gather",
    )(qy, x2, y, g_final)


def _row(v):
    return v.reshape(1, -1)


def _router_params(w_rg, b_rg, w_re, b_re):
    pad = ROUTER_ROWS - MOE_GROUPS - N_EXPERTS
    wt = jnp.pad(jnp.concatenate([w_rg, w_re], axis=1).T, ((0, pad), (0, 0)))
    w_hi = wt.astype(BF16)
    w_lo = (wt - w_hi.astype(F32)).astype(BF16)
    b_col = jnp.pad(jnp.concatenate([b_rg, b_re]), (0, pad)).reshape(ROUTER_ROWS, 1)
    return jnp.concatenate([w_hi, w_lo], axis=0), b_col


def kernel(x, mem, norm_mix_g, w_in, conv_a_w, conv_a_b, ln_a_g, ln_a_b, w_a_out, w_pool_grp, pool_scale, conv_c_w, w_c_out, w_o, norm_x_g, norm_mem_g, w_xq, w_xkv, w_xo, norm_ffn_g, w_rg, b_rg, w_re, b_re, w_e_gate, w_e_up, w_e_down, norm_f_g):
    bsz, seq, d = x.shape
    n = bsz * seq
    depth = w_in.shape[0]
    for l in range(depth):
        caw = jnp.broadcast_to(conv_a_w[l][:, None, :], (CONV_A_K, SUBLANES, D_CONV))
        x = _mixer(x, _row(norm_mix_g[l]), w_in[l].astype(BF16), caw, _row(conv_a_b[l]),
                   _row(ln_a_g[l]), _row(ln_a_b[l]), w_a_out[l].astype(BF16),
                   w_pool_grp[l].astype(BF16), _row(pool_scale[l]), conv_c_w[l],
                   w_c_out[l].astype(BF16), w_o[l].astype(BF16))
        kt, v = _kv_proj(mem, _row(norm_mem_g[l]), w_xkv[l].astype(BF16))
        w_r, b_r = _router_params(w_rg[l], b_rg[l], w_re[l], b_re[l])
        x, hp, cls = _xattn(x, _row(norm_x_g[l]), w_xq[l].astype(BF16), kt, v,
                                  w_xo[l].astype(BF16), _row(norm_ffn_g[l]), w_r, b_r)
        qs, qy, tile_ea, tile_eb, n_used = _plan(cls.reshape(n // LANES, LANES))
        y = _experts(tile_ea, tile_eb, n_used, _scatter(qs, hp), w_e_gate[l].astype(BF16),
                     w_e_up[l].astype(BF16), w_e_down[l].astype(BF16))
        x = _gather(qy, x.reshape(n, d), y, _row(norm_f_g),
                    final_norm=(l == depth - 1)).reshape(bsz, seq, d)
    return x
```

```python
import functools

import jax
import jax.numpy as jnp
from jax import lax
from jax.experimental import pallas as pl
from jax.experimental.pallas import tpu as pltpu

D_MODEL = 1024
D_CONV = D_MODEL // 2
D_POOL = D_MODEL // 2
D_SC = D_MODEL // 2
CONV_A_K = 31
SC_K = 3
POOL_WINDOWS = (2, 4, 8, 16)
POOL_GROUP_DIM = D_POOL // len(POOL_WINDOWS)
POOL_OUT_DIM = D_MODEL // len(POOL_WINDOWS)
N_MEM = 256
XATTN_HEADS = 4
XATTN_HEAD_DIM = D_MODEL // XATTN_HEADS
MOE_GROUPS = 4
EXPERTS_PER_GROUP = 4
N_EXPERTS = MOE_GROUPS * EXPERTS_PER_GROUP
D_EXPERT = D_MODEL // 4
EPS = 1e-6

OFF_A = 0
OFF_POOL = 2 * D_CONV
OFF_C = OFF_POOL + D_POOL
OFF_G = OFF_C + 3 * D_SC
D_IN_PROJ = OFF_G + 3 * D_MODEL

SUBLANES = 8
LANES = 128
VMEM_LIMIT_BYTES = 56 * 1024 * 1024

HIST_A = 32
HIST_POOL = 16
HIST_C = 8

SEQ_TILE = 256
CONV_ROW_CHUNK = 16
ROUTER_LANES = LANES

PAIRS_PER_GROUP = EXPERTS_PER_GROUP * (EXPERTS_PER_GROUP - 1) // 2
N_CLASSES = MOE_GROUPS * PAIRS_PER_GROUP
EXPERT_ROW_TILE = 128
EXPERT_STEP_TILES = 4
ROUTER_ROWS = 32
ROW_MOVE_TILE = 512
VMEM_LIMIT_RESIDENT_BYTES = 58 * 1024 * 1024
H_TILES = D_MODEL // 2 // LANES
ROW_TILES = H_TILES + 1

BF16 = jnp.bfloat16
F32 = jnp.float32


def _sorted_rows(n_tokens):
    return n_tokens + N_CLASSES * EXPERT_ROW_TILE


def _sigmoid(v):
    return 0.5 * jnp.tanh(0.5 * v) + 0.5


def _rmsnorm(xf, g):
    return xf * lax.rsqrt(jnp.mean(xf * xf, axis=-1, keepdims=True) + EPS) * g


def _dot(a, b):
    return jnp.dot(a, b, preferred_element_type=F32)


def _const_spec(shape):
    nd = len(shape)
    return pl.BlockSpec(shape, lambda *_: (0,) * nd, pipeline_mode=pl.Buffered(1))


def _unpack_pair(q):
    return jnp.concatenate(
        [pltpu.unpack_elementwise(q, index=0, packed_dtype=BF16, unpacked_dtype=F32),
         pltpu.unpack_elementwise(q, index=1, packed_dtype=BF16, unpacked_dtype=F32)], axis=-1)


def _pack_pair(v):
    half = v.shape[-1] // 2
    return pltpu.pack_elementwise([v[:, :half], v[:, half:]], packed_dtype=BF16)


def _to_tile_rows(v):
    t, w = v.shape
    nj = w // LANES
    parts = [v[:, j * LANES:(j + 1) * LANES].reshape(t // SUBLANES, 1, SUBLANES, LANES)
             for j in range(nj)]
    return jnp.concatenate(parts, axis=1).reshape(nj * t, LANES)


def _from_tile_rows(v2, nj):
    t = v2.shape[0] // nj
    v4 = v2.reshape(t // SUBLANES, nj, SUBLANES, LANES)
    return [v4[:, j].reshape(t, LANES) for j in range(nj)]


def _tile_row_offset(r, nj):
    hi = jnp.floor(r / SUBLANES)
    return hi * (SUBLANES * nj) + (r - hi * SUBLANES)


def _mixer_kernel(x_ref, g_ref, w_in_ref, caw_ref, cab_ref, lng_ref, lnb_ref, w_a_out_ref,
                  w_pool_ref, pool_scale_ref, ccw_ref, w_c_out_ref, w_o_ref,
                  o_ref, abuf, ubuf, vbuf, actbuf, pbuf):
    ts = x_ref.shape[1]
    s = pl.program_id(1)

    @pl.when(s == 0)
    def _():
        abuf[0, 0:HIST_A, :] = jnp.zeros((HIST_A, D_CONV), F32)
        ubuf[0:HIST_POOL, :] = jnp.zeros((HIST_POOL, D_POOL), F32)
        vbuf[0:HIST_C, :] = jnp.zeros((HIST_C, D_SC), F32)

    x = x_ref[0]
    h = _rmsnorm(x, g_ref[...]).astype(BF16)

    def proj(off, width):
        return _dot(h, w_in_ref[:, off:off + width])

    pa = proj(OFF_A, 2 * D_CONV)
    abuf[0, HIST_A:HIST_A + ts, :] = pa[:, :D_CONV] * _sigmoid(pa[:, D_CONV:])
    shifted_rows = HIST_A + ts - SUBLANES
    for r in range(1, SUBLANES):
        abuf[r, 0:shifted_rows, :] = abuf[0, r:r + shifted_rows, :]
    chunk3 = (CONV_ROW_CHUNK // SUBLANES, SUBLANES, D_CONV)
    for c0 in range(0, ts, CONV_ROW_CHUNK):
        acc = jnp.broadcast_to(cab_ref[...], chunk3)
        for k in range(CONV_A_K):
            q, r = divmod(HIST_A - (CONV_A_K - 1) + k, SUBLANES)
            lo = c0 + q * SUBLANES
            acc = acc + caw_ref[k] * abuf[r, lo:lo + CONV_ROW_CHUNK, :].reshape(chunk3)
        acc = acc.reshape(CONV_ROW_CHUNK, D_CONV)
        mu = jnp.mean(acc, axis=-1, keepdims=True)
        cen = acc - mu
        var = jnp.mean(cen * cen, axis=-1, keepdims=True)
        ln = cen * lax.rsqrt(var + EPS) * lng_ref[...] + lnb_ref[...]
        actbuf[c0:c0 + CONV_ROW_CHUNK, :] = (ln * _sigmoid(ln)).astype(BF16)
    abuf[0, 0:HIST_A, :] = abuf[0, ts:ts + HIST_A, :]

    pbuf[...] = proj(OFF_POOL, D_IN_PROJ - OFF_POOL)

    def pcol(off, width):
        return pbuf[:, off - OFF_POOL:off - OFF_POOL + width]

    pu = pcol(OFF_POOL, D_POOL)
    ubuf[HIST_POOL:HIST_POOL + ts, :] = pu
    merged = _sigmoid(pcol(OFF_G, D_MODEL)) * _dot(actbuf[...], w_a_out_ref[...])

    t_glob = s * ts + lax.broadcasted_iota(jnp.int32, (ts, 1), 0)
    p_parts = []
    for i, w in enumerate(POOL_WINDOWS):
        c0, c1 = i * POOL_GROUP_DIM, (i + 1) * POOL_GROUP_DIM
        tok = pu[:, c0:c1]
        win = tok
        for j in range(1, w):
            win = win + ubuf[HIST_POOL - j:HIST_POOL - j + ts, c0:c1]
        cnt = jnp.minimum(t_glob + 1, w).astype(F32)
        pin = (win / cnt - tok).astype(BF16)
        p_parts.append(_dot(pin, w_pool_ref[i]))
    ubuf[0:HIST_POOL, :] = ubuf[ts:ts + HIST_POOL, :]
    merged = merged + _sigmoid(pcol(OFF_G + D_MODEL, D_MODEL)) * (jnp.concatenate(p_parts, axis=-1) * pool_scale_ref[...])

    c_b = pcol(OFF_C + D_SC, D_SC)
    v = pcol(OFF_C + 2 * D_SC, D_SC) * pcol(OFF_C, D_SC)
    vbuf[HIST_C:HIST_C + ts, :] = v
    conv_c = ccw_ref[SC_K - 1:SC_K, :] * v
    for k in range(SC_K - 1):
        off = HIST_C - (SC_K - 1) + k
        conv_c = conv_c + ccw_ref[k:k + 1, :] * vbuf[off:off + ts, :]
    vbuf[0:HIST_C, :] = vbuf[ts:ts + HIST_C, :]
    merged = merged + _sigmoid(pcol(OFF_G + 2 * D_MODEL, D_MODEL)) * _dot((c_b * conv_c).astype(BF16), w_c_out_ref[...])

    o_ref[0] = x + _dot(merged.astype(BF16), w_o_ref[...])


def _mixer(x, g, w_in, caw, cab, lng, lnb, w_a_out, w_pool, pool_scale, ccw, w_c_out, w_o):
    b, s, d = x.shape
    ts = SEQ_TILE
    consts = (g, w_in, caw, cab, lng, lnb, w_a_out, w_pool, pool_scale, ccw, w_c_out, w_o)
    return pl.pallas_call(
        _mixer_kernel,
        out_shape=jax.ShapeDtypeStruct(x.shape, F32),
        grid=(b, s // ts),
        in_specs=[pl.BlockSpec((1, ts, d), lambda i, j: (i, j, 0))]
        + [_const_spec(c.shape) for c in consts],
        out_specs=pl.BlockSpec((1, ts, d), lambda i, j: (i, j, 0)),
        scratch_shapes=[pltpu.VMEM((SUBLANES, HIST_A + ts, D_CONV), F32),
                        pltpu.VMEM((HIST_POOL + ts, D_POOL), F32),
                        pltpu.VMEM((HIST_C + ts, D_SC), F32),
                        pltpu.VMEM((ts, D_CONV), BF16),
                        pltpu.VMEM((ts, D_IN_PROJ - OFF_POOL), F32)],
        compiler_params=pltpu.CompilerParams(
            dimension_semantics=("arbitrary", "arbitrary"),
            vmem_limit_bytes=VMEM_LIMIT_BYTES),
        name="mixer",
    )(x, *consts)


def _kv_kernel(mem_ref, g_ref, w_kv_ref, kt_ref, v_ref):
    mn = _rmsnorm(mem_ref[0], g_ref[...]).astype(BF16)
    kv = _dot(mn, w_kv_ref[...])
    kt_ref[0] = kv[:, :D_MODEL].T.astype(BF16)
    v_ref[0] = kv[:, D_MODEL:].astype(BF16)


def _kv_proj(mem, g, w_kv):
    b, m, d = mem.shape
    return pl.pallas_call(
        _kv_kernel,
        out_shape=(jax.ShapeDtypeStruct((b, d, m), BF16), jax.ShapeDtypeStruct((b, m, d), BF16)),
        grid=(b,),
        in_specs=[pl.BlockSpec((1, m, d), lambda i: (i, 0, 0)),
                  _const_spec(g.shape), _const_spec(w_kv.shape)],
        out_specs=(pl.BlockSpec((1, d, m), lambda i: (i, 0, 0)),
                   pl.BlockSpec((1, m, d), lambda i: (i, 0, 0))),
        compiler_params=pltpu.CompilerParams(
            dimension_semantics=("arbitrary",), vmem_limit_bytes=VMEM_LIMIT_BYTES),
        name="kv_proj",
    )(mem, g, w_kv)


def _route_rows(lt):
    def first_argmax(vals, vmax):
        idx = jnp.full(vmax.shape, len(vals) - 1, jnp.int32)
        for k in range(len(vals) - 2, -1, -1):
            idx = jnp.where(vals[k] == vmax, k, idx)
        return idx

    g = [lt[k:k + 1, :] for k in range(MOE_GROUPS)]
    gmax = functools.reduce(jnp.maximum, g)
    g_idx = first_argmax(g, gmax)
    g_val = 1.0 / sum(jnp.exp(v - gmax) for v in g)
    e = []
    for j in range(EXPERTS_PER_GROUP):
        ej = lt[MOE_GROUPS + j:MOE_GROUPS + j + 1, :]
        for grp in range(1, MOE_GROUPS):
            r = MOE_GROUPS + grp * EXPERTS_PER_GROUP + j
            ej = jnp.where(g_idx == grp, lt[r:r + 1, :], ej)
        e.append(ej)
    m1 = functools.reduce(jnp.maximum, e)
    i1 = first_argmax(e, m1)
    e2 = [jnp.where(i1 == j, -jnp.inf, e[j]) for j in range(EXPERTS_PER_GROUP)]
    m2 = functools.reduce(jnp.maximum, e2)
    i2 = first_argmax(e2, m2)
    esum = sum(jnp.exp(v - m1) for v in e)
    p1 = 1.0 / esum
    p2 = jnp.exp(m2 - m1) / esum
    w1 = g_val * (p1 / (p1 + p2))
    w2 = g_val * (p2 / (p1 + p2))
    lo = jnp.minimum(i1, i2)
    hi = jnp.maximum(i1, i2)
    pair = ((lo * (2 * EXPERTS_PER_GROUP - 1 - lo)) >> 1) + (hi - lo - 1)
    cls = g_idx * PAIRS_PER_GROUP + pair
    first_is_low = i1 < i2
    return cls, jnp.where(first_is_low, w1, w2), jnp.where(first_is_low, w2, w1)


def _xattn_kernel(x_ref, g_ref, w_q_ref, kt_ref, v_ref, w_o_ref, gf_ref, w_r_ref, b_r_ref,
                  o_ref, hp_ref, cls_ref, xn_prev):
    @pl.when(pl.program_id(0) == 0)
    def _():
        xn_prev[...] = jnp.zeros(xn_prev.shape, F32)

    xp = xn_prev[...]
    x = x_ref[0]
    h = _rmsnorm(x, g_ref[...]).astype(BF16)
    q = (_dot(h, w_q_ref[...]) * (XATTN_HEAD_DIM ** -0.5)).astype(BF16)
    heads = []
    for hd in range(XATTN_HEADS):
        c0, c1 = hd * XATTN_HEAD_DIM, (hd + 1) * XATTN_HEAD_DIM
        sc = _dot(q[:, c0:c1], kt_ref[0, c0:c1, :])
        e = jnp.exp(sc - jnp.max(sc, axis=-1, keepdims=True))
        denom = jnp.sum(e, axis=-1, keepdims=True)
        heads.append(_dot(e.astype(BF16), v_ref[0, :, c0:c1]) / denom)
    o = jnp.concatenate(heads, axis=-1).astype(BF16)
    xn = x + _dot(o, w_o_ref[...])
    o_ref[0] = xn
    xn_prev[...] = xn

    hf = _rmsnorm(xp, gf_ref[...])
    h_hi = hf.astype(BF16)
    h_lo = (hf - h_hi.astype(F32)).astype(BF16)
    nt_dims = (((1,), (1,)), ((), ()))
    part = lax.dot_general(w_r_ref[...], h_hi, nt_dims, preferred_element_type=F32)
    lt = (part[:ROUTER_ROWS] + part[ROUTER_ROWS:]
          + lax.dot_general(w_r_ref[:ROUTER_ROWS, :], h_lo, nt_dims, preferred_element_type=F32)
          + b_r_ref[...])
    cls, wa, wb = _route_rows(lt)
    ts = cls.shape[1]
    for c in range(ts // LANES):
        cls_ref[0, c:c + 1, :] = cls[:, c * LANES:(c + 1) * LANES]
    row = lax.broadcasted_iota(jnp.int32, (LANES, ts), 0)

    def weight_lanes(w):
        w_hi = w.astype(BF16).astype(F32)
        return jnp.where(row == 0, w_hi, jnp.where(row == 1, w - w_hi, 0.0)).T

    wt_tile = pltpu.pack_elementwise([weight_lanes(wa), weight_lanes(wb)], packed_dtype=BF16)
    hp_ref[...] = _to_tile_rows(jnp.concatenate([_pack_pair(hf), wt_tile], axis=-1))


def _xattn(x, g, w_q, kt, v, w_o, g_ffn, w_r, b_r):
    b, s, d = x.shape
    ts = SEQ_TILE
    nt = s // ts
    n = b * s
    tiles = b * nt

    def attn_tile(step):
        return jnp.minimum(step, tiles - 1)

    def route_tile(step):
        return jnp.maximum(step - 1, 0)

    return pl.pallas_call(
        _xattn_kernel,
        out_shape=(jax.ShapeDtypeStruct(x.shape, F32),
                   jax.ShapeDtypeStruct((ROW_TILES * n, LANES), jnp.uint32),
                   jax.ShapeDtypeStruct((n // ts, ts // LANES, LANES), jnp.int32)),
        grid=(tiles + 1,),
        in_specs=[pl.BlockSpec((1, ts, d), lambda i: (attn_tile(i) // nt, attn_tile(i) % nt, 0)),
                  _const_spec(g.shape), _const_spec(w_q.shape),
                  pl.BlockSpec((1, d, N_MEM), lambda i: (attn_tile(i) // nt, 0, 0)),
                  pl.BlockSpec((1, N_MEM, d), lambda i: (attn_tile(i) // nt, 0, 0)),
                  _const_spec(w_o.shape), _const_spec(g_ffn.shape),
                  _const_spec(w_r.shape), _const_spec(b_r.shape)],
        out_specs=(pl.BlockSpec((1, ts, d), lambda i: (attn_tile(i) // nt, attn_tile(i) % nt, 0)),
                   pl.BlockSpec((ROW_TILES * ts, LANES), lambda i: (route_tile(i), 0)),
                   pl.BlockSpec((1, ts // LANES, LANES), lambda i: (route_tile(i), 0, 0))),
        scratch_shapes=[pltpu.VMEM((ts, d), F32)],
        compiler_params=pltpu.CompilerParams(
            dimension_semantics=("arbitrary",), vmem_limit_bytes=VMEM_LIMIT_BYTES),
        name="xattn",
    )(x, g, w_q, kt, v, w_o, g_ffn, w_r, b_r)


def _plan_kernel(cls_ref, qs_ref, qy_ref, ea_ref, eb_ref, nu_ref):
    cls = cls_ref[...]
    nr, nl = cls.shape
    r = lax.broadcasted_iota(jnp.int32, (nl, nl), 0)
    c = lax.broadcasted_iota(jnp.int32, (nl, nl), 1)
    lanes_before = (r < c).astype(BF16)
    rr = lax.broadcasted_iota(jnp.int32, (nr, nr), 0)
    rc = lax.broadcasted_iota(jnp.int32, (nr, nr), 1)
    rows_before = (rc < rr).astype(BF16)
    ones_l = jnp.ones((nl, nl), BF16)
    ones_r = jnp.ones((nr, nr), BF16)
    n_tab = ea_ref.shape[1]
    tile_i = lax.broadcasted_iota(jnp.int32, (SUBLANES, n_tab), 1).astype(F32)
    run = jnp.zeros((nr, nl), F32)
    pos = jnp.zeros((nr, nl), F32)
    tile_cls = jnp.zeros((SUBLANES, n_tab), F32)
    for k in range(N_CLASSES):
        m = cls == k
        mb = jnp.where(m, 1.0, 0.0).astype(BF16)
        rowtot = _dot(mb, ones_l).astype(BF16)
        rank = _dot(mb, lanes_before) + _dot(rows_before, rowtot)
        total = _dot(ones_r, rowtot)
        pos = jnp.where(m, run * EXPERT_ROW_TILE + rank, pos)
        run = run + jnp.floor((total + (EXPERT_ROW_TILE - 1)) / EXPERT_ROW_TILE)
        end = jnp.concatenate([run[:SUBLANES, :]] * (n_tab // nl), axis=1)
        tile_cls = tile_cls + jnp.where(end <= tile_i, 1.0, 0.0)
    qs_ref[...] = _tile_row_offset(pos, ROW_TILES).astype(jnp.int32)
    qy_ref[...] = _tile_row_offset(pos, H_TILES).astype(jnp.int32)
    tc = jnp.minimum(tile_cls, N_CLASSES - 1.0).astype(jnp.int32)
    grp = sum(jnp.where(tc >= g * PAIRS_PER_GROUP, 1, 0) for g in range(1, MOE_GROUPS))
    pair = tc - grp * PAIRS_PER_GROUP
    pair_lo = jnp.where(pair >= 3, 1, 0) + jnp.where(pair >= 5, 1, 0)
    pair_hi = pair + 1 - jnp.where(pair >= 5, 3, jnp.where(pair >= 3, 2, 0))
    ea_ref[...] = grp * EXPERTS_PER_GROUP + pair_lo
    eb_ref[...] = grp * EXPERTS_PER_GROUP + pair_hi
    nu_ref[...] = run[:SUBLANES, :].astype(jnp.int32)


def _plan(cls2):
    nr, nl = cls2.shape
    n_tiles = _sorted_rows(nr * nl) // EXPERT_ROW_TILE
    n_tab = -(-n_tiles // nl) * nl
    vmem = pl.BlockSpec(memory_space=pltpu.VMEM)
    qs, qy, ea, eb, nu = pl.pallas_call(
        _plan_kernel,
        out_shape=(jax.ShapeDtypeStruct((nr, nl), jnp.int32),
                   jax.ShapeDtypeStruct((nr, nl), jnp.int32),
                   jax.ShapeDtypeStruct((SUBLANES, n_tab), jnp.int32),
                   jax.ShapeDtypeStruct((SUBLANES, n_tab), jnp.int32),
                   jax.ShapeDtypeStruct((SUBLANES, nl), jnp.int32)),
        in_specs=[vmem],
        out_specs=(vmem, vmem, vmem, vmem, vmem),
        name="moe_plan",
    )(cls2)
    return (qs.reshape(nr * nl), qy.reshape(nr * nl),
            ea[0, :n_tiles], eb[0, :n_tiles], nu[0, :1])


def _scatter_kernel(qs_ref, hp_ref, s_ref):
    i = pl.program_id(0)
    tn = hp_ref.shape[0] // ROW_TILES
    group = SUBLANES * ROW_TILES

    @pl.when(i == 0)
    def _():
        zero_rows = _pack_pair(jnp.zeros((hp_ref.shape[0], 2 * LANES), F32))

        def fill(c, carry):
            start = pl.multiple_of(c * hp_ref.shape[0], SUBLANES)
            s_ref[pl.ds(start, hp_ref.shape[0]), :] = zero_rows
            return carry

        lax.fori_loop(0, s_ref.shape[0] // hp_ref.shape[0], fill, 0)

    def body(j, carry):
        src = pl.multiple_of(j * group, SUBLANES)
        for k in range(SUBLANES):
            q = qs_ref[i * tn + j * SUBLANES + k]
            s_ref[pl.ds(q, ROW_TILES, stride=SUBLANES), :] = (
                hp_ref[pl.ds(src + k, ROW_TILES, stride=SUBLANES), :])
        return carry

    lax.fori_loop(0, tn // SUBLANES, body, 0)


def _scatter(qs, hp):
    n = qs.shape[0]
    tn = ROW_MOVE_TILE
    rows = ROW_TILES * _sorted_rows(n)
    grid_spec = pltpu.PrefetchScalarGridSpec(
        num_scalar_prefetch=1,
        grid=(n // tn,),
        in_specs=[pl.BlockSpec((ROW_TILES * tn, LANES), lambda i, qs: (i, 0))],
        out_specs=pl.BlockSpec((rows, LANES), lambda i, qs: (0, 0), pipeline_mode=pl.Buffered(1)),
    )
    return pl.pallas_call(
        _scatter_kernel,
        out_shape=jax.ShapeDtypeStruct((rows, LANES), jnp.uint32),
        grid_spec=grid_spec,
        compiler_params=pltpu.CompilerParams(
            dimension_semantics=("arbitrary",), vmem_limit_bytes=VMEM_LIMIT_RESIDENT_BYTES),
        name="moe_scatter",
    )(qs, hp)


def _expert_kernel(ea_ref, eb_ref, nu_ref, s_ref, wg_ref, wu_ref, wd_ref, y_ref):
    i = pl.program_id(0)
    first = i * EXPERT_STEP_TILES
    rec = EXPERT_ROW_TILE * ROW_TILES
    out = EXPERT_ROW_TILE * H_TILES

    @pl.when(first < nu_ref[0])
    def _():
        for j in range(EXPERT_STEP_TILES):
            t = first + j
            tiles = _from_tile_rows(s_ref[j * rec:(j + 1) * rec, :], ROW_TILES)
            hb = _unpack_pair(jnp.concatenate(tiles[:H_TILES], axis=-1)).astype(BF16)
            y = None
            for e_ref, idx in ((ea_ref, 0), (eb_ref, 1)):
                e = e_ref[t]
                wt = pltpu.unpack_elementwise(tiles[H_TILES], index=idx, packed_dtype=BF16,
                                              unpacked_dtype=F32)
                wt = wt[:, 0:1] + wt[:, 1:2]
                gt = _dot(hb, wg_ref[e])
                act = gt * _sigmoid(gt) * _dot(hb, wu_ref[e]) * wt
                part = _dot(act.astype(BF16), wd_ref[e])
                y = part if y is None else y + part
            y_ref[j * out:(j + 1) * out, :] = _to_tile_rows(_pack_pair(y))

    @pl.when(first >= nu_ref[0])
    def _():
        y_ref[...] = _pack_pair(jnp.zeros((y_ref.shape[0], 2 * LANES), F32))


def _experts(tile_ea, tile_eb, n_used, s, wg, wu, wd):
    rows = s.shape[0] // ROW_TILES
    tb = EXPERT_ROW_TILE * EXPERT_STEP_TILES
    grid_spec = pltpu.PrefetchScalarGridSpec(
        num_scalar_prefetch=3,
        grid=(rows // tb,),
        in_specs=[pl.BlockSpec((ROW_TILES * tb, LANES), lambda i, *_: (i, 0)),
                  _const_spec(wg.shape), _const_spec(wu.shape), _const_spec(wd.shape)],
        out_specs=pl.BlockSpec((H_TILES * tb, LANES), lambda i, *_: (i, 0)),
    )
    return pl.pallas_call(
        _expert_kernel,
        out_shape=jax.ShapeDtypeStruct((H_TILES * rows, LANES), jnp.uint32),
        grid_spec=grid_spec,
        compiler_params=pltpu.CompilerParams(
            dimension_semantics=("arbitrary",), vmem_limit_bytes=VMEM_LIMIT_BYTES),
        name="moe_experts",
    )(tile_ea, tile_eb, n_used, s, wg, wu, wd)


def _gather_kernel(qy_ref, x_ref, y_ref, gn_ref, o_ref, rows_ref, *, final_norm):
    i = pl.program_id(0)
    tn = x_ref.shape[0]
    group = SUBLANES * H_TILES

    def body(j, carry):
        dst = pl.multiple_of(j * group, SUBLANES)
        for k in range(SUBLANES):
            q = qy_ref[i * tn + j * SUBLANES + k]
            rows_ref[pl.ds(dst + k, H_TILES, stride=SUBLANES), :] = (
                y_ref[pl.ds(q, H_TILES, stride=SUBLANES), :])
        return carry

    lax.fori_loop(0, tn // SUBLANES, body, 0)
    packed = jnp.concatenate(_from_tile_rows(rows_ref[...], H_TILES), axis=-1)
    out = x_ref[...] + _unpack_pair(packed)
    if final_norm:
        out = _rmsnorm(out, gn_ref[...])
    o_ref[...] = out


def _gather(qy, x2, y, g_final, final_norm):
    n, d = x2.shape
    tn = ROW_MOVE_TILE
    grid_spec = pltpu.PrefetchScalarGridSpec(
        num_scalar_prefetch=1,
        grid=(n // tn,),
        in_specs=[pl.BlockSpec((tn, d), lambda i, qy: (i, 0)),
                  pl.BlockSpec(y.shape, lambda i, qy: (0, 0), pipeline_mode=pl.Buffered(1)),
                  pl.BlockSpec(g_final.shape, lambda i, qy: (0, 0), pipeline_mode=pl.Buffered(1))],
        out_specs=pl.BlockSpec((tn, d), lambda i, qy: (i, 0)),
        scratch_shapes=[pltpu.VMEM((H_TILES * tn, LANES), jnp.uint32)],
    )
    return pl.pallas_call(
        functools.partial(_gather_kernel, final_norm=final_norm),
        out_shape=jax.ShapeDtypeStruct(x2.shape, F32),
        grid_spec=grid_spec,
        compiler_params=pltpu.CompilerParams(
            dimension_semantics=("arbitrary",), vmem_limit_bytes=VMEM_LIMIT_RESIDENT_BYTES),
        name="moe_gather",
    )(qy, x2, y, g_final)


def _row(v):
    return v.reshape(1, -1)


def _router_params(w_rg, b_rg, w_re, b_re):
    pad = ROUTER_ROWS - MOE_GROUPS - N_EXPERTS
    wt = jnp.pad(jnp.concatenate([w_rg, w_re], axis=1).T, ((0, pad), (0, 0)))
    w_hi = wt.astype(BF16)
    w_lo = (wt - w_hi.astype(F32)).astype(BF16)
    b_col = jnp.pad(jnp.concatenate([b_rg, b_re]), (0, pad)).reshape(ROUTER_ROWS, 1)
    return jnp.concatenate([w_hi, w_lo], axis=0), b_col


def kernel(x, mem, norm_mix_g, w_in, conv_a_w, conv_a_b, ln_a_g, ln_a_b, w_a_out, w_pool_grp, pool_scale, conv_c_w, w_c_out, w_o, norm_x_g, norm_mem_g, w_xq, w_xkv, w_xo, norm_ffn_g, w_rg, b_rg, w_re, b_re, w_e_gate, w_e_up, w_e_down, norm_f_g):
    bsz, seq, d = x.shape
    n = bsz * seq
    depth = w_in.shape[0]
    for l in range(depth):
        caw = jnp.broadcast_to(conv_a_w[l][:, None, :], (CONV_A_K, SUBLANES, D_CONV))
        x = _mixer(x, _row(norm_mix_g[l]), w_in[l].astype(BF16), caw, _row(conv_a_b[l]),
                   _row(ln_a_g[l]), _row(ln_a_b[l]), w_a_out[l].astype(BF16),
                   w_pool_grp[l].astype(BF16), _row(pool_scale[l]), conv_c_w[l],
                   w_c_out[l].astype(BF16), w_o[l].astype(BF16))
        kt, v = _kv_proj(mem, _row(norm_mem_g[l]), w_xkv[l].astype(BF16))
        w_r, b_r = _router_params(w_rg[l], b_rg[l], w_re[l], b_re[l])
        x, hp, cls = _xattn(x, _row(norm_x_g[l]), w_xq[l].astype(BF16), kt, v,
                                  w_xo[l].astype(BF16), _row(norm_ffn_g[l]), w_r, b_r)
        qs, qy, tile_ea, tile_eb, n_used = _plan(cls.reshape(n // LANES, LANES))
        y = _experts(tile_ea, tile_eb, n_used, _scatter(qs, hp), w_e_gate[l].astype(BF16),
                     w_e_up[l].astype(BF16), w_e_down[l].astype(BF16))
        x = _gather(qy, x.reshape(n, d), y, _row(norm_f_g),
                    final_norm=(l == depth - 1)).reshape(bsz, seq, d)
    return x
```

```python
import functools

import jax
import jax.numpy as jnp
from jax import lax
from jax.experimental import pallas as pl
from jax.experimental.pallas import tpu as pltpu

D_MODEL = 1024
D_CONV = D_MODEL // 2
D_POOL = D_MODEL // 2
D_SC = D_MODEL // 2
CONV_A_K = 31
SC_K = 3
POOL_WINDOWS = (2, 4, 8, 16)
POOL_GROUP_DIM = D_POOL // len(POOL_WINDOWS)
POOL_OUT_DIM = D_MODEL // len(POOL_WINDOWS)
N_MEM = 256
XATTN_HEADS = 4
XATTN_HEAD_DIM = D_MODEL // XATTN_HEADS
MOE_GROUPS = 4
EXPERTS_PER_GROUP = 4
N_EXPERTS = MOE_GROUPS * EXPERTS_PER_GROUP
D_EXPERT = D_MODEL // 4
EPS = 1e-6

OFF_A = 0
OFF_POOL = 2 * D_CONV
OFF_C = OFF_POOL + D_POOL
OFF_G = OFF_C + 3 * D_SC
D_IN_PROJ = OFF_G + 3 * D_MODEL

SUBLANES = 8
LANES = 128
VMEM_LIMIT_BYTES = 56 * 1024 * 1024

HIST_A = 32
HIST_POOL = 16
HIST_C = 8

SEQ_TILE = 512
XATTN_SEQ_TILE = 1024
CONV_ROW_CHUNK = 16
ROUTER_LANES = LANES

PAIRS_PER_GROUP = EXPERTS_PER_GROUP * (EXPERTS_PER_GROUP - 1) // 2
N_CLASSES = MOE_GROUPS * PAIRS_PER_GROUP
EXPERT_ROW_TILE = 128
EXPERT_STEP_TILES = 8
ROUTER_ROWS = 32
ROW_MOVE_TILE = 512
VMEM_LIMIT_RESIDENT_BYTES = 58 * 1024 * 1024
H_TILES = D_MODEL // 2 // LANES
ROW_TILES = H_TILES + 1

BF16 = jnp.bfloat16
F32 = jnp.float32


def _sorted_rows(n_tokens):
    return n_tokens + N_CLASSES * EXPERT_ROW_TILE


def _sigmoid(v):
    return 0.5 * jnp.tanh(0.5 * v) + 0.5


def _rmsnorm(xf, g):
    return xf * lax.rsqrt(jnp.mean(xf * xf, axis=-1, keepdims=True) + EPS) * g


def _dot(a, b):
    return jnp.dot(a, b, preferred_element_type=F32)


def _const_spec(shape):
    nd = len(shape)
    return pl.BlockSpec(shape, lambda *_: (0,) * nd, pipeline_mode=pl.Buffered(1))


def _unpack_pair(q):
    return jnp.concatenate(
        [pltpu.unpack_elementwise(q, index=0, packed_dtype=BF16, unpacked_dtype=F32),
         pltpu.unpack_elementwise(q, index=1, packed_dtype=BF16, unpacked_dtype=F32)], axis=-1)


def _pack_pair(v):
    half = v.shape[-1] // 2
    return pltpu.pack_elementwise([v[:, :half], v[:, half:]], packed_dtype=BF16)


def _to_tile_rows(v):
    t, w = v.shape
    nj = w // LANES
    parts = [v[:, j * LANES:(j + 1) * LANES].reshape(t // SUBLANES, 1, SUBLANES, LANES)
             for j in range(nj)]
    return jnp.concatenate(parts, axis=1).reshape(nj * t, LANES)


def _from_tile_rows(v2, nj):
    t = v2.shape[0] // nj
    v4 = v2.reshape(t // SUBLANES, nj, SUBLANES, LANES)
    return [v4[:, j].reshape(t, LANES) for j in range(nj)]


def _tile_row_offset(r, nj):
    hi = jnp.floor(r / SUBLANES)
    return hi * (SUBLANES * nj) + (r - hi * SUBLANES)


def _mixer_kernel(x_ref, g_ref, w_in_ref, caw_ref, cab_ref, lng_ref, lnb_ref, w_a_out_ref,
                  w_pool_ref, pool_scale_ref, ccw_ref, w_c_out_ref, w_o_ref,
                  o_ref, abuf, ubuf, vbuf, actbuf, pbuf):
    ts = x_ref.shape[1]
    s = pl.program_id(1)

    @pl.when(s == 0)
    def _():
        abuf[0, 0:HIST_A, :] = jnp.zeros((HIST_A, D_CONV), F32)
        ubuf[0:HIST_POOL, :] = jnp.zeros((HIST_POOL, D_POOL), F32)
        vbuf[0:HIST_C, :] = jnp.zeros((HIST_C, D_SC), F32)

    x = x_ref[0]
    h = _rmsnorm(x, g_ref[...]).astype(BF16)

    def proj(off, width):
        return _dot(h, w_in_ref[:, off:off + width])

    pa = proj(OFF_A, 2 * D_CONV)
    abuf[0, HIST_A:HIST_A + ts, :] = pa[:, :D_CONV] * _sigmoid(pa[:, D_CONV:])
    shifted_rows = HIST_A + ts - SUBLANES
    for r in range(1, SUBLANES):
        abuf[r, 0:shifted_rows, :] = abuf[0, r:r + shifted_rows, :]
    chunk3 = (CONV_ROW_CHUNK // SUBLANES, SUBLANES, D_CONV)
    for c0 in range(0, ts, CONV_ROW_CHUNK):
        acc = jnp.broadcast_to(cab_ref[...], chunk3)
        for k in range(CONV_A_K):
            q, r = divmod(HIST_A - (CONV_A_K - 1) + k, SUBLANES)
            lo = c0 + q * SUBLANES
            acc = acc + caw_ref[k] * abuf[r, lo:lo + CONV_ROW_CHUNK, :].reshape(chunk3)
        acc = acc.reshape(CONV_ROW_CHUNK, D_CONV)
        mu = jnp.mean(acc, axis=-1, keepdims=True)
        cen = acc - mu
        var = jnp.mean(cen * cen, axis=-1, keepdims=True)
        ln = cen * lax.rsqrt(var + EPS) * lng_ref[...] + lnb_ref[...]
        actbuf[c0:c0 + CONV_ROW_CHUNK, :] = (ln * _sigmoid(ln)).astype(BF16)
    abuf[0, 0:HIST_A, :] = abuf[0, ts:ts + HIST_A, :]

    pbuf[...] = proj(OFF_POOL, D_IN_PROJ - OFF_POOL)

    def pcol(off, width):
        return pbuf[:, off - OFF_POOL:off - OFF_POOL + width]

    pu = pcol(OFF_POOL, D_POOL)
    ubuf[HIST_POOL:HIST_POOL + ts, :] = pu
    merged = _sigmoid(pcol(OFF_G, D_MODEL)) * _dot(actbuf[...], w_a_out_ref[...])

    t_glob = s * ts + lax.broadcasted_iota(jnp.int32, (ts, 1), 0)
    p_parts = []
    for i, w in enumerate(POOL_WINDOWS):
        c0, c1 = i * POOL_GROUP_DIM, (i + 1) * POOL_GROUP_DIM
        tok = pu[:, c0:c1]
        win = tok
        for j in range(1, w):
            win = win + ubuf[HIST_POOL - j:HIST_POOL - j + ts, c0:c1]
        cnt = jnp.minimum(t_glob + 1, w).astype(F32)
        pin = (win / cnt - tok).astype(BF16)
        p_parts.append(_dot(pin, w_pool_ref[i]))
    ubuf[0:HIST_POOL, :] = ubuf[ts:ts + HIST_POOL, :]
    merged = merged + _sigmoid(pcol(OFF_G + D_MODEL, D_MODEL)) * (jnp.concatenate(p_parts, axis=-1) * pool_scale_ref[...])

    c_b = pcol(OFF_C + D_SC, D_SC)
    v = pcol(OFF_C + 2 * D_SC, D_SC) * pcol(OFF_C, D_SC)
    vbuf[HIST_C:HIST_C + ts, :] = v
    conv_c = ccw_ref[SC_K - 1:SC_K, :] * v
    for k in range(SC_K - 1):
        off = HIST_C - (SC_K - 1) + k
        conv_c = conv_c + ccw_ref[k:k + 1, :] * vbuf[off:off + ts, :]
    vbuf[0:HIST_C, :] = vbuf[ts:ts + HIST_C, :]
    merged = merged + _sigmoid(pcol(OFF_G + 2 * D_MODEL, D_MODEL)) * _dot((c_b * conv_c).astype(BF16), w_c_out_ref[...])

    o_ref[0] = x + _dot(merged.astype(BF16), w_o_ref[...])


def _mixer(x, g, w_in, caw, cab, lng, lnb, w_a_out, w_pool, pool_scale, ccw, w_c_out, w_o):
    b, s, d = x.shape
    ts = SEQ_TILE
    consts = (g, w_in, caw, cab, lng, lnb, w_a_out, w_pool, pool_scale, ccw, w_c_out, w_o)
    return pl.pallas_call(
        _mixer_kernel,
        out_shape=jax.ShapeDtypeStruct(x.shape, F32),
        grid=(b, s // ts),
        in_specs=[pl.BlockSpec((1, ts, d), lambda i, j: (i, j, 0))]
        + [_const_spec(c.shape) for c in consts],
        out_specs=pl.BlockSpec((1, ts, d), lambda i, j: (i, j, 0)),
        scratch_shapes=[pltpu.VMEM((SUBLANES, HIST_A + ts, D_CONV), F32),
                        pltpu.VMEM((HIST_POOL + ts, D_POOL), F32),
                        pltpu.VMEM((HIST_C + ts, D_SC), F32),
                        pltpu.VMEM((ts, D_CONV), BF16),
                        pltpu.VMEM((ts, D_IN_PROJ - OFF_POOL), F32)],
        compiler_params=pltpu.CompilerParams(
            dimension_semantics=("arbitrary", "arbitrary"),
            vmem_limit_bytes=VMEM_LIMIT_BYTES),
        name="mixer",
    )(x, *consts)


def _kv_kernel(mem_ref, g_ref, w_kv_ref, kt_ref, v_ref):
    mn = _rmsnorm(mem_ref[0], g_ref[...]).astype(BF16)
    kv = _dot(mn, w_kv_ref[...])
    kt_ref[0] = kv[:, :D_MODEL].T.astype(BF16)
    v_ref[0] = kv[:, D_MODEL:].astype(BF16)


def _kv_proj(mem, g, w_kv):
    b, m, d = mem.shape
    return pl.pallas_call(
        _kv_kernel,
        out_shape=(jax.ShapeDtypeStruct((b, d, m), BF16), jax.ShapeDtypeStruct((b, m, d), BF16)),
        grid=(b,),
        in_specs=[pl.BlockSpec((1, m, d), lambda i: (i, 0, 0)),
                  _const_spec(g.shape), _const_spec(w_kv.shape)],
        out_specs=(pl.BlockSpec((1, d, m), lambda i: (i, 0, 0)),
                   pl.BlockSpec((1, m, d), lambda i: (i, 0, 0))),
        compiler_params=pltpu.CompilerParams(
            dimension_semantics=("arbitrary",), vmem_limit_bytes=VMEM_LIMIT_BYTES),
        name="kv_proj",
    )(mem, g, w_kv)


def _route_rows(lt):
    def first_argmax(vals, vmax):
        idx = jnp.full(vmax.shape, len(vals) - 1, jnp.int32)
        for k in range(len(vals) - 2, -1, -1):
            idx = jnp.where(vals[k] == vmax, k, idx)
        return idx

    g = [lt[k:k + 1, :] for k in range(MOE_GROUPS)]
    gmax = functools.reduce(jnp.maximum, g)
    g_idx = first_argmax(g, gmax)
    g_val = 1.0 / sum(jnp.exp(v - gmax) for v in g)
    e = []
    for j in range(EXPERTS_PER_GROUP):
        ej = lt[MOE_GROUPS + j:MOE_GROUPS + j + 1, :]
        for grp in range(1, MOE_GROUPS):
            r = MOE_GROUPS + grp * EXPERTS_PER_GROUP + j
            ej = jnp.where(g_idx == grp, lt[r:r + 1, :], ej)
        e.append(ej)
    m1 = functools.reduce(jnp.maximum, e)
    i1 = first_argmax(e, m1)
    e2 = [jnp.where(i1 == j, -jnp.inf, e[j]) for j in range(EXPERTS_PER_GROUP)]
    m2 = functools.reduce(jnp.maximum, e2)
    i2 = first_argmax(e2, m2)
    esum = sum(jnp.exp(v - m1) for v in e)
    p1 = 1.0 / esum
    p2 = jnp.exp(m2 - m1) / esum
    w1 = g_val * (p1 / (p1 + p2))
    w2 = g_val * (p2 / (p1 + p2))
    lo = jnp.minimum(i1, i2)
    hi = jnp.maximum(i1, i2)
    pair = ((lo * (2 * EXPERTS_PER_GROUP - 1 - lo)) >> 1) + (hi - lo - 1)
    cls = g_idx * PAIRS_PER_GROUP + pair
    first_is_low = i1 < i2
    return cls, jnp.where(first_is_low, w1, w2), jnp.where(first_is_low, w2, w1)


def _xattn_kernel(x_ref, g_ref, w_q_ref, kt_ref, v_ref, w_o_ref, gf_ref, w_r_ref, b_r_ref,
                  o_ref, hp_ref, cls_ref, xn_prev):
    @pl.when(pl.program_id(0) == 0)
    def _():
        xn_prev[...] = jnp.zeros(xn_prev.shape, F32)

    xp = xn_prev[...]
    x = x_ref[0]
    h = _rmsnorm(x, g_ref[...]).astype(BF16)
    q = (_dot(h, w_q_ref[...]) * (XATTN_HEAD_DIM ** -0.5)).astype(BF16)
    heads = []
    for hd in range(XATTN_HEADS):
        c0, c1 = hd * XATTN_HEAD_DIM, (hd + 1) * XATTN_HEAD_DIM
        sc = _dot(q[:, c0:c1], kt_ref[0, c0:c1, :])
        e = jnp.exp(sc - jnp.max(sc, axis=-1, keepdims=True))
        denom = jnp.sum(e, axis=-1, keepdims=True)
        heads.append(_dot(e.astype(BF16), v_ref[0, :, c0:c1]) / denom)
    o = jnp.concatenate(heads, axis=-1).astype(BF16)
    xn = x + _dot(o, w_o_ref[...])
    o_ref[0] = xn
    xn_prev[...] = xn

    hf = _rmsnorm(xp, gf_ref[...])
    h_hi = hf.astype(BF16)
    h_lo = (hf - h_hi.astype(F32)).astype(BF16)
    nt_dims = (((1,), (1,)), ((), ()))
    part = lax.dot_general(w_r_ref[...], h_hi, nt_dims, preferred_element_type=F32)
    lt = (part[:ROUTER_ROWS] + part[ROUTER_ROWS:]
          + lax.dot_general(w_r_ref[:ROUTER_ROWS, :], h_lo, nt_dims, preferred_element_type=F32)
          + b_r_ref[...])
    cls, wa, wb = _route_rows(lt)
    ts = cls.shape[1]
    for c in range(ts // LANES):
        cls_ref[0, c:c + 1, :] = cls[:, c * LANES:(c + 1) * LANES]
    row = lax.broadcasted_iota(jnp.int32, (LANES, ts), 0)

    def weight_lanes(w):
        w_hi = w.astype(BF16).astype(F32)
        return jnp.where(row == 0, w_hi, jnp.where(row == 1, w - w_hi, 0.0)).T

    wt_tile = pltpu.pack_elementwise([weight_lanes(wa), weight_lanes(wb)], packed_dtype=BF16)
    hp_ref[...] = _to_tile_rows(jnp.concatenate([_pack_pair(hf), wt_tile], axis=-1))


def _xattn(x, g, w_q, kt, v, w_o, g_ffn, w_r, b_r):
    b, s, d = x.shape
    ts = XATTN_SEQ_TILE
    nt = s // ts
    n = b * s
    tiles = b * nt

    def attn_tile(step):
        return jnp.minimum(step, tiles - 1)

    def route_tile(step):
        return jnp.maximum(step - 1, 0)

    return pl.pallas_call(
        _xattn_kernel,
        out_shape=(jax.ShapeDtypeStruct(x.shape, F32),
                   jax.ShapeDtypeStruct((ROW_TILES * n, LANES), jnp.uint32),
                   jax.ShapeDtypeStruct((n // ts, ts // LANES, LANES), jnp.int32)),
        grid=(tiles + 1,),
        in_specs=[pl.BlockSpec((1, ts, d), lambda i: (attn_tile(i) // nt, attn_tile(i) % nt, 0)),
                  _const_spec(g.shape), _const_spec(w_q.shape),
                  pl.BlockSpec((1, d, N_MEM), lambda i: (attn_tile(i) // nt, 0, 0)),
                  pl.BlockSpec((1, N_MEM, d), lambda i: (attn_tile(i) // nt, 0, 0)),
                  _const_spec(w_o.shape), _const_spec(g_ffn.shape),
                  _const_spec(w_r.shape), _const_spec(b_r.shape)],
        out_specs=(pl.BlockSpec((1, ts, d), lambda i: (attn_tile(i) // nt, attn_tile(i) % nt, 0)),
                   pl.BlockSpec((ROW_TILES * ts, LANES), lambda i: (route_tile(i), 0)),
                   pl.BlockSpec((1, ts // LANES, LANES), lambda i: (route_tile(i), 0, 0))),
        scratch_shapes=[pltpu.VMEM((ts, d), F32)],
        compiler_params=pltpu.CompilerParams(
            dimension_semantics=("arbitrary",), vmem_limit_bytes=VMEM_LIMIT_BYTES),
        name="xattn",
    )(x, g, w_q, kt, v, w_o, g_ffn, w_r, b_r)


def _plan_kernel(cls_ref, qs_ref, qy_ref, ea_ref, eb_ref, nu_ref):
    cls = cls_ref[...]
    nr, nl = cls.shape
    r = lax.broadcasted_iota(jnp.int32, (nl, nl), 0)
    c = lax.broadcasted_iota(jnp.int32, (nl, nl), 1)
    lanes_before = (r < c).astype(BF16)
    rr = lax.broadcasted_iota(jnp.int32, (nr, nr), 0)
    rc = lax.broadcasted_iota(jnp.int32, (nr, nr), 1)
    rows_before = (rc < rr).astype(BF16)
    ones_l = jnp.ones((nl, nl), BF16)
    ones_r = jnp.ones((nr, nr), BF16)
    n_tab = ea_ref.shape[1]
    tile_i = lax.broadcasted_iota(jnp.int32, (SUBLANES, n_tab), 1).astype(F32)
    run = jnp.zeros((nr, nl), F32)
    pos = jnp.zeros((nr, nl), F32)
    tile_cls = jnp.zeros((SUBLANES, n_tab), F32)
    for k in range(N_CLASSES):
        m = cls == k
        mb = jnp.where(m, 1.0, 0.0).astype(BF16)
        rowtot = _dot(mb, ones_l).astype(BF16)
        rank = _dot(mb, lanes_before) + _dot(rows_before, rowtot)
        total = _dot(ones_r, rowtot)
        pos = jnp.where(m, run * EXPERT_ROW_TILE + rank, pos)
        run = run + jnp.floor((total + (EXPERT_ROW_TILE - 1)) / EXPERT_ROW_TILE)
        end = jnp.concatenate([run[:SUBLANES, :]] * (n_tab // nl), axis=1)
        tile_cls = tile_cls + jnp.where(end <= tile_i, 1.0, 0.0)
    qs_ref[...] = _tile_row_offset(pos, ROW_TILES).astype(jnp.int32)
    qy_ref[...] = _tile_row_offset(pos, H_TILES).astype(jnp.int32)
    tc = jnp.minimum(tile_cls, N_CLASSES - 1.0).astype(jnp.int32)
    grp = sum(jnp.where(tc >= g * PAIRS_PER_GROUP, 1, 0) for g in range(1, MOE_GROUPS))
    pair = tc - grp * PAIRS_PER_GROUP
    pair_lo = jnp.where(pair >= 3, 1, 0) + jnp.where(pair >= 5, 1, 0)
    pair_hi = pair + 1 - jnp.where(pair >= 5, 3, jnp.where(pair >= 3, 2, 0))
    ea_ref[...] = grp * EXPERTS_PER_GROUP + pair_lo
    eb_ref[...] = grp * EXPERTS_PER_GROUP + pair_hi
    nu_ref[...] = run[:SUBLANES, :].astype(jnp.int32)


def _plan(cls2):
    nr, nl = cls2.shape
    n_tiles = _sorted_rows(nr * nl) // EXPERT_ROW_TILE
    n_tab = -(-n_tiles // nl) * nl
    vmem = pl.BlockSpec(memory_space=pltpu.VMEM)
    qs, qy, ea, eb, nu = pl.pallas_call(
        _plan_kernel,
        out_shape=(jax.ShapeDtypeStruct((nr, nl), jnp.int32),
                   jax.ShapeDtypeStruct((nr, nl), jnp.int32),
                   jax.ShapeDtypeStruct((SUBLANES, n_tab), jnp.int32),
                   jax.ShapeDtypeStruct((SUBLANES, n_tab), jnp.int32),
                   jax.ShapeDtypeStruct((SUBLANES, nl), jnp.int32)),
        in_specs=[vmem],
        out_specs=(vmem, vmem, vmem, vmem, vmem),
        name="moe_plan",
    )(cls2)
    return (qs.reshape(nr * nl), qy.reshape(nr * nl),
            ea[0, :n_tiles], eb[0, :n_tiles], nu[0, :1])


def _scatter_kernel(qs_ref, hp_ref, s_ref):
    i = pl.program_id(0)
    tn = hp_ref.shape[0] // ROW_TILES
    group = SUBLANES * ROW_TILES

    @pl.when(i == 0)
    def _():
        zero_rows = _pack_pair(jnp.zeros((hp_ref.shape[0], 2 * LANES), F32))

        def fill(c, carry):
            start = pl.multiple_of(c * hp_ref.shape[0], SUBLANES)
            s_ref[pl.ds(start, hp_ref.shape[0]), :] = zero_rows
            return carry

        lax.fori_loop(0, s_ref.shape[0] // hp_ref.shape[0], fill, 0)

    def body(j, carry):
        src = pl.multiple_of(j * group, SUBLANES)
        for k in range(SUBLANES):
            q = qs_ref[i * tn + j * SUBLANES + k]
            s_ref[pl.ds(q, ROW_TILES, stride=SUBLANES), :] = (
                hp_ref[pl.ds(src + k, ROW_TILES, stride=SUBLANES), :])
        return carry

    lax.fori_loop(0, tn // SUBLANES, body, 0)


def _scatter(qs, hp):
    n = qs.shape[0]
    tn = ROW_MOVE_TILE
    rows = ROW_TILES * _sorted_rows(n)
    grid_spec = pltpu.PrefetchScalarGridSpec(
        num_scalar_prefetch=1,
        grid=(n // tn,),
        in_specs=[pl.BlockSpec((ROW_TILES * tn, LANES), lambda i, qs: (i, 0))],
        out_specs=pl.BlockSpec((rows, LANES), lambda i, qs: (0, 0), pipeline_mode=pl.Buffered(1)),
    )
    return pl.pallas_call(
        _scatter_kernel,
        out_shape=jax.ShapeDtypeStruct((rows, LANES), jnp.uint32),
        grid_spec=grid_spec,
        compiler_params=pltpu.CompilerParams(
            dimension_semantics=("arbitrary",), vmem_limit_bytes=VMEM_LIMIT_RESIDENT_BYTES),
        name="moe_scatter",
    )(qs, hp)


def _expert_kernel(ea_ref, eb_ref, nu_ref, s_ref, wg_ref, wu_ref, wd_ref, y_ref):
    i = pl.program_id(0)
    first = i * EXPERT_STEP_TILES
    rec = EXPERT_ROW_TILE * ROW_TILES
    out = EXPERT_ROW_TILE * H_TILES

    @pl.when(first < nu_ref[0])
    def _():
        for j in range(EXPERT_STEP_TILES):
            t = first + j
            tiles = _from_tile_rows(s_ref[j * rec:(j + 1) * rec, :], ROW_TILES)
            hb = _unpack_pair(jnp.concatenate(tiles[:H_TILES], axis=-1)).astype(BF16)
            y = None
            for e_ref, idx in ((ea_ref, 0), (eb_ref, 1)):
                e = e_ref[t]
                wt = pltpu.unpack_elementwise(tiles[H_TILES], index=idx, packed_dtype=BF16,
                                              unpacked_dtype=F32)
                wt = wt[:, 0:1] + wt[:, 1:2]
                gt = _dot(hb, wg_ref[e])
                act = gt * _sigmoid(gt) * _dot(hb, wu_ref[e]) * wt
                part = _dot(act.astype(BF16), wd_ref[e])
                y = part if y is None else y + part
            y_ref[j * out:(j + 1) * out, :] = _to_tile_rows(_pack_pair(y))

    @pl.when(first >= nu_ref[0])
    def _():
        y_ref[...] = _pack_pair(jnp.zeros((y_ref.shape[0], 2 * LANES), F32))


def _experts(tile_ea, tile_eb, n_used, s, wg, wu, wd):
    rows = s.shape[0] // ROW_TILES
    tb = EXPERT_ROW_TILE * EXPERT_STEP_TILES
    grid_spec = pltpu.PrefetchScalarGridSpec(
        num_scalar_prefetch=3,
        grid=(rows // tb,),
        in_specs=[pl.BlockSpec((ROW_TILES * tb, LANES), lambda i, *_: (i, 0)),
                  _const_spec(wg.shape), _const_spec(wu.shape), _const_spec(wd.shape)],
        out_specs=pl.BlockSpec((H_TILES * tb, LANES), lambda i, *_: (i, 0)),
    )
    return pl.pallas_call(
        _expert_kernel,
        out_shape=jax.ShapeDtypeStruct((H_TILES * rows, LANES), jnp.uint32),
        grid_spec=grid_spec,
        compiler_params=pltpu.CompilerParams(
            dimension_semantics=("arbitrary",), vmem_limit_bytes=VMEM_LIMIT_BYTES),
        name="moe_experts",
    )(tile_ea, tile_eb, n_used, s, wg, wu, wd)


def _gather_kernel(qy_ref, x_ref, y_ref, gn_ref, o_ref, rows_ref, *, final_norm):
    i = pl.program_id(0)
    tn = x_ref.shape[0]
    group = SUBLANES * H_TILES

    def body(j, carry):
        dst = pl.multiple_of(j * group, SUBLANES)
        for k in range(SUBLANES):
            q = qy_ref[i * tn + j * SUBLANES + k]
            rows_ref[pl.ds(dst + k, H_TILES, stride=SUBLANES), :] = (
                y_ref[pl.ds(q, H_TILES, stride=SUBLANES), :])
        return carry

    lax.fori_loop(0, tn // SUBLANES, body, 0)
    packed = jnp.concatenate(_from_tile_rows(rows_ref[...], H_TILES), axis=-1)
    out = x_ref[...] + _unpack_pair(packed)
    if final_norm:
        out = _rmsnorm(out, gn_ref[...])
    o_ref[...] = out


def _gather(qy, x2, y, g_final, final_norm):
    n, d = x2.shape
    tn = ROW_MOVE_TILE
    grid_spec = pltpu.PrefetchScalarGridSpec(
        num_scalar_prefetch=1,
        grid=(n // tn,),
        in_specs=[pl.BlockSpec((tn, d), lambda i, qy: (i, 0)),
                  pl.BlockSpec(y.shape, lambda i, qy: (0, 0), pipeline_mode=pl.Buffered(1)),
                  pl.BlockSpec(g_final.shape, lambda i, qy: (0, 0), pipeline_mode=pl.Buffered(1))],
        out_specs=pl.BlockSpec((tn, d), lambda i, qy: (i, 0)),
        scratch_shapes=[pltpu.VMEM((H_TILES * tn, LANES), jnp.uint32)],
    )
    return pl.pallas_call(
        functools.partial(_gather_kernel, final_norm=final_norm),
        out_shape=jax.ShapeDtypeStruct(x2.shape, F32),
        grid_spec=grid_spec,
        compiler_params=pltpu.CompilerParams(
            dimension_semantics=("arbitrary",), vmem_limit_bytes=VMEM_LIMIT_RESIDENT_BYTES),
        name="moe_gather",
    )(qy, x2, y, g_final)


def _row(v):
    return v.reshape(1, -1)


def _router_params(w_rg, b_rg, w_re, b_re):
    pad = ROUTER_ROWS - MOE_GROUPS - N_EXPERTS
    wt = jnp.pad(jnp.concatenate([w_rg, w_re], axis=1).T, ((0, pad), (0, 0)))
    w_hi = wt.astype(BF16)
    w_lo = (wt - w_hi.astype(F32)).astype(BF16)
    b_col = jnp.pad(jnp.concatenate([b_rg, b_re]), (0, pad)).reshape(ROUTER_ROWS, 1)
    return jnp.concatenate([w_hi, w_lo], axis=0), b_col


def kernel(x, mem, norm_mix_g, w_in, conv_a_w, conv_a_b, ln_a_g, ln_a_b, w_a_out, w_pool_grp, pool_scale, conv_c_w, w_c_out, w_o, norm_x_g, norm_mem_g, w_xq, w_xkv, w_xo, norm_ffn_g, w_rg, b_rg, w_re, b_re, w_e_gate, w_e_up, w_e_down, norm_f_g):
    bsz, seq, d = x.shape
    n = bsz * seq
    depth = w_in.shape[0]
    for l in range(depth):
        caw = jnp.broadcast_to(conv_a_w[l][:, None, :], (CONV_A_K, SUBLANES, D_CONV))
        x = _mixer(x, _row(norm_mix_g[l]), w_in[l].astype(BF16), caw, _row(conv_a_b[l]),
                   _row(ln_a_g[l]), _row(ln_a_b[l]), w_a_out[l].astype(BF16),
                   w_pool_grp[l].astype(BF16), _row(pool_scale[l]), conv_c_w[l],
                   w_c_out[l].astype(BF16), w_o[l].astype(BF16))
        kt, v = _kv_proj(mem, _row(norm_mem_g[l]), w_xkv[l].astype(BF16))
        w_r, b_r = _router_params(w_rg[l], b_rg[l], w_re[l], b_re[l])
        x, hp, cls = _xattn(x, _row(norm_x_g[l]), w_xq[l].astype(BF16), kt, v,
                                  w_xo[l].astype(BF16), _row(norm_ffn_g[l]), w_r, b_r)
        qs, qy, tile_ea, tile_eb, n_used = _plan(cls.reshape(n // LANES, LANES))
        y = _experts(tile_ea, tile_eb, n_used, _scatter(qs, hp), w_e_gate[l].astype(BF16),
                     w_e_up[l].astype(BF16), w_e_down[l].astype(BF16))
        x = _gather(qy, x.reshape(n, d), y, _row(norm_f_g),
                    final_norm=(l == depth - 1)).reshape(bsz, seq, d)
    return x
```

```python
import functools

import jax
import jax.numpy as jnp
from jax import lax
from jax.experimental import pallas as pl
from jax.experimental.pallas import tpu as pltpu

D_MODEL = 1024
D_CONV = D_MODEL // 2
D_POOL = D_MODEL // 2
D_SC = D_MODEL // 2
CONV_A_K = 31
SC_K = 3
POOL_WINDOWS = (2, 4, 8, 16)
POOL_GROUP_DIM = D_POOL // len(POOL_WINDOWS)
POOL_OUT_DIM = D_MODEL // len(POOL_WINDOWS)
N_MEM = 256
XATTN_HEADS = 4
XATTN_HEAD_DIM = D_MODEL // XATTN_HEADS
MOE_GROUPS = 4
EXPERTS_PER_GROUP = 4
N_EXPERTS = MOE_GROUPS * EXPERTS_PER_GROUP
D_EXPERT = D_MODEL // 4
EPS = 1e-6

OFF_A = 0
OFF_POOL = 2 * D_CONV
OFF_C = OFF_POOL + D_POOL
OFF_G = OFF_C + 3 * D_SC
D_IN_PROJ = OFF_G + 3 * D_MODEL

SUBLANES = 8
LANES = 128
VMEM_LIMIT_BYTES = 56 * 1024 * 1024

HIST_A = 32
HIST_POOL = 16
HIST_C = 8

SEQ_TILE = 512
XATTN_SEQ_TILE = 1024
CONV_ROW_CHUNK = 16
ROUTER_LANES = LANES

PAIRS_PER_GROUP = EXPERTS_PER_GROUP * (EXPERTS_PER_GROUP - 1) // 2
N_CLASSES = MOE_GROUPS * PAIRS_PER_GROUP
EXPERT_ROW_TILE = 128
EXPERT_STEP_TILES = 8
ROUTER_ROWS = 32
ROW_MOVE_TILE = 512
VMEM_LIMIT_RESIDENT_BYTES = 58 * 1024 * 1024
H_TILES = D_MODEL // 2 // LANES
ROW_TILES = H_TILES + 1

BF16 = jnp.bfloat16
F32 = jnp.float32


def _sorted_rows(n_tokens):
    return n_tokens + N_CLASSES * EXPERT_ROW_TILE


def _sigmoid(v):
    return 0.5 * jnp.tanh(0.5 * v) + 0.5


def _rmsnorm(xf, g):
    return xf * lax.rsqrt(jnp.mean(xf * xf, axis=-1, keepdims=True) + EPS) * g


def _dot(a, b):
    return jnp.dot(a, b, preferred_element_type=F32)


def _const_spec(shape):
    nd = len(shape)
    return pl.BlockSpec(shape, lambda *_: (0,) * nd, pipeline_mode=pl.Buffered(1))


def _unpack_pair(q):
    return jnp.concatenate(
        [pltpu.unpack_elementwise(q, index=0, packed_dtype=BF16, unpacked_dtype=F32),
         pltpu.unpack_elementwise(q, index=1, packed_dtype=BF16, unpacked_dtype=F32)], axis=-1)


def _pack_pair(v):
    half = v.shape[-1] // 2
    return pltpu.pack_elementwise([v[:, :half], v[:, half:]], packed_dtype=BF16)


def _to_tile_rows(v):
    t, w = v.shape
    nj = w // LANES
    parts = [v[:, j * LANES:(j + 1) * LANES].reshape(t // SUBLANES, 1, SUBLANES, LANES)
             for j in range(nj)]
    return jnp.concatenate(parts, axis=1).reshape(nj * t, LANES)


def _from_tile_rows(v2, nj):
    t = v2.shape[0] // nj
    v4 = v2.reshape(t // SUBLANES, nj, SUBLANES, LANES)
    return [v4[:, j].reshape(t, LANES) for j in range(nj)]


def _tile_row_offset(r, nj):
    hi = jnp.floor(r / SUBLANES)
    return hi * (SUBLANES * nj) + (r - hi * SUBLANES)


def _mixer_kernel(x_ref, *refs, has_moe_delta):
    y_ref, refs = (refs[0], refs[1:]) if has_moe_delta else (None, refs)
    (g_ref, w_in_ref, caw_ref, cab_ref, lng_ref, lnb_ref, w_a_out_ref, w_pool_ref,
     pool_scale_ref, ccw_ref, w_c_out_ref, w_o_ref, o_ref, abuf, ubuf, vbuf, actbuf, pbuf) = refs
    ts = x_ref.shape[1]
    s = pl.program_id(1)

    @pl.when(s == 0)
    def _():
        abuf[0, 0:HIST_A, :] = jnp.zeros((HIST_A, D_CONV), F32)
        ubuf[0:HIST_POOL, :] = jnp.zeros((HIST_POOL, D_POOL), F32)
        vbuf[0:HIST_C, :] = jnp.zeros((HIST_C, D_SC), F32)

    x = x_ref[0]
    if y_ref is not None:
        x = x + _unpack_pair(jnp.concatenate(_from_tile_rows(y_ref[...], H_TILES), axis=-1))
    h = _rmsnorm(x, g_ref[...]).astype(BF16)

    def proj(off, width):
        return _dot(h, w_in_ref[:, off:off + width])

    pa = proj(OFF_A, 2 * D_CONV)
    abuf[0, HIST_A:HIST_A + ts, :] = pa[:, :D_CONV] * _sigmoid(pa[:, D_CONV:])
    shifted_rows = HIST_A + ts - SUBLANES
    for r in range(1, SUBLANES):
        abuf[r, 0:shifted_rows, :] = abuf[0, r:r + shifted_rows, :]
    chunk3 = (CONV_ROW_CHUNK // SUBLANES, SUBLANES, D_CONV)
    for c0 in range(0, ts, CONV_ROW_CHUNK):
        acc = jnp.broadcast_to(cab_ref[...], chunk3)
        for k in range(CONV_A_K):
            q, r = divmod(HIST_A - (CONV_A_K - 1) + k, SUBLANES)
            lo = c0 + q * SUBLANES
            acc = acc + caw_ref[k] * abuf[r, lo:lo + CONV_ROW_CHUNK, :].reshape(chunk3)
        acc = acc.reshape(CONV_ROW_CHUNK, D_CONV)
        mu = jnp.mean(acc, axis=-1, keepdims=True)
        cen = acc - mu
        var = jnp.mean(cen * cen, axis=-1, keepdims=True)
        ln = cen * lax.rsqrt(var + EPS) * lng_ref[...] + lnb_ref[...]
        actbuf[c0:c0 + CONV_ROW_CHUNK, :] = (ln * _sigmoid(ln)).astype(BF16)
    abuf[0, 0:HIST_A, :] = abuf[0, ts:ts + HIST_A, :]

    pbuf[...] = proj(OFF_POOL, D_IN_PROJ - OFF_POOL)

    def pcol(off, width):
        return pbuf[:, off - OFF_POOL:off - OFF_POOL + width]

    pu = pcol(OFF_POOL, D_POOL)
    ubuf[HIST_POOL:HIST_POOL + ts, :] = pu
    merged = _sigmoid(pcol(OFF_G, D_MODEL)) * _dot(actbuf[...], w_a_out_ref[...])

    t_glob = s * ts + lax.broadcasted_iota(jnp.int32, (ts, 1), 0)
    p_parts = []
    for i, w in enumerate(POOL_WINDOWS):
        c0, c1 = i * POOL_GROUP_DIM, (i + 1) * POOL_GROUP_DIM
        tok = pu[:, c0:c1]
        win = tok
        for j in range(1, w):
            win = win + ubuf[HIST_POOL - j:HIST_POOL - j + ts, c0:c1]
        cnt = jnp.minimum(t_glob + 1, w).astype(F32)
        pin = (win / cnt - tok).astype(BF16)
        p_parts.append(_dot(pin, w_pool_ref[i]))
    ubuf[0:HIST_POOL, :] = ubuf[ts:ts + HIST_POOL, :]
    merged = merged + _sigmoid(pcol(OFF_G + D_MODEL, D_MODEL)) * (jnp.concatenate(p_parts, axis=-1) * pool_scale_ref[...])

    c_b = pcol(OFF_C + D_SC, D_SC)
    v = pcol(OFF_C + 2 * D_SC, D_SC) * pcol(OFF_C, D_SC)
    vbuf[HIST_C:HIST_C + ts, :] = v
    conv_c = ccw_ref[SC_K - 1:SC_K, :] * v
    for k in range(SC_K - 1):
        off = HIST_C - (SC_K - 1) + k
        conv_c = conv_c + ccw_ref[k:k + 1, :] * vbuf[off:off + ts, :]
    vbuf[0:HIST_C, :] = vbuf[ts:ts + HIST_C, :]
    merged = merged + _sigmoid(pcol(OFF_G + 2 * D_MODEL, D_MODEL)) * _dot((c_b * conv_c).astype(BF16), w_c_out_ref[...])

    o_ref[0] = x + _dot(merged.astype(BF16), w_o_ref[...])


def _mixer(x, moe_delta, g, w_in, caw, cab, lng, lnb, w_a_out, w_pool, pool_scale, ccw, w_c_out,
           w_o):
    b, s, d = x.shape
    ts = SEQ_TILE
    nt = s // ts
    consts = (g, w_in, caw, cab, lng, lnb, w_a_out, w_pool, pool_scale, ccw, w_c_out, w_o)
    delta = () if moe_delta is None else (moe_delta,)
    delta_specs = [pl.BlockSpec((H_TILES * ts, LANES), lambda i, j: (i * nt + j, 0))] * len(delta)
    return pl.pallas_call(
        functools.partial(_mixer_kernel, has_moe_delta=moe_delta is not None),
        out_shape=jax.ShapeDtypeStruct(x.shape, F32),
        grid=(b, nt),
        in_specs=[pl.BlockSpec((1, ts, d), lambda i, j: (i, j, 0))] + delta_specs
        + [_const_spec(c.shape) for c in consts],
        out_specs=pl.BlockSpec((1, ts, d), lambda i, j: (i, j, 0)),
        scratch_shapes=[pltpu.VMEM((SUBLANES, HIST_A + ts, D_CONV), F32),
                        pltpu.VMEM((HIST_POOL + ts, D_POOL), F32),
                        pltpu.VMEM((HIST_C + ts, D_SC), F32),
                        pltpu.VMEM((ts, D_CONV), BF16),
                        pltpu.VMEM((ts, D_IN_PROJ - OFF_POOL), F32)],
        compiler_params=pltpu.CompilerParams(
            dimension_semantics=("arbitrary", "arbitrary"),
            vmem_limit_bytes=VMEM_LIMIT_BYTES),
        name="mixer",
    )(x, *delta, *consts)


def _kv_kernel(mem_ref, g_ref, w_kv_ref, kt_ref, v_ref):
    mn = _rmsnorm(mem_ref[0], g_ref[...]).astype(BF16)
    kv = _dot(mn, w_kv_ref[...])
    kt_ref[0] = kv[:, :D_MODEL].T.astype(BF16)
    v_ref[0] = kv[:, D_MODEL:].astype(BF16)


def _kv_proj(mem, g, w_kv):
    b, m, d = mem.shape
    return pl.pallas_call(
        _kv_kernel,
        out_shape=(jax.ShapeDtypeStruct((b, d, m), BF16), jax.ShapeDtypeStruct((b, m, d), BF16)),
        grid=(b,),
        in_specs=[pl.BlockSpec((1, m, d), lambda i: (i, 0, 0)),
                  _const_spec(g.shape), _const_spec(w_kv.shape)],
        out_specs=(pl.BlockSpec((1, d, m), lambda i: (i, 0, 0)),
                   pl.BlockSpec((1, m, d), lambda i: (i, 0, 0))),
        compiler_params=pltpu.CompilerParams(
            dimension_semantics=("arbitrary",), vmem_limit_bytes=VMEM_LIMIT_BYTES),
        name="kv_proj",
    )(mem, g, w_kv)


def _route_rows(lt):
    def first_argmax(vals, vmax):
        idx = jnp.full(vmax.shape, len(vals) - 1, jnp.int32)
        for k in range(len(vals) - 2, -1, -1):
            idx = jnp.where(vals[k] == vmax, k, idx)
        return idx

    g = [lt[k:k + 1, :] for k in range(MOE_GROUPS)]
    gmax = functools.reduce(jnp.maximum, g)
    g_idx = first_argmax(g, gmax)
    g_val = 1.0 / sum(jnp.exp(v - gmax) for v in g)
    e = []
    for j in range(EXPERTS_PER_GROUP):
        ej = lt[MOE_GROUPS + j:MOE_GROUPS + j + 1, :]
        for grp in range(1, MOE_GROUPS):
            r = MOE_GROUPS + grp * EXPERTS_PER_GROUP + j
            ej = jnp.where(g_idx == grp, lt[r:r + 1, :], ej)
        e.append(ej)
    m1 = functools.reduce(jnp.maximum, e)
    i1 = first_argmax(e, m1)
    e2 = [jnp.where(i1 == j, -jnp.inf, e[j]) for j in range(EXPERTS_PER_GROUP)]
    m2 = functools.reduce(jnp.maximum, e2)
    i2 = first_argmax(e2, m2)
    esum = sum(jnp.exp(v - m1) for v in e)
    p1 = 1.0 / esum
    p2 = jnp.exp(m2 - m1) / esum
    w1 = g_val * (p1 / (p1 + p2))
    w2 = g_val * (p2 / (p1 + p2))
    lo = jnp.minimum(i1, i2)
    hi = jnp.maximum(i1, i2)
    pair = ((lo * (2 * EXPERTS_PER_GROUP - 1 - lo)) >> 1) + (hi - lo - 1)
    cls = g_idx * PAIRS_PER_GROUP + pair
    first_is_low = i1 < i2
    return cls, jnp.where(first_is_low, w1, w2), jnp.where(first_is_low, w2, w1)


def _xattn_kernel(x_ref, g_ref, w_q_ref, kt_ref, v_ref, w_o_ref, gf_ref, w_r_ref, b_r_ref,
                  o_ref, hp_ref, cls_ref, xn_prev):
    @pl.when(pl.program_id(0) == 0)
    def _():
        xn_prev[...] = jnp.zeros(xn_prev.shape, F32)

    xp = xn_prev[...]
    x = x_ref[0]
    h = _rmsnorm(x, g_ref[...]).astype(BF16)
    q = (_dot(h, w_q_ref[...]) * (XATTN_HEAD_DIM ** -0.5)).astype(BF16)
    heads = []
    for hd in range(XATTN_HEADS):
        c0, c1 = hd * XATTN_HEAD_DIM, (hd + 1) * XATTN_HEAD_DIM
        sc = _dot(q[:, c0:c1], kt_ref[0, c0:c1, :])
        e = jnp.exp(sc - jnp.max(sc, axis=-1, keepdims=True))
        denom = jnp.sum(e, axis=-1, keepdims=True)
        heads.append(_dot(e.astype(BF16), v_ref[0, :, c0:c1]) / denom)
    o = jnp.concatenate(heads, axis=-1).astype(BF16)
    xn = x + _dot(o, w_o_ref[...])
    o_ref[0] = xn
    xn_prev[...] = xn

    hf = _rmsnorm(xp, gf_ref[...])
    h_hi = hf.astype(BF16)
    h_lo = (hf - h_hi.astype(F32)).astype(BF16)
    nt_dims = (((1,), (1,)), ((), ()))
    part = lax.dot_general(w_r_ref[...], h_hi, nt_dims, preferred_element_type=F32)
    lt = (part[:ROUTER_ROWS] + part[ROUTER_ROWS:]
          + lax.dot_general(w_r_ref[:ROUTER_ROWS, :], h_lo, nt_dims, preferred_element_type=F32)
          + b_r_ref[...])
    cls, wa, wb = _route_rows(lt)
    ts = cls.shape[1]
    for c in range(ts // LANES):
        cls_ref[0, c:c + 1, :] = cls[:, c * LANES:(c + 1) * LANES]
    row = lax.broadcasted_iota(jnp.int32, (LANES, ts), 0)

    def weight_lanes(w):
        w_hi = w.astype(BF16).astype(F32)
        return jnp.where(row == 0, w_hi, jnp.where(row == 1, w - w_hi, 0.0)).T

    wt_tile = pltpu.pack_elementwise([weight_lanes(wa), weight_lanes(wb)], packed_dtype=BF16)
    hp_ref[...] = _to_tile_rows(jnp.concatenate([_pack_pair(hf), wt_tile], axis=-1))


def _xattn(x, g, w_q, kt, v, w_o, g_ffn, w_r, b_r):
    b, s, d = x.shape
    ts = XATTN_SEQ_TILE
    nt = s // ts
    n = b * s
    tiles = b * nt

    def attn_tile(step):
        return jnp.minimum(step, tiles - 1)

    def route_tile(step):
        return jnp.maximum(step - 1, 0)

    return pl.pallas_call(
        _xattn_kernel,
        out_shape=(jax.ShapeDtypeStruct(x.shape, F32),
                   jax.ShapeDtypeStruct((ROW_TILES * n, LANES), jnp.uint32),
                   jax.ShapeDtypeStruct((n // ts, ts // LANES, LANES), jnp.int32)),
        grid=(tiles + 1,),
        in_specs=[pl.BlockSpec((1, ts, d), lambda i: (attn_tile(i) // nt, attn_tile(i) % nt, 0)),
                  _const_spec(g.shape), _const_spec(w_q.shape),
                  pl.BlockSpec((1, d, N_MEM), lambda i: (attn_tile(i) // nt, 0, 0)),
                  pl.BlockSpec((1, N_MEM, d), lambda i: (attn_tile(i) // nt, 0, 0)),
                  _const_spec(w_o.shape), _const_spec(g_ffn.shape),
                  _const_spec(w_r.shape), _const_spec(b_r.shape)],
        out_specs=(pl.BlockSpec((1, ts, d), lambda i: (attn_tile(i) // nt, attn_tile(i) % nt, 0)),
                   pl.BlockSpec((ROW_TILES * ts, LANES), lambda i: (route_tile(i), 0)),
                   pl.BlockSpec((1, ts // LANES, LANES), lambda i: (route_tile(i), 0, 0))),
        scratch_shapes=[pltpu.VMEM((ts, d), F32)],
        compiler_params=pltpu.CompilerParams(
            dimension_semantics=("arbitrary",), vmem_limit_bytes=VMEM_LIMIT_BYTES),
        name="xattn",
    )(x, g, w_q, kt, v, w_o, g_ffn, w_r, b_r)


def _plan_kernel(cls_ref, qs_ref, qy_ref, ea_ref, eb_ref, nu_ref):
    cls = cls_ref[...]
    nr, nl = cls.shape
    r = lax.broadcasted_iota(jnp.int32, (nl, nl), 0)
    c = lax.broadcasted_iota(jnp.int32, (nl, nl), 1)
    lanes_before = (r < c).astype(BF16)
    rr = lax.broadcasted_iota(jnp.int32, (nr, nr), 0)
    rc = lax.broadcasted_iota(jnp.int32, (nr, nr), 1)
    rows_before = (rc < rr).astype(BF16)
    ones_l = jnp.ones((nl, nl), BF16)
    ones_r = jnp.ones((nr, nr), BF16)
    n_tab = ea_ref.shape[1]
    tile_i = lax.broadcasted_iota(jnp.int32, (SUBLANES, n_tab), 1).astype(F32)
    run = jnp.zeros((nr, nl), F32)
    pos = jnp.zeros((nr, nl), F32)
    tile_cls = jnp.zeros((SUBLANES, n_tab), F32)
    for k in range(N_CLASSES):
        m = cls == k
        mb = jnp.where(m, 1.0, 0.0).astype(BF16)
        rowtot = _dot(mb, ones_l).astype(BF16)
        rank = _dot(mb, lanes_before) + _dot(rows_before, rowtot)
        total = _dot(ones_r, rowtot)
        pos = jnp.where(m, run * EXPERT_ROW_TILE + rank, pos)
        run = run + jnp.floor((total + (EXPERT_ROW_TILE - 1)) / EXPERT_ROW_TILE)
        end = jnp.concatenate([run[:SUBLANES, :]] * (n_tab // nl), axis=1)
        tile_cls = tile_cls + jnp.where(end <= tile_i, 1.0, 0.0)
    qs_ref[...] = _tile_row_offset(pos, ROW_TILES).astype(jnp.int32)
    qy_ref[...] = _tile_row_offset(pos, H_TILES).astype(jnp.int32)
    tc = jnp.minimum(tile_cls, N_CLASSES - 1.0).astype(jnp.int32)
    grp = sum(jnp.where(tc >= g * PAIRS_PER_GROUP, 1, 0) for g in range(1, MOE_GROUPS))
    pair = tc - grp * PAIRS_PER_GROUP
    pair_lo = jnp.where(pair >= 3, 1, 0) + jnp.where(pair >= 5, 1, 0)
    pair_hi = pair + 1 - jnp.where(pair >= 5, 3, jnp.where(pair >= 3, 2, 0))
    ea_ref[...] = grp * EXPERTS_PER_GROUP + pair_lo
    eb_ref[...] = grp * EXPERTS_PER_GROUP + pair_hi
    nu_ref[...] = run[:SUBLANES, :].astype(jnp.int32)


def _plan(cls2):
    nr, nl = cls2.shape
    n_tiles = _sorted_rows(nr * nl) // EXPERT_ROW_TILE
    n_tab = -(-n_tiles // nl) * nl
    vmem = pl.BlockSpec(memory_space=pltpu.VMEM)
    qs, qy, ea, eb, nu = pl.pallas_call(
        _plan_kernel,
        out_shape=(jax.ShapeDtypeStruct((nr, nl), jnp.int32),
                   jax.ShapeDtypeStruct((nr, nl), jnp.int32),
                   jax.ShapeDtypeStruct((SUBLANES, n_tab), jnp.int32),
                   jax.ShapeDtypeStruct((SUBLANES, n_tab), jnp.int32),
                   jax.ShapeDtypeStruct((SUBLANES, nl), jnp.int32)),
        in_specs=[vmem],
        out_specs=(vmem, vmem, vmem, vmem, vmem),
        name="moe_plan",
    )(cls2)
    return (qs.reshape(nr * nl), qy.reshape(nr * nl),
            ea[0, :n_tiles], eb[0, :n_tiles], nu[0, :1])


def _scatter_kernel(qs_ref, hp_ref, s_ref):
    i = pl.program_id(0)
    tn = hp_ref.shape[0] // ROW_TILES
    group = SUBLANES * ROW_TILES

    @pl.when(i == 0)
    def _():
        zero_rows = _pack_pair(jnp.zeros((hp_ref.shape[0], 2 * LANES), F32))

        def fill(c, carry):
            start = pl.multiple_of(c * hp_ref.shape[0], SUBLANES)
            s_ref[pl.ds(start, hp_ref.shape[0]), :] = zero_rows
            return carry

        lax.fori_loop(0, s_ref.shape[0] // hp_ref.shape[0], fill, 0)

    def body(j, carry):
        src = pl.multiple_of(j * group, SUBLANES)
        for k in range(SUBLANES):
            q = qs_ref[i * tn + j * SUBLANES + k]
            s_ref[pl.ds(q, ROW_TILES, stride=SUBLANES), :] = (
                hp_ref[pl.ds(src + k, ROW_TILES, stride=SUBLANES), :])
        return carry

    lax.fori_loop(0, tn // SUBLANES, body, 0)


def _scatter(qs, hp):
    n = qs.shape[0]
    tn = ROW_MOVE_TILE
    rows = ROW_TILES * _sorted_rows(n)
    grid_spec = pltpu.PrefetchScalarGridSpec(
        num_scalar_prefetch=1,
        grid=(n // tn,),
        in_specs=[pl.BlockSpec((ROW_TILES * tn, LANES), lambda i, qs: (i, 0))],
        out_specs=pl.BlockSpec((rows, LANES), lambda i, qs: (0, 0), pipeline_mode=pl.Buffered(1)),
    )
    return pl.pallas_call(
        _scatter_kernel,
        out_shape=jax.ShapeDtypeStruct((rows, LANES), jnp.uint32),
        grid_spec=grid_spec,
        compiler_params=pltpu.CompilerParams(
            dimension_semantics=("arbitrary",), vmem_limit_bytes=VMEM_LIMIT_RESIDENT_BYTES),
        name="moe_scatter",
    )(qs, hp)


def _expert_kernel(ea_ref, eb_ref, nu_ref, s_ref, wg_ref, wu_ref, wd_ref, y_ref):
    i = pl.program_id(0)
    first = i * EXPERT_STEP_TILES
    rec = EXPERT_ROW_TILE * ROW_TILES
    out = EXPERT_ROW_TILE * H_TILES

    @pl.when(first < nu_ref[0])
    def _():
        for j in range(EXPERT_STEP_TILES):
            t = first + j
            tiles = _from_tile_rows(s_ref[j * rec:(j + 1) * rec, :], ROW_TILES)
            hb = _unpack_pair(jnp.concatenate(tiles[:H_TILES], axis=-1)).astype(BF16)
            y = None
            for e_ref, idx in ((ea_ref, 0), (eb_ref, 1)):
                e = e_ref[t]
                wt = pltpu.unpack_elementwise(tiles[H_TILES], index=idx, packed_dtype=BF16,
                                              unpacked_dtype=F32)
                wt = wt[:, 0:1] + wt[:, 1:2]
                gt = _dot(hb, wg_ref[e])
                act = gt * _sigmoid(gt) * _dot(hb, wu_ref[e]) * wt
                part = _dot(act.astype(BF16), wd_ref[e])
                y = part if y is None else y + part
            y_ref[j * out:(j + 1) * out, :] = _to_tile_rows(_pack_pair(y))

    @pl.when(first >= nu_ref[0])
    def _():
        y_ref[...] = _pack_pair(jnp.zeros((y_ref.shape[0], 2 * LANES), F32))


def _experts(tile_ea, tile_eb, n_used, s, wg, wu, wd):
    rows = s.shape[0] // ROW_TILES
    tb = EXPERT_ROW_TILE * EXPERT_STEP_TILES
    grid_spec = pltpu.PrefetchScalarGridSpec(
        num_scalar_prefetch=3,
        grid=(rows // tb,),
        in_specs=[pl.BlockSpec((ROW_TILES * tb, LANES), lambda i, *_: (i, 0)),
                  _const_spec(wg.shape), _const_spec(wu.shape), _const_spec(wd.shape)],
        out_specs=pl.BlockSpec((H_TILES * tb, LANES), lambda i, *_: (i, 0)),
    )
    return pl.pallas_call(
        _expert_kernel,
        out_shape=jax.ShapeDtypeStruct((H_TILES * rows, LANES), jnp.uint32),
        grid_spec=grid_spec,
        compiler_params=pltpu.CompilerParams(
            dimension_semantics=("arbitrary",), vmem_limit_bytes=VMEM_LIMIT_BYTES),
        name="moe_experts",
    )(tile_ea, tile_eb, n_used, s, wg, wu, wd)


def _unsort_kernel(qy_ref, y_ref, o_ref):
    i = pl.program_id(0)
    tn = o_ref.shape[0] // H_TILES
    group = SUBLANES * H_TILES

    def body(j, carry):
        dst = pl.multiple_of(j * group, SUBLANES)
        for k in range(SUBLANES):
            q = qy_ref[i * tn + j * SUBLANES + k]
            o_ref[pl.ds(dst + k, H_TILES, stride=SUBLANES), :] = (
                y_ref[pl.ds(q, H_TILES, stride=SUBLANES), :])
        return carry

    lax.fori_loop(0, tn // SUBLANES, body, 0)


def _unsort(qy, y):
    n = qy.shape[0]
    tn = ROW_MOVE_TILE
    grid_spec = pltpu.PrefetchScalarGridSpec(
        num_scalar_prefetch=1,
        grid=(n // tn,),
        in_specs=[pl.BlockSpec(y.shape, lambda i, qy: (0, 0), pipeline_mode=pl.Buffered(1))],
        out_specs=pl.BlockSpec((H_TILES * tn, LANES), lambda i, qy: (i, 0)),
    )
    return pl.pallas_call(
        _unsort_kernel,
        out_shape=jax.ShapeDtypeStruct((H_TILES * n, LANES), jnp.uint32),
        grid_spec=grid_spec,
        compiler_params=pltpu.CompilerParams(
            dimension_semantics=("arbitrary",), vmem_limit_bytes=VMEM_LIMIT_RESIDENT_BYTES),
        name="moe_unsort",
    )(qy, y)


def _gather_kernel(qy_ref, x_ref, y_ref, gn_ref, o_ref, rows_ref):
    i = pl.program_id(0)
    tn = x_ref.shape[0]
    group = SUBLANES * H_TILES

    def body(j, carry):
        dst = pl.multiple_of(j * group, SUBLANES)
        for k in range(SUBLANES):
            q = qy_ref[i * tn + j * SUBLANES + k]
            rows_ref[pl.ds(dst + k, H_TILES, stride=SUBLANES), :] = (
                y_ref[pl.ds(q, H_TILES, stride=SUBLANES), :])
        return carry

    lax.fori_loop(0, tn // SUBLANES, body, 0)
    packed = jnp.concatenate(_from_tile_rows(rows_ref[...], H_TILES), axis=-1)
    o_ref[...] = _rmsnorm(x_ref[...] + _unpack_pair(packed), gn_ref[...])


def _gather(qy, x2, y, g_final):
    n, d = x2.shape
    tn = ROW_MOVE_TILE
    grid_spec = pltpu.PrefetchScalarGridSpec(
        num_scalar_prefetch=1,
        grid=(n // tn,),
        in_specs=[pl.BlockSpec((tn, d), lambda i, qy: (i, 0)),
                  pl.BlockSpec(y.shape, lambda i, qy: (0, 0), pipeline_mode=pl.Buffered(1)),
                  pl.BlockSpec(g_final.shape, lambda i, qy: (0, 0), pipeline_mode=pl.Buffered(1))],
        out_specs=pl.BlockSpec((tn, d), lambda i, qy: (i, 0)),
        scratch_shapes=[pltpu.VMEM((H_TILES * tn, LANES), jnp.uint32)],
    )
    return pl.pallas_call(
        _gather_kernel,
        out_shape=jax.ShapeDtypeStruct(x2.shape, F32),
        grid_spec=grid_spec,
        compiler_params=pltpu.CompilerParams(
            dimension_semantics=("arbitrary",), vmem_limit_bytes=VMEM_LIMIT_RESIDENT_BYTES),
        name="moe_gather",
    )(qy, x2, y, g_final)


def _row(v):
    return v.reshape(1, -1)


def _router_params(w_rg, b_rg, w_re, b_re):
    pad = ROUTER_ROWS - MOE_GROUPS - N_EXPERTS
    wt = jnp.pad(jnp.concatenate([w_rg, w_re], axis=1).T, ((0, pad), (0, 0)))
    w_hi = wt.astype(BF16)
    w_lo = (wt - w_hi.astype(F32)).astype(BF16)
    b_col = jnp.pad(jnp.concatenate([b_rg, b_re]), (0, pad)).reshape(ROUTER_ROWS, 1)
    return jnp.concatenate([w_hi, w_lo], axis=0), b_col


def kernel(x, mem, norm_mix_g, w_in, conv_a_w, conv_a_b, ln_a_g, ln_a_b, w_a_out, w_pool_grp, pool_scale, conv_c_w, w_c_out, w_o, norm_x_g, norm_mem_g, w_xq, w_xkv, w_xo, norm_ffn_g, w_rg, b_rg, w_re, b_re, w_e_gate, w_e_up, w_e_down, norm_f_g):
    bsz, seq, d = x.shape
    n = bsz * seq
    depth = w_in.shape[0]
    moe_delta = None
    for l in range(depth):
        caw = jnp.broadcast_to(conv_a_w[l][:, None, :], (CONV_A_K, SUBLANES, D_CONV))
        x = _mixer(x, moe_delta, _row(norm_mix_g[l]), w_in[l].astype(BF16), caw, _row(conv_a_b[l]),
                   _row(ln_a_g[l]), _row(ln_a_b[l]), w_a_out[l].astype(BF16),
                   w_pool_grp[l].astype(BF16), _row(pool_scale[l]), conv_c_w[l],
                   w_c_out[l].astype(BF16), w_o[l].astype(BF16))
        kt, v = _kv_proj(mem, _row(norm_mem_g[l]), w_xkv[l].astype(BF16))
        w_r, b_r = _router_params(w_rg[l], b_rg[l], w_re[l], b_re[l])
        x, hp, cls = _xattn(x, _row(norm_x_g[l]), w_xq[l].astype(BF16), kt, v,
                                  w_xo[l].astype(BF16), _row(norm_ffn_g[l]), w_r, b_r)
        qs, qy, tile_ea, tile_eb, n_used = _plan(cls.reshape(n // LANES, LANES))
        y = _experts(tile_ea, tile_eb, n_used, _scatter(qs, hp), w_e_gate[l].astype(BF16),
                     w_e_up[l].astype(BF16), w_e_down[l].astype(BF16))
        if l < depth - 1:
            moe_delta = _unsort(qy, y)
    return _gather(qy, x.reshape(n, d), y, _row(norm_f_g)).reshape(bsz, seq, d)
```

```python
import functools

import jax
import jax.numpy as jnp
from jax import lax
from jax.experimental import pallas as pl
from jax.experimental.pallas import tpu as pltpu

D_MODEL = 1024
D_CONV = D_MODEL // 2
D_POOL = D_MODEL // 2
D_SC = D_MODEL // 2
CONV_A_K = 31
SC_K = 3
POOL_WINDOWS = (2, 4, 8, 16)
POOL_GROUP_DIM = D_POOL // len(POOL_WINDOWS)
POOL_OUT_DIM = D_MODEL // len(POOL_WINDOWS)
N_MEM = 256
XATTN_HEADS = 4
XATTN_HEAD_DIM = D_MODEL // XATTN_HEADS
MOE_GROUPS = 4
EXPERTS_PER_GROUP = 4
N_EXPERTS = MOE_GROUPS * EXPERTS_PER_GROUP
D_EXPERT = D_MODEL // 4
EPS = 1e-6

OFF_A = 0
OFF_POOL = 2 * D_CONV
OFF_C = OFF_POOL + D_POOL
OFF_G = OFF_C + 3 * D_SC
D_IN_PROJ = OFF_G + 3 * D_MODEL

SUBLANES = 8
LANES = 128
VMEM_LIMIT_BYTES = 56 * 1024 * 1024

HIST_A = 32
HIST_POOL = 16
HIST_C = 8

SEQ_TILE = 512
XATTN_SEQ_TILE = 1024
CONV_ROW_CHUNK = 16
MIXER_W_STEPS = 8
ROUTER_LANES = LANES

PAIRS_PER_GROUP = EXPERTS_PER_GROUP * (EXPERTS_PER_GROUP - 1) // 2
N_CLASSES = MOE_GROUPS * PAIRS_PER_GROUP
EXPERT_ROW_TILE = 128
EXPERT_STEP_TILES = 8
ROUTER_ROWS = 32
ROW_MOVE_TILE = 512
VMEM_LIMIT_RESIDENT_BYTES = 58 * 1024 * 1024
H_TILES = D_MODEL // 2 // LANES
ROW_TILES = H_TILES + 1

BF16 = jnp.bfloat16
F32 = jnp.float32


def _sorted_rows(n_tokens):
    return n_tokens + N_CLASSES * EXPERT_ROW_TILE


def _sigmoid(v):
    return 0.5 * jnp.tanh(0.5 * v) + 0.5


def _rmsnorm(xf, g):
    return xf * lax.rsqrt(jnp.mean(xf * xf, axis=-1, keepdims=True) + EPS) * g


def _dot(a, b):
    return jnp.dot(a, b, preferred_element_type=F32)


def _const_spec(shape):
    nd = len(shape)
    return pl.BlockSpec(shape, lambda *_: (0,) * nd, pipeline_mode=pl.Buffered(1))


def _unpack_pair(q):
    return jnp.concatenate(
        [pltpu.unpack_elementwise(q, index=0, packed_dtype=BF16, unpacked_dtype=F32),
         pltpu.unpack_elementwise(q, index=1, packed_dtype=BF16, unpacked_dtype=F32)], axis=-1)


def _pack_pair(v):
    half = v.shape[-1] // 2
    return pltpu.pack_elementwise([v[:, :half], v[:, half:]], packed_dtype=BF16)


def _to_tile_rows(v):
    t, w = v.shape
    nj = w // LANES
    parts = [v[:, j * LANES:(j + 1) * LANES].reshape(t // SUBLANES, 1, SUBLANES, LANES)
             for j in range(nj)]
    return jnp.concatenate(parts, axis=1).reshape(nj * t, LANES)


def _from_tile_rows(v2, nj):
    t = v2.shape[0] // nj
    v4 = v2.reshape(t // SUBLANES, nj, SUBLANES, LANES)
    return [v4[:, j].reshape(t, LANES) for j in range(nj)]


def _tile_row_offset(r, nj):
    hi = jnp.floor(r / SUBLANES)
    return hi * (SUBLANES * nj) + (r - hi * SUBLANES)


def _mixer_kernel(x_ref, *refs, has_moe_delta, tiles_per_seq):
    y_ref, refs = (refs[0], refs[1:]) if has_moe_delta else (None, refs)
    (g_ref, w_in32, caw_ref, cab_ref, lng_ref, lnb_ref, w_a_out32, w_pool32, pool_scale_ref,
     ccw_ref, w_c_out32, w_o32, o_ref, abuf, ubuf, vbuf, actbuf, pbuf,
     w_in_ref, w_a_out_ref, w_pool_ref, w_c_out_ref, w_o_ref) = refs
    step = pl.program_id(0)

    @pl.when(step < MIXER_W_STEPS)
    def _():
        for src, dst in ((w_in32, w_in_ref), (w_a_out32, w_a_out_ref),
                         (w_c_out32, w_c_out_ref), (w_o32, w_o_ref)):
            rows = src.shape[0]
            dst[pl.ds(pl.multiple_of(step * rows, rows), rows), :] = src[...].astype(BF16)
        rows = w_pool32.shape[1]
        w_pool_ref[:, pl.ds(pl.multiple_of(step * rows, rows), rows), :] = (
            w_pool32[...].astype(BF16))

    @pl.when(step >= MIXER_W_STEPS)
    def _():
        _mix_tile((step - MIXER_W_STEPS) % tiles_per_seq, x_ref, y_ref, g_ref, w_in_ref, caw_ref,
                  cab_ref, lng_ref, lnb_ref, w_a_out_ref, w_pool_ref, pool_scale_ref, ccw_ref,
                  w_c_out_ref, w_o_ref, o_ref, abuf, ubuf, vbuf, actbuf, pbuf)


def _mix_tile(s, x_ref, y_ref, g_ref, w_in_ref, caw_ref, cab_ref, lng_ref, lnb_ref, w_a_out_ref,
              w_pool_ref, pool_scale_ref, ccw_ref, w_c_out_ref, w_o_ref,
              o_ref, abuf, ubuf, vbuf, actbuf, pbuf):
    ts = x_ref.shape[1]

    @pl.when(s == 0)
    def _():
        abuf[0, 0:HIST_A, :] = jnp.zeros((HIST_A, D_CONV), F32)
        ubuf[0:HIST_POOL, :] = jnp.zeros((HIST_POOL, D_POOL), F32)
        vbuf[0:HIST_C, :] = jnp.zeros((HIST_C, D_SC), F32)

    x = x_ref[0]
    if y_ref is not None:
        x = x + _unpack_pair(jnp.concatenate(_from_tile_rows(y_ref[...], H_TILES), axis=-1))
    h = _rmsnorm(x, g_ref[...]).astype(BF16)

    def proj(off, width):
        return _dot(h, w_in_ref[:, off:off + width])

    pa = proj(OFF_A, 2 * D_CONV)
    abuf[0, HIST_A:HIST_A + ts, :] = pa[:, :D_CONV] * _sigmoid(pa[:, D_CONV:])
    shifted_rows = HIST_A + ts - SUBLANES
    for r in range(1, SUBLANES):
        abuf[r, 0:shifted_rows, :] = abuf[0, r:r + shifted_rows, :]
    chunk3 = (CONV_ROW_CHUNK // SUBLANES, SUBLANES, D_CONV)
    for c0 in range(0, ts, CONV_ROW_CHUNK):
        acc = jnp.broadcast_to(cab_ref[...], chunk3)
        for k in range(CONV_A_K):
            q, r = divmod(HIST_A - (CONV_A_K - 1) + k, SUBLANES)
            lo = c0 + q * SUBLANES
            acc = acc + caw_ref[k] * abuf[r, lo:lo + CONV_ROW_CHUNK, :].reshape(chunk3)
        acc = acc.reshape(CONV_ROW_CHUNK, D_CONV)
        mu = jnp.mean(acc, axis=-1, keepdims=True)
        cen = acc - mu
        var = jnp.mean(cen * cen, axis=-1, keepdims=True)
        ln = cen * lax.rsqrt(var + EPS) * lng_ref[...] + lnb_ref[...]
        actbuf[c0:c0 + CONV_ROW_CHUNK, :] = (ln * _sigmoid(ln)).astype(BF16)
    abuf[0, 0:HIST_A, :] = abuf[0, ts:ts + HIST_A, :]

    pbuf[...] = proj(OFF_POOL, D_IN_PROJ - OFF_POOL)

    def pcol(off, width):
        return pbuf[:, off - OFF_POOL:off - OFF_POOL + width]

    pu = pcol(OFF_POOL, D_POOL)
    ubuf[HIST_POOL:HIST_POOL + ts, :] = pu
    merged = _sigmoid(pcol(OFF_G, D_MODEL)) * _dot(actbuf[...], w_a_out_ref[...])

    t_glob = s * ts + lax.broadcasted_iota(jnp.int32, (ts, 1), 0)
    p_parts = []
    for i, w in enumerate(POOL_WINDOWS):
        c0, c1 = i * POOL_GROUP_DIM, (i + 1) * POOL_GROUP_DIM
        tok = pu[:, c0:c1]
        win = tok
        for j in range(1, w):
            win = win + ubuf[HIST_POOL - j:HIST_POOL - j + ts, c0:c1]
        cnt = jnp.minimum(t_glob + 1, w).astype(F32)
        pin = (win / cnt - tok).astype(BF16)
        p_parts.append(_dot(pin, w_pool_ref[i]))
    ubuf[0:HIST_POOL, :] = ubuf[ts:ts + HIST_POOL, :]
    merged = merged + _sigmoid(pcol(OFF_G + D_MODEL, D_MODEL)) * (jnp.concatenate(p_parts, axis=-1) * pool_scale_ref[...])

    c_b = pcol(OFF_C + D_SC, D_SC)
    v = pcol(OFF_C + 2 * D_SC, D_SC) * pcol(OFF_C, D_SC)
    vbuf[HIST_C:HIST_C + ts, :] = v
    conv_c = ccw_ref[SC_K - 1:SC_K, :] * v
    for k in range(SC_K - 1):
        off = HIST_C - (SC_K - 1) + k
        conv_c = conv_c + ccw_ref[k:k + 1, :] * vbuf[off:off + ts, :]
    vbuf[0:HIST_C, :] = vbuf[ts:ts + HIST_C, :]
    merged = merged + _sigmoid(pcol(OFF_G + 2 * D_MODEL, D_MODEL)) * _dot((c_b * conv_c).astype(BF16), w_c_out_ref[...])

    o_ref[0] = x + _dot(merged.astype(BF16), w_o_ref[...])


def _mixer(x, moe_delta, layer, g, w_in, caw, cab, lng, lnb, w_a_out, w_pool, pool_scale, ccw,
           w_c_out, w_o):
    b, s, d = x.shape
    ts = SEQ_TILE
    nt = s // ts

    def tile(i):
        return jnp.maximum(i - MIXER_W_STEPS, 0)

    def slab(i):
        return jnp.minimum(i, MIXER_W_STEPS - 1)

    def small(c):
        return _const_spec(c.shape)

    def slab_spec(w, axis):
        block = list(w.shape[1:])
        block[axis] //= MIXER_W_STEPS
        return pl.BlockSpec(
            (None, *block),
            lambda i: (layer,) + tuple(slab(i) if a == axis else 0 for a in range(len(block))))

    delta = () if moe_delta is None else (moe_delta,)
    delta_specs = [pl.BlockSpec((H_TILES * ts, LANES), lambda i: (tile(i), 0))] * len(delta)
    x_spec = pl.BlockSpec((1, ts, d), lambda i: (tile(i) // nt, tile(i) % nt, 0))
    return pl.pallas_call(
        functools.partial(_mixer_kernel, has_moe_delta=moe_delta is not None, tiles_per_seq=nt),
        out_shape=jax.ShapeDtypeStruct(x.shape, F32),
        grid=(MIXER_W_STEPS + b * nt,),
        in_specs=[x_spec] + delta_specs
        + [small(g), slab_spec(w_in, 0), small(caw), small(cab), small(lng), small(lnb),
           slab_spec(w_a_out, 0), slab_spec(w_pool, 1), small(pool_scale), small(ccw),
           slab_spec(w_c_out, 0), slab_spec(w_o, 0)],
        out_specs=x_spec,
        scratch_shapes=[pltpu.VMEM((SUBLANES, HIST_A + ts, D_CONV), F32),
                        pltpu.VMEM((HIST_POOL + ts, D_POOL), F32),
                        pltpu.VMEM((HIST_C + ts, D_SC), F32),
                        pltpu.VMEM((ts, D_CONV), BF16),
                        pltpu.VMEM((ts, D_IN_PROJ - OFF_POOL), F32)]
        + [pltpu.VMEM(w.shape[1:], BF16) for w in (w_in, w_a_out, w_pool, w_c_out, w_o)],
        compiler_params=pltpu.CompilerParams(
            dimension_semantics=("arbitrary",), vmem_limit_bytes=VMEM_LIMIT_BYTES),
        name="mixer",
    )(x, *delta, g, w_in, caw, cab, lng, lnb, w_a_out, w_pool, pool_scale, ccw, w_c_out, w_o)


def _kv_kernel(mem_ref, g_ref, w_kv_ref, kt_ref, v_ref):
    mn = _rmsnorm(mem_ref[0], g_ref[...]).astype(BF16)
    kv = _dot(mn, w_kv_ref[...])
    kt_ref[0] = kv[:, :D_MODEL].T.astype(BF16)
    v_ref[0] = kv[:, D_MODEL:].astype(BF16)


def _kv_proj(mem, g, w_kv):
    b, m, d = mem.shape
    return pl.pallas_call(
        _kv_kernel,
        out_shape=(jax.ShapeDtypeStruct((b, d, m), BF16), jax.ShapeDtypeStruct((b, m, d), BF16)),
        grid=(b,),
        in_specs=[pl.BlockSpec((1, m, d), lambda i: (i, 0, 0)),
                  _const_spec(g.shape), _const_spec(w_kv.shape)],
        out_specs=(pl.BlockSpec((1, d, m), lambda i: (i, 0, 0)),
                   pl.BlockSpec((1, m, d), lambda i: (i, 0, 0))),
        compiler_params=pltpu.CompilerParams(
            dimension_semantics=("arbitrary",), vmem_limit_bytes=VMEM_LIMIT_BYTES),
        name="kv_proj",
    )(mem, g, w_kv)


def _route_rows(lt):
    def first_argmax(vals, vmax):
        idx = jnp.full(vmax.shape, len(vals) - 1, jnp.int32)
        for k in range(len(vals) - 2, -1, -1):
            idx = jnp.where(vals[k] == vmax, k, idx)
        return idx

    g = [lt[k:k + 1, :] for k in range(MOE_GROUPS)]
    gmax = functools.reduce(jnp.maximum, g)
    g_idx = first_argmax(g, gmax)
    g_val = 1.0 / sum(jnp.exp(v - gmax) for v in g)
    e = []
    for j in range(EXPERTS_PER_GROUP):
        ej = lt[MOE_GROUPS + j:MOE_GROUPS + j + 1, :]
        for grp in range(1, MOE_GROUPS):
            r = MOE_GROUPS + grp * EXPERTS_PER_GROUP + j
            ej = jnp.where(g_idx == grp, lt[r:r + 1, :], ej)
        e.append(ej)
    m1 = functools.reduce(jnp.maximum, e)
    i1 = first_argmax(e, m1)
    e2 = [jnp.where(i1 == j, -jnp.inf, e[j]) for j in range(EXPERTS_PER_GROUP)]
    m2 = functools.reduce(jnp.maximum, e2)
    i2 = first_argmax(e2, m2)
    esum = sum(jnp.exp(v - m1) for v in e)
    p1 = 1.0 / esum
    p2 = jnp.exp(m2 - m1) / esum
    w1 = g_val * (p1 / (p1 + p2))
    w2 = g_val * (p2 / (p1 + p2))
    lo = jnp.minimum(i1, i2)
    hi = jnp.maximum(i1, i2)
    pair = ((lo * (2 * EXPERTS_PER_GROUP - 1 - lo)) >> 1) + (hi - lo - 1)
    cls = g_idx * PAIRS_PER_GROUP + pair
    first_is_low = i1 < i2
    return cls, jnp.where(first_is_low, w1, w2), jnp.where(first_is_low, w2, w1)


def _xattn_kernel(x_ref, g_ref, w_q_ref, kt_ref, v_ref, w_o_ref, gf_ref, w_r_ref, b_r_ref,
                  o_ref, hp_ref, cls_ref, xn_prev):
    @pl.when(pl.program_id(0) == 0)
    def _():
        xn_prev[...] = jnp.zeros(xn_prev.shape, F32)

    xp = xn_prev[...]
    x = x_ref[0]
    h = _rmsnorm(x, g_ref[...]).astype(BF16)
    q = (_dot(h, w_q_ref[...]) * (XATTN_HEAD_DIM ** -0.5)).astype(BF16)
    heads = []
    for hd in range(XATTN_HEADS):
        c0, c1 = hd * XATTN_HEAD_DIM, (hd + 1) * XATTN_HEAD_DIM
        sc = _dot(q[:, c0:c1], kt_ref[0, c0:c1, :])
        e = jnp.exp(sc - jnp.max(sc, axis=-1, keepdims=True))
        denom = jnp.sum(e, axis=-1, keepdims=True)
        heads.append(_dot(e.astype(BF16), v_ref[0, :, c0:c1]) / denom)
    o = jnp.concatenate(heads, axis=-1).astype(BF16)
    xn = x + _dot(o, w_o_ref[...])
    o_ref[0] = xn
    xn_prev[...] = xn

    hf = _rmsnorm(xp, gf_ref[...])
    h_hi = hf.astype(BF16)
    h_lo = (hf - h_hi.astype(F32)).astype(BF16)
    nt_dims = (((1,), (1,)), ((), ()))
    part = lax.dot_general(w_r_ref[...], h_hi, nt_dims, preferred_element_type=F32)
    lt = (part[:ROUTER_ROWS] + part[ROUTER_ROWS:]
          + lax.dot_general(w_r_ref[:ROUTER_ROWS, :], h_lo, nt_dims, preferred_element_type=F32)
          + b_r_ref[...])
    cls, wa, wb = _route_rows(lt)
    ts = cls.shape[1]
    for c in range(ts // LANES):
        cls_ref[0, c:c + 1, :] = cls[:, c * LANES:(c + 1) * LANES]
    row = lax.broadcasted_iota(jnp.int32, (LANES, ts), 0)

    def weight_lanes(w):
        w_hi = w.astype(BF16).astype(F32)
        return jnp.where(row == 0, w_hi, jnp.where(row == 1, w - w_hi, 0.0)).T

    wt_tile = pltpu.pack_elementwise([weight_lanes(wa), weight_lanes(wb)], packed_dtype=BF16)
    hp_ref[...] = _to_tile_rows(jnp.concatenate([_pack_pair(hf), wt_tile], axis=-1))


def _xattn(x, g, w_q, kt, v, w_o, g_ffn, w_r, b_r):
    b, s, d = x.shape
    ts = XATTN_SEQ_TILE
    nt = s // ts
    n = b * s
    tiles = b * nt

    def attn_tile(step):
        return jnp.minimum(step, tiles - 1)

    def route_tile(step):
        return jnp.maximum(step - 1, 0)

    return pl.pallas_call(
        _xattn_kernel,
        out_shape=(jax.ShapeDtypeStruct(x.shape, F32),
                   jax.ShapeDtypeStruct((ROW_TILES * n, LANES), jnp.uint32),
                   jax.ShapeDtypeStruct((n // ts, ts // LANES, LANES), jnp.int32)),
        grid=(tiles + 1,),
        in_specs=[pl.BlockSpec((1, ts, d), lambda i: (attn_tile(i) // nt, attn_tile(i) % nt, 0)),
                  _const_spec(g.shape), _const_spec(w_q.shape),
                  pl.BlockSpec((1, d, N_MEM), lambda i: (attn_tile(i) // nt, 0, 0)),
                  pl.BlockSpec((1, N_MEM, d), lambda i: (attn_tile(i) // nt, 0, 0)),
                  _const_spec(w_o.shape), _const_spec(g_ffn.shape),
                  _const_spec(w_r.shape), _const_spec(b_r.shape)],
        out_specs=(pl.BlockSpec((1, ts, d), lambda i: (attn_tile(i) // nt, attn_tile(i) % nt, 0)),
                   pl.BlockSpec((ROW_TILES * ts, LANES), lambda i: (route_tile(i), 0)),
                   pl.BlockSpec((1, ts // LANES, LANES), lambda i: (route_tile(i), 0, 0))),
        scratch_shapes=[pltpu.VMEM((ts, d), F32)],
        compiler_params=pltpu.CompilerParams(
            dimension_semantics=("arbitrary",), vmem_limit_bytes=VMEM_LIMIT_BYTES),
        name="xattn",
    )(x, g, w_q, kt, v, w_o, g_ffn, w_r, b_r)


def _plan_kernel(cls_ref, qs_ref, qy_ref, ea_ref, eb_ref, nu_ref):
    cls = cls_ref[...]
    nr, nl = cls.shape
    r = lax.broadcasted_iota(jnp.int32, (nl, nl), 0)
    c = lax.broadcasted_iota(jnp.int32, (nl, nl), 1)
    lanes_before = (r < c).astype(BF16)
    rr = lax.broadcasted_iota(jnp.int32, (nr, nr), 0)
    rc = lax.broadcasted_iota(jnp.int32, (nr, nr), 1)
    rows_before = (rc < rr).astype(BF16)
    ones_l = jnp.ones((nl, nl), BF16)
    ones_r = jnp.ones((nr, nr), BF16)
    n_tab = ea_ref.shape[1]
    tile_i = lax.broadcasted_iota(jnp.int32, (SUBLANES, n_tab), 1).astype(F32)
    run = jnp.zeros((nr, nl), F32)
    pos = jnp.zeros((nr, nl), F32)
    tile_cls = jnp.zeros((SUBLANES, n_tab), F32)
    for k in range(N_CLASSES):
        m = cls == k
        mb = jnp.where(m, 1.0, 0.0).astype(BF16)
        rowtot = _dot(mb, ones_l).astype(BF16)
        rank = _dot(mb, lanes_before) + _dot(rows_before, rowtot)
        total = _dot(ones_r, rowtot)
        pos = jnp.where(m, run * EXPERT_ROW_TILE + rank, pos)
        run = run + jnp.floor((total + (EXPERT_ROW_TILE - 1)) / EXPERT_ROW_TILE)
        end = jnp.concatenate([run[:SUBLANES, :]] * (n_tab // nl), axis=1)
        tile_cls = tile_cls + jnp.where(end <= tile_i, 1.0, 0.0)
    qs_ref[...] = _tile_row_offset(pos, ROW_TILES).astype(jnp.int32)
    qy_ref[...] = _tile_row_offset(pos, H_TILES).astype(jnp.int32)
    tc = jnp.minimum(tile_cls, N_CLASSES - 1.0).astype(jnp.int32)
    grp = sum(jnp.where(tc >= g * PAIRS_PER_GROUP, 1, 0) for g in range(1, MOE_GROUPS))
    pair = tc - grp * PAIRS_PER_GROUP
    pair_lo = jnp.where(pair >= 3, 1, 0) + jnp.where(pair >= 5, 1, 0)
    pair_hi = pair + 1 - jnp.where(pair >= 5, 3, jnp.where(pair >= 3, 2, 0))
    ea_ref[...] = grp * EXPERTS_PER_GROUP + pair_lo
    eb_ref[...] = grp * EXPERTS_PER_GROUP + pair_hi
    nu_ref[...] = run[:SUBLANES, :].astype(jnp.int32)


def _plan(cls2):
    nr, nl = cls2.shape
    n_tiles = _sorted_rows(nr * nl) // EXPERT_ROW_TILE
    n_tab = -(-n_tiles // nl) * nl
    vmem = pl.BlockSpec(memory_space=pltpu.VMEM)
    qs, qy, ea, eb, nu = pl.pallas_call(
        _plan_kernel,
        out_shape=(jax.ShapeDtypeStruct((nr, nl), jnp.int32),
                   jax.ShapeDtypeStruct((nr, nl), jnp.int32),
                   jax.ShapeDtypeStruct((SUBLANES, n_tab), jnp.int32),
                   jax.ShapeDtypeStruct((SUBLANES, n_tab), jnp.int32),
                   jax.ShapeDtypeStruct((SUBLANES, nl), jnp.int32)),
        in_specs=[vmem],
        out_specs=(vmem, vmem, vmem, vmem, vmem),
        name="moe_plan",
    )(cls2)
    return (qs.reshape(nr * nl), qy.reshape(nr * nl),
            ea[0, :n_tiles], eb[0, :n_tiles], nu[0, :1])


def _scatter_kernel(qs_ref, hp_ref, s_ref):
    i = pl.program_id(0)
    tn = hp_ref.shape[0] // ROW_TILES
    group = SUBLANES * ROW_TILES

    @pl.when(i == 0)
    def _():
        zero_rows = _pack_pair(jnp.zeros((hp_ref.shape[0], 2 * LANES), F32))

        def fill(c, carry):
            start = pl.multiple_of(c * hp_ref.shape[0], SUBLANES)
            s_ref[pl.ds(start, hp_ref.shape[0]), :] = zero_rows
            return carry

        lax.fori_loop(0, s_ref.shape[0] // hp_ref.shape[0], fill, 0)

    def body(j, carry):
        src = pl.multiple_of(j * group, SUBLANES)
        for k in range(SUBLANES):
            q = qs_ref[i * tn + j * SUBLANES + k]
            s_ref[pl.ds(q, ROW_TILES, stride=SUBLANES), :] = (
                hp_ref[pl.ds(src + k, ROW_TILES, stride=SUBLANES), :])
        return carry

    lax.fori_loop(0, tn // SUBLANES, body, 0)


def _scatter(qs, hp):
    n = qs.shape[0]
    tn = ROW_MOVE_TILE
    rows = ROW_TILES * _sorted_rows(n)
    grid_spec = pltpu.PrefetchScalarGridSpec(
        num_scalar_prefetch=1,
        grid=(n // tn,),
        in_specs=[pl.BlockSpec((ROW_TILES * tn, LANES), lambda i, qs: (i, 0))],
        out_specs=pl.BlockSpec((rows, LANES), lambda i, qs: (0, 0), pipeline_mode=pl.Buffered(1)),
    )
    return pl.pallas_call(
        _scatter_kernel,
        out_shape=jax.ShapeDtypeStruct((rows, LANES), jnp.uint32),
        grid_spec=grid_spec,
        compiler_params=pltpu.CompilerParams(
            dimension_semantics=("arbitrary",), vmem_limit_bytes=VMEM_LIMIT_RESIDENT_BYTES),
        name="moe_scatter",
    )(qs, hp)


def _expert_kernel(ea_ref, eb_ref, nu_ref, s_ref, wg32_ref, wu32_ref, wd32_ref, y_ref,
                   wg_ref, wu_ref, wd_ref):
    step = pl.program_id(0)

    @pl.when(step < N_EXPERTS)
    def _():
        wg_ref[step] = wg32_ref[...].astype(BF16)
        wu_ref[step] = wu32_ref[...].astype(BF16)
        wd_ref[step] = wd32_ref[...].astype(BF16)

    first = (step - N_EXPERTS) * EXPERT_STEP_TILES
    rec = EXPERT_ROW_TILE * ROW_TILES
    out = EXPERT_ROW_TILE * H_TILES

    @pl.when((step >= N_EXPERTS) & (first < nu_ref[0]))
    def _():
        for j in range(EXPERT_STEP_TILES):
            t = first + j
            tiles = _from_tile_rows(s_ref[j * rec:(j + 1) * rec, :], ROW_TILES)
            hb = _unpack_pair(jnp.concatenate(tiles[:H_TILES], axis=-1)).astype(BF16)
            y = None
            for e_ref, idx in ((ea_ref, 0), (eb_ref, 1)):
                e = e_ref[t]
                wt = pltpu.unpack_elementwise(tiles[H_TILES], index=idx, packed_dtype=BF16,
                                              unpacked_dtype=F32)
                wt = wt[:, 0:1] + wt[:, 1:2]
                gt = _dot(hb, wg_ref[e])
                act = gt * _sigmoid(gt) * _dot(hb, wu_ref[e]) * wt
                part = _dot(act.astype(BF16), wd_ref[e])
                y = part if y is None else y + part
            y_ref[j * out:(j + 1) * out, :] = _to_tile_rows(_pack_pair(y))

    @pl.when((step >= N_EXPERTS) & (first >= nu_ref[0]))
    def _():
        y_ref[...] = _pack_pair(jnp.zeros((y_ref.shape[0], 2 * LANES), F32))


def _experts(tile_ea, tile_eb, n_used, s, wg, wu, wd, layer):
    rows = s.shape[0] // ROW_TILES
    tb = EXPERT_ROW_TILE * EXPERT_STEP_TILES

    def row_block(i, *_):
        return (jnp.maximum(i - N_EXPERTS, 0), 0)

    def expert_block(i, *_):
        return (layer, jnp.minimum(i, N_EXPERTS - 1), 0, 0)

    grid_spec = pltpu.PrefetchScalarGridSpec(
        num_scalar_prefetch=3,
        grid=(N_EXPERTS + rows // tb,),
        in_specs=[pl.BlockSpec((ROW_TILES * tb, LANES), row_block),
                  pl.BlockSpec((None, None) + wg.shape[2:], expert_block),
                  pl.BlockSpec((None, None) + wu.shape[2:], expert_block),
                  pl.BlockSpec((None, None) + wd.shape[2:], expert_block)],
        out_specs=pl.BlockSpec((H_TILES * tb, LANES), row_block),
        scratch_shapes=[pltpu.VMEM(wg.shape[1:], BF16), pltpu.VMEM(wu.shape[1:], BF16),
                        pltpu.VMEM(wd.shape[1:], BF16)],
    )
    return pl.pallas_call(
        _expert_kernel,
        out_shape=jax.ShapeDtypeStruct((H_TILES * rows, LANES), jnp.uint32),
        grid_spec=grid_spec,
        compiler_params=pltpu.CompilerParams(
            dimension_semantics=("arbitrary",), vmem_limit_bytes=VMEM_LIMIT_BYTES),
        name="moe_experts",
    )(tile_ea, tile_eb, n_used, s, wg, wu, wd)


def _unsort_kernel(qy_ref, y_ref, o_ref):
    i = pl.program_id(0)
    tn = o_ref.shape[0] // H_TILES
    group = SUBLANES * H_TILES

    def body(j, carry):
        dst = pl.multiple_of(j * group, SUBLANES)
        for k in range(SUBLANES):
            q = qy_ref[i * tn + j * SUBLANES + k]
            o_ref[pl.ds(dst + k, H_TILES, stride=SUBLANES), :] = (
                y_ref[pl.ds(q, H_TILES, stride=SUBLANES), :])
        return carry

    lax.fori_loop(0, tn // SUBLANES, body, 0)


def _unsort(qy, y):
    n = qy.shape[0]
    tn = ROW_MOVE_TILE
    grid_spec = pltpu.PrefetchScalarGridSpec(
        num_scalar_prefetch=1,
        grid=(n // tn,),
        in_specs=[pl.BlockSpec(y.shape, lambda i, qy: (0, 0), pipeline_mode=pl.Buffered(1))],
        out_specs=pl.BlockSpec((H_TILES * tn, LANES), lambda i, qy: (i, 0)),
    )
    return pl.pallas_call(
        _unsort_kernel,
        out_shape=jax.ShapeDtypeStruct((H_TILES * n, LANES), jnp.uint32),
        grid_spec=grid_spec,
        compiler_params=pltpu.CompilerParams(
            dimension_semantics=("arbitrary",), vmem_limit_bytes=VMEM_LIMIT_RESIDENT_BYTES),
        name="moe_unsort",
    )(qy, y)


def _gather_kernel(qy_ref, x_ref, y_ref, gn_ref, o_ref, rows_ref):
    i = pl.program_id(0)
    tn = x_ref.shape[0]
    group = SUBLANES * H_TILES

    def body(j, carry):
        dst = pl.multiple_of(j * group, SUBLANES)
        for k in range(SUBLANES):
            q = qy_ref[i * tn + j * SUBLANES + k]
            rows_ref[pl.ds(dst + k, H_TILES, stride=SUBLANES), :] = (
                y_ref[pl.ds(q, H_TILES, stride=SUBLANES), :])
        return carry

    lax.fori_loop(0, tn // SUBLANES, body, 0)
    packed = jnp.concatenate(_from_tile_rows(rows_ref[...], H_TILES), axis=-1)
    o_ref[...] = _rmsnorm(x_ref[...] + _unpack_pair(packed), gn_ref[...])


def _gather(qy, x2, y, g_final):
    n, d = x2.shape
    tn = ROW_MOVE_TILE
    grid_spec = pltpu.PrefetchScalarGridSpec(
        num_scalar_prefetch=1,
        grid=(n // tn,),
        in_specs=[pl.BlockSpec((tn, d), lambda i, qy: (i, 0)),
                  pl.BlockSpec(y.shape, lambda i, qy: (0, 0), pipeline_mode=pl.Buffered(1)),
                  pl.BlockSpec(g_final.shape, lambda i, qy: (0, 0), pipeline_mode=pl.Buffered(1))],
        out_specs=pl.BlockSpec((tn, d), lambda i, qy: (i, 0)),
        scratch_shapes=[pltpu.VMEM((H_TILES * tn, LANES), jnp.uint32)],
    )
    return pl.pallas_call(
        _gather_kernel,
        out_shape=jax.ShapeDtypeStruct(x2.shape, F32),
        grid_spec=grid_spec,
        compiler_params=pltpu.CompilerParams(
            dimension_semantics=("arbitrary",), vmem_limit_bytes=VMEM_LIMIT_RESIDENT_BYTES),
        name="moe_gather",
    )(qy, x2, y, g_final)


def _row(v):
    return v.reshape(1, -1)


def _router_params(w_rg, b_rg, w_re, b_re):
    pad = ROUTER_ROWS - MOE_GROUPS - N_EXPERTS
    wt = jnp.pad(jnp.concatenate([w_rg, w_re], axis=1).T, ((0, pad), (0, 0)))
    w_hi = wt.astype(BF16)
    w_lo = (wt - w_hi.astype(F32)).astype(BF16)
    b_col = jnp.pad(jnp.concatenate([b_rg, b_re]), (0, pad)).reshape(ROUTER_ROWS, 1)
    return jnp.concatenate([w_hi, w_lo], axis=0), b_col


def kernel(x, mem, norm_mix_g, w_in, conv_a_w, conv_a_b, ln_a_g, ln_a_b, w_a_out, w_pool_grp, pool_scale, conv_c_w, w_c_out, w_o, norm_x_g, norm_mem_g, w_xq, w_xkv, w_xo, norm_ffn_g, w_rg, b_rg, w_re, b_re, w_e_gate, w_e_up, w_e_down, norm_f_g):
    bsz, seq, d = x.shape
    n = bsz * seq
    depth = w_in.shape[0]
    moe_delta = None
    for l in range(depth):
        caw = jnp.broadcast_to(conv_a_w[l][:, None, :], (CONV_A_K, SUBLANES, D_CONV))
        x = _mixer(x, moe_delta, l, _row(norm_mix_g[l]), w_in, caw, _row(conv_a_b[l]),
                   _row(ln_a_g[l]), _row(ln_a_b[l]), w_a_out, w_pool_grp, _row(pool_scale[l]),
                   conv_c_w[l], w_c_out, w_o)
        kt, v = _kv_proj(mem, _row(norm_mem_g[l]), w_xkv[l].astype(BF16))
        w_r, b_r = _router_params(w_rg[l], b_rg[l], w_re[l], b_re[l])
        x, hp, cls = _xattn(x, _row(norm_x_g[l]), w_xq[l].astype(BF16), kt, v,
                                  w_xo[l].astype(BF16), _row(norm_ffn_g[l]), w_r, b_r)
        qs, qy, tile_ea, tile_eb, n_used = _plan(cls.reshape(n // LANES, LANES))
        y = _experts(tile_ea, tile_eb, n_used, _scatter(qs, hp), w_e_gate, w_e_up, w_e_down, l)
        if l < depth - 1:
            moe_delta = _unsort(qy, y)
    return _gather(qy, x.reshape(n, d), y, _row(norm_f_g)).reshape(bsz, seq, d)
```

```python
import functools

import jax
import jax.numpy as jnp
from jax import lax
from jax.experimental import pallas as pl
from jax.experimental.pallas import tpu as pltpu

D_MODEL = 1024
D_CONV = D_MODEL // 2
D_POOL = D_MODEL // 2
D_SC = D_MODEL // 2
CONV_A_K = 31
SC_K = 3
POOL_WINDOWS = (2, 4, 8, 16)
POOL_GROUP_DIM = D_POOL // len(POOL_WINDOWS)
POOL_OUT_DIM = D_MODEL // len(POOL_WINDOWS)
N_MEM = 256
XATTN_HEADS = 4
XATTN_HEAD_DIM = D_MODEL // XATTN_HEADS
MOE_GROUPS = 4
EXPERTS_PER_GROUP = 4
N_EXPERTS = MOE_GROUPS * EXPERTS_PER_GROUP
D_EXPERT = D_MODEL // 4
EPS = 1e-6

OFF_A = 0
OFF_POOL = 2 * D_CONV
OFF_C = OFF_POOL + D_POOL
OFF_G = OFF_C + 3 * D_SC
D_IN_PROJ = OFF_G + 3 * D_MODEL

SUBLANES = 8
LANES = 128
VMEM_LIMIT_BYTES = 56 * 1024 * 1024

HIST_A = 32
HIST_POOL = 16
HIST_C = 8

SEQ_TILE = 512
XATTN_SEQ_TILE = 1024
CONV_ROW_CHUNK = 16
W_STEPS = 8
ROUTER_LANES = LANES

PAIRS_PER_GROUP = EXPERTS_PER_GROUP * (EXPERTS_PER_GROUP - 1) // 2
N_CLASSES = MOE_GROUPS * PAIRS_PER_GROUP
EXPERT_ROW_TILE = 128
EXPERT_STEP_TILES = 8
ROUTER_ROWS = 32
ROW_MOVE_TILE = 512
VMEM_LIMIT_RESIDENT_BYTES = 58 * 1024 * 1024
H_TILES = D_MODEL // 2 // LANES
ROW_TILES = H_TILES + 1

BF16 = jnp.bfloat16
F32 = jnp.float32


def _sorted_rows(n_tokens):
    return n_tokens + N_CLASSES * EXPERT_ROW_TILE


def _sigmoid(v):
    return 0.5 * jnp.tanh(0.5 * v) + 0.5


def _rmsnorm(xf, g):
    return xf * lax.rsqrt(jnp.mean(xf * xf, axis=-1, keepdims=True) + EPS) * g


def _dot(a, b):
    return jnp.dot(a, b, preferred_element_type=F32)


def _const_spec(shape):
    nd = len(shape)
    return pl.BlockSpec(shape, lambda *_: (0,) * nd, pipeline_mode=pl.Buffered(1))


def _weight_slab_spec(w, layer, axis=0):
    block = list(w.shape[1:])
    block[axis] //= W_STEPS
    return pl.BlockSpec(
        (None, *block),
        lambda i, *_: (layer,) + tuple(
            jnp.minimum(i, W_STEPS - 1) if a == axis else 0 for a in range(len(block))))


def _store_weight_slab(step, src_ref, dst_ref, axis=0):
    rows = src_ref.shape[axis]
    at = pl.ds(pl.multiple_of(step * rows, rows), rows)
    idx = tuple(at if a == axis else slice(None) for a in range(len(src_ref.shape)))
    dst_ref[idx] = src_ref[...].astype(BF16)


def _unpack_pair(q):
    return jnp.concatenate(
        [pltpu.unpack_elementwise(q, index=0, packed_dtype=BF16, unpacked_dtype=F32),
         pltpu.unpack_elementwise(q, index=1, packed_dtype=BF16, unpacked_dtype=F32)], axis=-1)


def _pack_pair(v):
    half = v.shape[-1] // 2
    return pltpu.pack_elementwise([v[:, :half], v[:, half:]], packed_dtype=BF16)


def _to_tile_rows(v):
    t, w = v.shape
    nj = w // LANES
    parts = [v[:, j * LANES:(j + 1) * LANES].reshape(t // SUBLANES, 1, SUBLANES, LANES)
             for j in range(nj)]
    return jnp.concatenate(parts, axis=1).reshape(nj * t, LANES)


def _from_tile_rows(v2, nj):
    t = v2.shape[0] // nj
    v4 = v2.reshape(t // SUBLANES, nj, SUBLANES, LANES)
    return [v4[:, j].reshape(t, LANES) for j in range(nj)]


def _tile_row_offset(r, nj):
    hi = jnp.floor(r / SUBLANES)
    return hi * (SUBLANES * nj) + (r - hi * SUBLANES)


def _mixer_kernel(x_ref, *refs, has_moe_delta, tiles_per_seq):
    y_ref, refs = (refs[0], refs[1:]) if has_moe_delta else (None, refs)
    (g_ref, w_in32, caw_ref, cab_ref, lng_ref, lnb_ref, w_a_out32, w_pool32, pool_scale_ref,
     ccw_ref, w_c_out32, w_o32, o_ref, abuf, ubuf, vbuf, actbuf, pbuf,
     w_in_ref, w_a_out_ref, w_pool_ref, w_c_out_ref, w_o_ref) = refs
    step = pl.program_id(0)

    @pl.when(step < W_STEPS)
    def _():
        for src, dst in ((w_in32, w_in_ref), (w_a_out32, w_a_out_ref),
                         (w_c_out32, w_c_out_ref), (w_o32, w_o_ref)):
            _store_weight_slab(step, src, dst)
        _store_weight_slab(step, w_pool32, w_pool_ref, axis=1)

    @pl.when(step >= W_STEPS)
    def _():
        _mix_tile((step - W_STEPS) % tiles_per_seq, x_ref, y_ref, g_ref, w_in_ref, caw_ref,
                  cab_ref, lng_ref, lnb_ref, w_a_out_ref, w_pool_ref, pool_scale_ref, ccw_ref,
                  w_c_out_ref, w_o_ref, o_ref, abuf, ubuf, vbuf, actbuf, pbuf)


def _mix_tile(s, x_ref, y_ref, g_ref, w_in_ref, caw_ref, cab_ref, lng_ref, lnb_ref, w_a_out_ref,
              w_pool_ref, pool_scale_ref, ccw_ref, w_c_out_ref, w_o_ref,
              o_ref, abuf, ubuf, vbuf, actbuf, pbuf):
    ts = x_ref.shape[1]

    @pl.when(s == 0)
    def _():
        abuf[0, 0:HIST_A, :] = jnp.zeros((HIST_A, D_CONV), F32)
        ubuf[0:HIST_POOL, :] = jnp.zeros((HIST_POOL, D_POOL), F32)
        vbuf[0:HIST_C, :] = jnp.zeros((HIST_C, D_SC), F32)

    x = x_ref[0]
    if y_ref is not None:
        x = x + _unpack_pair(jnp.concatenate(_from_tile_rows(y_ref[...], H_TILES), axis=-1))
    h = _rmsnorm(x, g_ref[...]).astype(BF16)

    def proj(off, width):
        return _dot(h, w_in_ref[:, off:off + width])

    pa = proj(OFF_A, 2 * D_CONV)
    abuf[0, HIST_A:HIST_A + ts, :] = pa[:, :D_CONV] * _sigmoid(pa[:, D_CONV:])
    shifted_rows = HIST_A + ts - SUBLANES
    for r in range(1, SUBLANES):
        abuf[r, 0:shifted_rows, :] = abuf[0, r:r + shifted_rows, :]
    chunk3 = (CONV_ROW_CHUNK // SUBLANES, SUBLANES, D_CONV)
    for c0 in range(0, ts, CONV_ROW_CHUNK):
        acc = jnp.broadcast_to(cab_ref[...], chunk3)
        for k in range(CONV_A_K):
            q, r = divmod(HIST_A - (CONV_A_K - 1) + k, SUBLANES)
            lo = c0 + q * SUBLANES
            acc = acc + caw_ref[k] * abuf[r, lo:lo + CONV_ROW_CHUNK, :].reshape(chunk3)
        acc = acc.reshape(CONV_ROW_CHUNK, D_CONV)
        mu = jnp.mean(acc, axis=-1, keepdims=True)
        cen = acc - mu
        var = jnp.mean(cen * cen, axis=-1, keepdims=True)
        ln = cen * lax.rsqrt(var + EPS) * lng_ref[...] + lnb_ref[...]
        actbuf[c0:c0 + CONV_ROW_CHUNK, :] = (ln * _sigmoid(ln)).astype(BF16)
    abuf[0, 0:HIST_A, :] = abuf[0, ts:ts + HIST_A, :]

    pbuf[...] = proj(OFF_POOL, D_IN_PROJ - OFF_POOL)

    def pcol(off, width):
        return pbuf[:, off - OFF_POOL:off - OFF_POOL + width]

    pu = pcol(OFF_POOL, D_POOL)
    ubuf[HIST_POOL:HIST_POOL + ts, :] = pu
    merged = _sigmoid(pcol(OFF_G, D_MODEL)) * _dot(actbuf[...], w_a_out_ref[...])

    t_glob = s * ts + lax.broadcasted_iota(jnp.int32, (ts, 1), 0)
    p_parts = []
    for i, w in enumerate(POOL_WINDOWS):
        c0, c1 = i * POOL_GROUP_DIM, (i + 1) * POOL_GROUP_DIM
        tok = pu[:, c0:c1]
        win = tok
        for j in range(1, w):
            win = win + ubuf[HIST_POOL - j:HIST_POOL - j + ts, c0:c1]
        cnt = jnp.minimum(t_glob + 1, w).astype(F32)
        pin = (win / cnt - tok).astype(BF16)
        p_parts.append(_dot(pin, w_pool_ref[i]))
    ubuf[0:HIST_POOL, :] = ubuf[ts:ts + HIST_POOL, :]
    merged = merged + _sigmoid(pcol(OFF_G + D_MODEL, D_MODEL)) * (jnp.concatenate(p_parts, axis=-1) * pool_scale_ref[...])

    c_b = pcol(OFF_C + D_SC, D_SC)
    v = pcol(OFF_C + 2 * D_SC, D_SC) * pcol(OFF_C, D_SC)
    vbuf[HIST_C:HIST_C + ts, :] = v
    conv_c = ccw_ref[SC_K - 1:SC_K, :] * v
    for k in range(SC_K - 1):
        off = HIST_C - (SC_K - 1) + k
        conv_c = conv_c + ccw_ref[k:k + 1, :] * vbuf[off:off + ts, :]
    vbuf[0:HIST_C, :] = vbuf[ts:ts + HIST_C, :]
    merged = merged + _sigmoid(pcol(OFF_G + 2 * D_MODEL, D_MODEL)) * _dot((c_b * conv_c).astype(BF16), w_c_out_ref[...])

    o_ref[0] = x + _dot(merged.astype(BF16), w_o_ref[...])


def _mixer(x, moe_delta, layer, g, w_in, caw, cab, lng, lnb, w_a_out, w_pool, pool_scale, ccw,
           w_c_out, w_o):
    b, s, d = x.shape
    ts = SEQ_TILE
    nt = s // ts

    def tile(i):
        return jnp.maximum(i - W_STEPS, 0)

    def small(c):
        return _const_spec(c.shape)

    def slab_spec(w, axis):
        return _weight_slab_spec(w, layer, axis)

    delta = () if moe_delta is None else (moe_delta,)
    delta_specs = [pl.BlockSpec((H_TILES * ts, LANES), lambda i: (tile(i), 0))] * len(delta)
    x_spec = pl.BlockSpec((1, ts, d), lambda i: (tile(i) // nt, tile(i) % nt, 0))
    return pl.pallas_call(
        functools.partial(_mixer_kernel, has_moe_delta=moe_delta is not None, tiles_per_seq=nt),
        out_shape=jax.ShapeDtypeStruct(x.shape, F32),
        grid=(W_STEPS + b * nt,),
        in_specs=[x_spec] + delta_specs
        + [small(g), slab_spec(w_in, 0), small(caw), small(cab), small(lng), small(lnb),
           slab_spec(w_a_out, 0), slab_spec(w_pool, 1), small(pool_scale), small(ccw),
           slab_spec(w_c_out, 0), slab_spec(w_o, 0)],
        out_specs=x_spec,
        scratch_shapes=[pltpu.VMEM((SUBLANES, HIST_A + ts, D_CONV), F32),
                        pltpu.VMEM((HIST_POOL + ts, D_POOL), F32),
                        pltpu.VMEM((HIST_C + ts, D_SC), F32),
                        pltpu.VMEM((ts, D_CONV), BF16),
                        pltpu.VMEM((ts, D_IN_PROJ - OFF_POOL), F32)]
        + [pltpu.VMEM(w.shape[1:], BF16) for w in (w_in, w_a_out, w_pool, w_c_out, w_o)],
        compiler_params=pltpu.CompilerParams(
            dimension_semantics=("arbitrary",), vmem_limit_bytes=VMEM_LIMIT_BYTES),
        name="mixer",
    )(x, *delta, g, w_in, caw, cab, lng, lnb, w_a_out, w_pool, pool_scale, ccw, w_c_out, w_o)


def _kv_kernel(mem_ref, g_ref, w_kv32, kt_ref, v_ref, w_kv_ref):
    step = pl.program_id(0)

    @pl.when(step < W_STEPS)
    def _():
        _store_weight_slab(step, w_kv32, w_kv_ref)

    @pl.when(step >= W_STEPS)
    def _():
        mn = _rmsnorm(mem_ref[0], g_ref[...]).astype(BF16)
        kv = _dot(mn, w_kv_ref[...])
        kt_ref[0] = kv[:, :D_MODEL].T.astype(BF16)
        v_ref[0] = kv[:, D_MODEL:].astype(BF16)


def _kv_proj(mem, g, w_kv, layer):
    b, m, d = mem.shape

    def row(i):
        return (jnp.maximum(i - W_STEPS, 0), 0, 0)

    return pl.pallas_call(
        _kv_kernel,
        out_shape=(jax.ShapeDtypeStruct((b, d, m), BF16), jax.ShapeDtypeStruct((b, m, d), BF16)),
        grid=(W_STEPS + b,),
        in_specs=[pl.BlockSpec((1, m, d), row), _const_spec(g.shape),
                  _weight_slab_spec(w_kv, layer)],
        out_specs=(pl.BlockSpec((1, d, m), row), pl.BlockSpec((1, m, d), row)),
        scratch_shapes=[pltpu.VMEM(w_kv.shape[1:], BF16)],
        compiler_params=pltpu.CompilerParams(
            dimension_semantics=("arbitrary",), vmem_limit_bytes=VMEM_LIMIT_BYTES),
        name="kv_proj",
    )(mem, g, w_kv)


def _route_rows(lt):
    def first_argmax(vals, vmax):
        idx = jnp.full(vmax.shape, len(vals) - 1, jnp.int32)
        for k in range(len(vals) - 2, -1, -1):
            idx = jnp.where(vals[k] == vmax, k, idx)
        return idx

    g = [lt[k:k + 1, :] for k in range(MOE_GROUPS)]
    gmax = functools.reduce(jnp.maximum, g)
    g_idx = first_argmax(g, gmax)
    g_val = 1.0 / sum(jnp.exp(v - gmax) for v in g)
    e = []
    for j in range(EXPERTS_PER_GROUP):
        ej = lt[MOE_GROUPS + j:MOE_GROUPS + j + 1, :]
        for grp in range(1, MOE_GROUPS):
            r = MOE_GROUPS + grp * EXPERTS_PER_GROUP + j
            ej = jnp.where(g_idx == grp, lt[r:r + 1, :], ej)
        e.append(ej)
    m1 = functools.reduce(jnp.maximum, e)
    i1 = first_argmax(e, m1)
    e2 = [jnp.where(i1 == j, -jnp.inf, e[j]) for j in range(EXPERTS_PER_GROUP)]
    m2 = functools.reduce(jnp.maximum, e2)
    i2 = first_argmax(e2, m2)
    esum = sum(jnp.exp(v - m1) for v in e)
    p1 = 1.0 / esum
    p2 = jnp.exp(m2 - m1) / esum
    w1 = g_val * (p1 / (p1 + p2))
    w2 = g_val * (p2 / (p1 + p2))
    lo = jnp.minimum(i1, i2)
    hi = jnp.maximum(i1, i2)
    pair = ((lo * (2 * EXPERTS_PER_GROUP - 1 - lo)) >> 1) + (hi - lo - 1)
    cls = g_idx * PAIRS_PER_GROUP + pair
    first_is_low = i1 < i2
    return cls, jnp.where(first_is_low, w1, w2), jnp.where(first_is_low, w2, w1)


def _xattn_kernel(x_ref, g_ref, w_q32, kt_ref, v_ref, w_o32, gf_ref, w_r_ref, b_r_ref,
                  o_ref, hp_ref, cls_ref, xn_prev, w_q_ref, w_o_ref):
    step = pl.program_id(0)

    @pl.when(step == 0)
    def _():
        xn_prev[...] = jnp.zeros(xn_prev.shape, F32)

    @pl.when(step < W_STEPS)
    def _():
        _store_weight_slab(step, w_q32, w_q_ref)
        _store_weight_slab(step, w_o32, w_o_ref)

    @pl.when(step >= W_STEPS)
    def _():
        _xattn_tile(x_ref, g_ref, w_q_ref, kt_ref, v_ref, w_o_ref, gf_ref, w_r_ref, b_r_ref,
                    o_ref, hp_ref, cls_ref, xn_prev)


def _xattn_tile(x_ref, g_ref, w_q_ref, kt_ref, v_ref, w_o_ref, gf_ref, w_r_ref, b_r_ref,
                o_ref, hp_ref, cls_ref, xn_prev):
    xp = xn_prev[...]
    x = x_ref[0]
    h = _rmsnorm(x, g_ref[...]).astype(BF16)
    q = (_dot(h, w_q_ref[...]) * (XATTN_HEAD_DIM ** -0.5)).astype(BF16)
    heads = []
    for hd in range(XATTN_HEADS):
        c0, c1 = hd * XATTN_HEAD_DIM, (hd + 1) * XATTN_HEAD_DIM
        sc = _dot(q[:, c0:c1], kt_ref[0, c0:c1, :])
        e = jnp.exp(sc - jnp.max(sc, axis=-1, keepdims=True))
        denom = jnp.sum(e, axis=-1, keepdims=True)
        heads.append(_dot(e.astype(BF16), v_ref[0, :, c0:c1]) / denom)
    o = jnp.concatenate(heads, axis=-1).astype(BF16)
    xn = x + _dot(o, w_o_ref[...])
    o_ref[0] = xn
    xn_prev[...] = xn

    hf = _rmsnorm(xp, gf_ref[...])
    h_hi = hf.astype(BF16)
    h_lo = (hf - h_hi.astype(F32)).astype(BF16)
    nt_dims = (((1,), (1,)), ((), ()))
    part = lax.dot_general(w_r_ref[...], h_hi, nt_dims, preferred_element_type=F32)
    lt = (part[:ROUTER_ROWS] + part[ROUTER_ROWS:]
          + lax.dot_general(w_r_ref[:ROUTER_ROWS, :], h_lo, nt_dims, preferred_element_type=F32)
          + b_r_ref[...])
    cls, wa, wb = _route_rows(lt)
    ts = cls.shape[1]
    for c in range(ts // LANES):
        cls_ref[0, c:c + 1, :] = cls[:, c * LANES:(c + 1) * LANES]
    row = lax.broadcasted_iota(jnp.int32, (LANES, ts), 0)

    def weight_lanes(w):
        w_hi = w.astype(BF16).astype(F32)
        return jnp.where(row == 0, w_hi, jnp.where(row == 1, w - w_hi, 0.0)).T

    wt_tile = pltpu.pack_elementwise([weight_lanes(wa), weight_lanes(wb)], packed_dtype=BF16)
    hp_ref[...] = _to_tile_rows(jnp.concatenate([_pack_pair(hf), wt_tile], axis=-1))


def _xattn(x, g, w_q, kt, v, w_o, g_ffn, w_r, b_r, layer):
    b, s, d = x.shape
    ts = XATTN_SEQ_TILE
    nt = s // ts
    n = b * s
    tiles = b * nt

    def attn_tile(step):
        return jnp.clip(step - W_STEPS, 0, tiles - 1)

    def route_tile(step):
        return jnp.maximum(step - W_STEPS - 1, 0)

    return pl.pallas_call(
        _xattn_kernel,
        out_shape=(jax.ShapeDtypeStruct(x.shape, F32),
                   jax.ShapeDtypeStruct((ROW_TILES * n, LANES), jnp.uint32),
                   jax.ShapeDtypeStruct((n // ts, ts // LANES, LANES), jnp.int32)),
        grid=(W_STEPS + tiles + 1,),
        in_specs=[pl.BlockSpec((1, ts, d), lambda i: (attn_tile(i) // nt, attn_tile(i) % nt, 0)),
                  _const_spec(g.shape), _weight_slab_spec(w_q, layer),
                  pl.BlockSpec((1, d, N_MEM), lambda i: (attn_tile(i) // nt, 0, 0)),
                  pl.BlockSpec((1, N_MEM, d), lambda i: (attn_tile(i) // nt, 0, 0)),
                  _weight_slab_spec(w_o, layer), _const_spec(g_ffn.shape),
                  _const_spec(w_r.shape), _const_spec(b_r.shape)],
        out_specs=(pl.BlockSpec((1, ts, d), lambda i: (attn_tile(i) // nt, attn_tile(i) % nt, 0)),
                   pl.BlockSpec((ROW_TILES * ts, LANES), lambda i: (route_tile(i), 0)),
                   pl.BlockSpec((1, ts // LANES, LANES), lambda i: (route_tile(i), 0, 0))),
        scratch_shapes=[pltpu.VMEM((ts, d), F32), pltpu.VMEM(w_q.shape[1:], BF16),
                        pltpu.VMEM(w_o.shape[1:], BF16)],
        compiler_params=pltpu.CompilerParams(
            dimension_semantics=("arbitrary",), vmem_limit_bytes=VMEM_LIMIT_BYTES),
        name="xattn",
    )(x, g, w_q, kt, v, w_o, g_ffn, w_r, b_r)


def _plan_kernel(cls_ref, qs_ref, qy_ref, ea_ref, eb_ref, nu_ref):
    cls = cls_ref[...]
    nr, nl = cls.shape
    r = lax.broadcasted_iota(jnp.int32, (nl, nl), 0)
    c = lax.broadcasted_iota(jnp.int32, (nl, nl), 1)
    lanes_before = (r < c).astype(BF16)
    rr = lax.broadcasted_iota(jnp.int32, (nr, nr), 0)
    rc = lax.broadcasted_iota(jnp.int32, (nr, nr), 1)
    rows_before = (rc < rr).astype(BF16)
    ones_l = jnp.ones((nl, nl), BF16)
    ones_r = jnp.ones((nr, nr), BF16)
    n_tab = ea_ref.shape[1]
    tile_i = lax.broadcasted_iota(jnp.int32, (SUBLANES, n_tab), 1).astype(F32)
    run = jnp.zeros((nr, nl), F32)
    pos = jnp.zeros((nr, nl), F32)
    tile_cls = jnp.zeros((SUBLANES, n_tab), F32)
    for k in range(N_CLASSES):
        m = cls == k
        mb = jnp.where(m, 1.0, 0.0).astype(BF16)
        rowtot = _dot(mb, ones_l).astype(BF16)
        rank = _dot(mb, lanes_before) + _dot(rows_before, rowtot)
        total = _dot(ones_r, rowtot)
        pos = jnp.where(m, run * EXPERT_ROW_TILE + rank, pos)
        run = run + jnp.floor((total + (EXPERT_ROW_TILE - 1)) / EXPERT_ROW_TILE)
        end = jnp.concatenate([run[:SUBLANES, :]] * (n_tab // nl), axis=1)
        tile_cls = tile_cls + jnp.where(end <= tile_i, 1.0, 0.0)
    qs_ref[...] = _tile_row_offset(pos, ROW_TILES).astype(jnp.int32)
    qy_ref[...] = _tile_row_offset(pos, H_TILES).astype(jnp.int32)
    tc = jnp.minimum(tile_cls, N_CLASSES - 1.0).astype(jnp.int32)
    grp = sum(jnp.where(tc >= g * PAIRS_PER_GROUP, 1, 0) for g in range(1, MOE_GROUPS))
    pair = tc - grp * PAIRS_PER_GROUP
    pair_lo = jnp.where(pair >= 3, 1, 0) + jnp.where(pair >= 5, 1, 0)
    pair_hi = pair + 1 - jnp.where(pair >= 5, 3, jnp.where(pair >= 3, 2, 0))
    ea_ref[...] = grp * EXPERTS_PER_GROUP + pair_lo
    eb_ref[...] = grp * EXPERTS_PER_GROUP + pair_hi
    nu_ref[...] = run[:SUBLANES, :].astype(jnp.int32)


def _plan(cls2):
    nr, nl = cls2.shape
    n_tiles = _sorted_rows(nr * nl) // EXPERT_ROW_TILE
    n_tab = -(-n_tiles // nl) * nl
    vmem = pl.BlockSpec(memory_space=pltpu.VMEM)
    qs, qy, ea, eb, nu = pl.pallas_call(
        _plan_kernel,
        out_shape=(jax.ShapeDtypeStruct((nr, nl), jnp.int32),
                   jax.ShapeDtypeStruct((nr, nl), jnp.int32),
                   jax.ShapeDtypeStruct((SUBLANES, n_tab), jnp.int32),
                   jax.ShapeDtypeStruct((SUBLANES, n_tab), jnp.int32),
                   jax.ShapeDtypeStruct((SUBLANES, nl), jnp.int32)),
        in_specs=[vmem],
        out_specs=(vmem, vmem, vmem, vmem, vmem),
        name="moe_plan",
    )(cls2)
    return (qs.reshape(nr * nl), qy.reshape(nr * nl),
            ea[0, :n_tiles], eb[0, :n_tiles], nu[0, :1])


def _scatter_kernel(qs_ref, hp_ref, s_ref):
    i = pl.program_id(0)
    tn = hp_ref.shape[0] // ROW_TILES
    group = SUBLANES * ROW_TILES

    @pl.when(i == 0)
    def _():
        zero_rows = _pack_pair(jnp.zeros((hp_ref.shape[0], 2 * LANES), F32))

        def fill(c, carry):
            start = pl.multiple_of(c * hp_ref.shape[0], SUBLANES)
            s_ref[pl.ds(start, hp_ref.shape[0]), :] = zero_rows
            return carry

        lax.fori_loop(0, s_ref.shape[0] // hp_ref.shape[0], fill, 0)

    def body(j, carry):
        src = pl.multiple_of(j * group, SUBLANES)
        for k in range(SUBLANES):
            q = qs_ref[i * tn + j * SUBLANES + k]
            s_ref[pl.ds(q, ROW_TILES, stride=SUBLANES), :] = (
                hp_ref[pl.ds(src + k, ROW_TILES, stride=SUBLANES), :])
        return carry

    lax.fori_loop(0, tn // SUBLANES, body, 0)


def _scatter(qs, hp):
    n = qs.shape[0]
    tn = ROW_MOVE_TILE
    rows = ROW_TILES * _sorted_rows(n)
    grid_spec = pltpu.PrefetchScalarGridSpec(
        num_scalar_prefetch=1,
        grid=(n // tn,),
        in_specs=[pl.BlockSpec((ROW_TILES * tn, LANES), lambda i, qs: (i, 0))],
        out_specs=pl.BlockSpec((rows, LANES), lambda i, qs: (0, 0), pipeline_mode=pl.Buffered(1)),
    )
    return pl.pallas_call(
        _scatter_kernel,
        out_shape=jax.ShapeDtypeStruct((rows, LANES), jnp.uint32),
        grid_spec=grid_spec,
        compiler_params=pltpu.CompilerParams(
            dimension_semantics=("arbitrary",), vmem_limit_bytes=VMEM_LIMIT_RESIDENT_BYTES),
        name="moe_scatter",
    )(qs, hp)


def _expert_kernel(ea_ref, eb_ref, nu_ref, s_ref, wg32_ref, wu32_ref, wd32_ref, y_ref,
                   wg_ref, wu_ref, wd_ref):
    step = pl.program_id(0)

    @pl.when(step < N_EXPERTS)
    def _():
        wg_ref[step] = wg32_ref[...].astype(BF16)
        wu_ref[step] = wu32_ref[...].astype(BF16)
        wd_ref[step] = wd32_ref[...].astype(BF16)

    first = (step - N_EXPERTS) * EXPERT_STEP_TILES
    rec = EXPERT_ROW_TILE * ROW_TILES
    out = EXPERT_ROW_TILE * H_TILES

    @pl.when((step >= N_EXPERTS) & (first < nu_ref[0]))
    def _():
        for j in range(EXPERT_STEP_TILES):
            t = first + j
            tiles = _from_tile_rows(s_ref[j * rec:(j + 1) * rec, :], ROW_TILES)
            hb = _unpack_pair(jnp.concatenate(tiles[:H_TILES], axis=-1)).astype(BF16)
            y = None
            for e_ref, idx in ((ea_ref, 0), (eb_ref, 1)):
                e = e_ref[t]
                wt = pltpu.unpack_elementwise(tiles[H_TILES], index=idx, packed_dtype=BF16,
                                              unpacked_dtype=F32)
                wt = wt[:, 0:1] + wt[:, 1:2]
                gt = _dot(hb, wg_ref[e])
                act = gt * _sigmoid(gt) * _dot(hb, wu_ref[e]) * wt
                part = _dot(act.astype(BF16), wd_ref[e])
                y = part if y is None else y + part
            y_ref[j * out:(j + 1) * out, :] = _to_tile_rows(_pack_pair(y))

    @pl.when((step >= N_EXPERTS) & (first >= nu_ref[0]))
    def _():
        y_ref[...] = _pack_pair(jnp.zeros((y_ref.shape[0], 2 * LANES), F32))


def _experts(tile_ea, tile_eb, n_used, s, wg, wu, wd, layer):
    rows = s.shape[0] // ROW_TILES
    tb = EXPERT_ROW_TILE * EXPERT_STEP_TILES

    def row_block(i, *_):
        return (jnp.maximum(i - N_EXPERTS, 0), 0)

    def expert_block(i, *_):
        return (layer, jnp.minimum(i, N_EXPERTS - 1), 0, 0)

    grid_spec = pltpu.PrefetchScalarGridSpec(
        num_scalar_prefetch=3,
        grid=(N_EXPERTS + rows // tb,),
        in_specs=[pl.BlockSpec((ROW_TILES * tb, LANES), row_block),
                  pl.BlockSpec((None, None) + wg.shape[2:], expert_block),
                  pl.BlockSpec((None, None) + wu.shape[2:], expert_block),
                  pl.BlockSpec((None, None) + wd.shape[2:], expert_block)],
        out_specs=pl.BlockSpec((H_TILES * tb, LANES), row_block),
        scratch_shapes=[pltpu.VMEM(wg.shape[1:], BF16), pltpu.VMEM(wu.shape[1:], BF16),
                        pltpu.VMEM(wd.shape[1:], BF16)],
    )
    return pl.pallas_call(
        _expert_kernel,
        out_shape=jax.ShapeDtypeStruct((H_TILES * rows, LANES), jnp.uint32),
        grid_spec=grid_spec,
        compiler_params=pltpu.CompilerParams(
            dimension_semantics=("arbitrary",), vmem_limit_bytes=VMEM_LIMIT_BYTES),
        name="moe_experts",
    )(tile_ea, tile_eb, n_used, s, wg, wu, wd)


def _unsort_kernel(qy_ref, y_ref, o_ref):
    i = pl.program_id(0)
    tn = o_ref.shape[0] // H_TILES
    group = SUBLANES * H_TILES

    def body(j, carry):
        dst = pl.multiple_of(j * group, SUBLANES)
        for k in range(SUBLANES):
            q = qy_ref[i * tn + j * SUBLANES + k]
            o_ref[pl.ds(dst + k, H_TILES, stride=SUBLANES), :] = (
                y_ref[pl.ds(q, H_TILES, stride=SUBLANES), :])
        return carry

    lax.fori_loop(0, tn // SUBLANES, body, 0)


def _unsort(qy, y):
    n = qy.shape[0]
    tn = ROW_MOVE_TILE
    grid_spec = pltpu.PrefetchScalarGridSpec(
        num_scalar_prefetch=1,
        grid=(n // tn,),
        in_specs=[pl.BlockSpec(y.shape, lambda i, qy: (0, 0), pipeline_mode=pl.Buffered(1))],
        out_specs=pl.BlockSpec((H_TILES * tn, LANES), lambda i, qy: (i, 0)),
    )
    return pl.pallas_call(
        _unsort_kernel,
        out_shape=jax.ShapeDtypeStruct((H_TILES * n, LANES), jnp.uint32),
        grid_spec=grid_spec,
        compiler_params=pltpu.CompilerParams(
            dimension_semantics=("arbitrary",), vmem_limit_bytes=VMEM_LIMIT_RESIDENT_BYTES),
        name="moe_unsort",
    )(qy, y)


def _gather_kernel(qy_ref, x_ref, y_ref, gn_ref, o_ref, rows_ref):
    i = pl.program_id(0)
    tn = x_ref.shape[0]
    group = SUBLANES * H_TILES

    def body(j, carry):
        dst = pl.multiple_of(j * group, SUBLANES)
        for k in range(SUBLANES):
            q = qy_ref[i * tn + j * SUBLANES + k]
            rows_ref[pl.ds(dst + k, H_TILES, stride=SUBLANES), :] = (
                y_ref[pl.ds(q, H_TILES, stride=SUBLANES), :])
        return carry

    lax.fori_loop(0, tn // SUBLANES, body, 0)
    packed = jnp.concatenate(_from_tile_rows(rows_ref[...], H_TILES), axis=-1)
    o_ref[...] = _rmsnorm(x_ref[...] + _unpack_pair(packed), gn_ref[...])


def _gather(qy, x2, y, g_final):
    n, d = x2.shape
    tn = ROW_MOVE_TILE
    grid_spec = pltpu.PrefetchScalarGridSpec(
        num_scalar_prefetch=1,
        grid=(n // tn,),
        in_specs=[pl.BlockSpec((tn, d), lambda i, qy: (i, 0)),
                  pl.BlockSpec(y.shape, lambda i, qy: (0, 0), pipeline_mode=pl.Buffered(1)),
                  pl.BlockSpec(g_final.shape, lambda i, qy: (0, 0), pipeline_mode=pl.Buffered(1))],
        out_specs=pl.BlockSpec((tn, d), lambda i, qy: (i, 0)),
        scratch_shapes=[pltpu.VMEM((H_TILES * tn, LANES), jnp.uint32)],
    )
    return pl.pallas_call(
        _gather_kernel,
        out_shape=jax.ShapeDtypeStruct(x2.shape, F32),
        grid_spec=grid_spec,
        compiler_params=pltpu.CompilerParams(
            dimension_semantics=("arbitrary",), vmem_limit_bytes=VMEM_LIMIT_RESIDENT_BYTES),
        name="moe_gather",
    )(qy, x2, y, g_final)


def _row(v):
    return v.reshape(1, -1)


def _router_params(w_rg, b_rg, w_re, b_re):
    pad = ROUTER_ROWS - MOE_GROUPS - N_EXPERTS
    wt = jnp.pad(jnp.concatenate([w_rg, w_re], axis=1).T, ((0, pad), (0, 0)))
    w_hi = wt.astype(BF16)
    w_lo = (wt - w_hi.astype(F32)).astype(BF16)
    b_col = jnp.pad(jnp.concatenate([b_rg, b_re]), (0, pad)).reshape(ROUTER_ROWS, 1)
    return jnp.concatenate([w_hi, w_lo], axis=0), b_col


def kernel(x, mem, norm_mix_g, w_in, conv_a_w, conv_a_b, ln_a_g, ln_a_b, w_a_out, w_pool_grp, pool_scale, conv_c_w, w_c_out, w_o, norm_x_g, norm_mem_g, w_xq, w_xkv, w_xo, norm_ffn_g, w_rg, b_rg, w_re, b_re, w_e_gate, w_e_up, w_e_down, norm_f_g):
    bsz, seq, d = x.shape
    n = bsz * seq
    depth = w_in.shape[0]
    moe_delta = None
    for l in range(depth):
        caw = jnp.broadcast_to(conv_a_w[l][:, None, :], (CONV_A_K, SUBLANES, D_CONV))
        x = _mixer(x, moe_delta, l, _row(norm_mix_g[l]), w_in, caw, _row(conv_a_b[l]),
                   _row(ln_a_g[l]), _row(ln_a_b[l]), w_a_out, w_pool_grp, _row(pool_scale[l]),
                   conv_c_w[l], w_c_out, w_o)
        kt, v = _kv_proj(mem, _row(norm_mem_g[l]), w_xkv, l)
        w_r, b_r = _router_params(w_rg[l], b_rg[l], w_re[l], b_re[l])
        x, hp, cls = _xattn(x, _row(norm_x_g[l]), w_xq, kt, v, w_xo, _row(norm_ffn_g[l]),
                            w_r, b_r, l)
        qs, qy, tile_ea, tile_eb, n_used = _plan(cls.reshape(n // LANES, LANES))
        y = _experts(tile_ea, tile_eb, n_used, _scatter(qs, hp), w_e_gate, w_e_up, w_e_down, l)
        if l < depth - 1:
            moe_delta = _unsort(qy, y)
    return _gather(qy, x.reshape(n, d), y, _row(norm_f_g)).reshape(bsz, seq, d)
```

```python
import functools

import jax
import jax.numpy as jnp
from jax import lax
from jax.experimental import pallas as pl
from jax.experimental.pallas import tpu as pltpu

D_MODEL = 1024
D_CONV = D_MODEL // 2
D_POOL = D_MODEL // 2
D_SC = D_MODEL // 2
CONV_A_K = 31
SC_K = 3
POOL_WINDOWS = (2, 4, 8, 16)
POOL_GROUP_DIM = D_POOL // len(POOL_WINDOWS)
POOL_OUT_DIM = D_MODEL // len(POOL_WINDOWS)
N_MEM = 256
XATTN_HEADS = 4
XATTN_HEAD_DIM = D_MODEL // XATTN_HEADS
MOE_GROUPS = 4
EXPERTS_PER_GROUP = 4
N_EXPERTS = MOE_GROUPS * EXPERTS_PER_GROUP
D_EXPERT = D_MODEL // 4
EPS = 1e-6

OFF_A = 0
OFF_POOL = 2 * D_CONV
OFF_C = OFF_POOL + D_POOL
OFF_G = OFF_C + 3 * D_SC
D_IN_PROJ = OFF_G + 3 * D_MODEL

SUBLANES = 8
LANES = 128
VMEM_LIMIT_BYTES = 56 * 1024 * 1024

HIST_A = 32
HIST_POOL = 16
HIST_C = 8

SEQ_TILE = 512
XATTN_SEQ_TILE = 1024
CONV_ROW_CHUNK = 16
W_STEPS = 8
ROUTER_LANES = LANES

PAIRS_PER_GROUP = EXPERTS_PER_GROUP * (EXPERTS_PER_GROUP - 1) // 2
N_CLASSES = MOE_GROUPS * PAIRS_PER_GROUP
EXPERT_ROW_TILE = 128
EXPERT_STEP_TILES = 8
ROUTER_ROWS = 32
ROW_MOVE_TILE = 512
VMEM_LIMIT_RESIDENT_BYTES = 58 * 1024 * 1024
H_TILES = D_MODEL // 2 // LANES
ROW_TILES = H_TILES + 1

BF16 = jnp.bfloat16
F32 = jnp.float32


def _sorted_rows(n_tokens):
    return n_tokens + N_CLASSES * EXPERT_ROW_TILE


def _sigmoid(v):
    return 0.5 * jnp.tanh(0.5 * v) + 0.5


def _rmsnorm(xf, g):
    return xf * lax.rsqrt(jnp.mean(xf * xf, axis=-1, keepdims=True) + EPS) * g


def _dot(a, b):
    return jnp.dot(a, b, preferred_element_type=F32)


def _const_spec(shape):
    nd = len(shape)
    return pl.BlockSpec(shape, lambda *_: (0,) * nd, pipeline_mode=pl.Buffered(1))


def _weight_slab_spec(w, layer, axis=0):
    block = list(w.shape[1:])
    block[axis] //= W_STEPS
    return pl.BlockSpec(
        (None, *block),
        lambda i, *_: (layer,) + tuple(
            jnp.minimum(i, W_STEPS - 1) if a == axis else 0 for a in range(len(block))))


def _store_weight_slab(step, src_ref, dst_ref, axis=0):
    rows = src_ref.shape[axis]
    at = pl.ds(pl.multiple_of(step * rows, rows), rows)
    idx = tuple(at if a == axis else slice(None) for a in range(len(src_ref.shape)))
    dst_ref[idx] = src_ref[...].astype(BF16)


def _unpack_pair(q):
    return jnp.concatenate(
        [pltpu.unpack_elementwise(q, index=0, packed_dtype=BF16, unpacked_dtype=F32),
         pltpu.unpack_elementwise(q, index=1, packed_dtype=BF16, unpacked_dtype=F32)], axis=-1)


def _pack_pair(v):
    half = v.shape[-1] // 2
    return pltpu.pack_elementwise([v[:, :half], v[:, half:]], packed_dtype=BF16)


def _to_tile_rows(v):
    t, w = v.shape
    nj = w // LANES
    parts = [v[:, j * LANES:(j + 1) * LANES].reshape(t // SUBLANES, 1, SUBLANES, LANES)
             for j in range(nj)]
    return jnp.concatenate(parts, axis=1).reshape(nj * t, LANES)


def _from_tile_rows(v2, nj):
    t = v2.shape[0] // nj
    v4 = v2.reshape(t // SUBLANES, nj, SUBLANES, LANES)
    return [v4[:, j].reshape(t, LANES) for j in range(nj)]


def _tile_row_offset(r, nj):
    hi = jnp.floor(r / SUBLANES)
    return hi * (SUBLANES * nj) + (r - hi * SUBLANES)


def _mixer_kernel(x_ref, *refs, has_moe_delta, tiles_per_seq):
    y_ref, refs = (refs[0], refs[1:]) if has_moe_delta else (None, refs)
    (g_ref, w_in32, caw_ref, cab_ref, lng_ref, lnb_ref, w_a_out32, w_pool32, pool_scale_ref,
     ccw_ref, w_c_out32, w_o32, o_ref, abuf, ubuf, vbuf, actbuf, pbuf,
     w_in_ref, w_a_out_ref, w_pool_ref, w_c_out_ref, w_o_ref) = refs
    step = pl.program_id(0)

    @pl.when(step < W_STEPS)
    def _():
        for src, dst in ((w_in32, w_in_ref), (w_a_out32, w_a_out_ref),
                         (w_c_out32, w_c_out_ref), (w_o32, w_o_ref)):
            _store_weight_slab(step, src, dst)
        _store_weight_slab(step, w_pool32, w_pool_ref, axis=1)

    @pl.when(step >= W_STEPS)
    def _():
        _mix_tile((step - W_STEPS) % tiles_per_seq, x_ref, y_ref, g_ref, w_in_ref, caw_ref,
                  cab_ref, lng_ref, lnb_ref, w_a_out_ref, w_pool_ref, pool_scale_ref, ccw_ref,
                  w_c_out_ref, w_o_ref, o_ref, abuf, ubuf, vbuf, actbuf, pbuf)


def _mix_tile(s, x_ref, y_ref, g_ref, w_in_ref, caw_ref, cab_ref, lng_ref, lnb_ref, w_a_out_ref,
              w_pool_ref, pool_scale_ref, ccw_ref, w_c_out_ref, w_o_ref,
              o_ref, abuf, ubuf, vbuf, actbuf, pbuf):
    ts = x_ref.shape[1]

    @pl.when(s == 0)
    def _():
        abuf[0, 0:HIST_A, :] = jnp.zeros((HIST_A, D_CONV), F32)
        ubuf[0:HIST_POOL, :] = jnp.zeros((HIST_POOL, D_POOL), F32)
        vbuf[0:HIST_C, :] = jnp.zeros((HIST_C, D_SC), F32)

    x = x_ref[0]
    if y_ref is not None:
        x = x + _unpack_pair(jnp.concatenate(_from_tile_rows(y_ref[...], H_TILES), axis=-1))
    h = _rmsnorm(x, g_ref[...]).astype(BF16)

    def proj(off, width):
        return _dot(h, w_in_ref[:, off:off + width])

    pa = proj(OFF_A, 2 * D_CONV)
    abuf[0, HIST_A:HIST_A + ts, :] = pa[:, :D_CONV] * _sigmoid(pa[:, D_CONV:])
    shifted_rows = HIST_A + ts - SUBLANES
    for r in range(1, SUBLANES):
        abuf[r, 0:shifted_rows, :] = abuf[0, r:r + shifted_rows, :]
    chunk3 = (CONV_ROW_CHUNK // SUBLANES, SUBLANES, D_CONV)
    for c0 in range(0, ts, CONV_ROW_CHUNK):
        acc = jnp.broadcast_to(cab_ref[...], chunk3)
        for k in range(CONV_A_K):
            q, r = divmod(HIST_A - (CONV_A_K - 1) + k, SUBLANES)
            lo = c0 + q * SUBLANES
            acc = acc + caw_ref[k] * abuf[r, lo:lo + CONV_ROW_CHUNK, :].reshape(chunk3)
        acc = acc.reshape(CONV_ROW_CHUNK, D_CONV)
        mu = jnp.mean(acc, axis=-1, keepdims=True)
        cen = acc - mu
        var = jnp.mean(cen * cen, axis=-1, keepdims=True)
        ln = cen * lax.rsqrt(var + EPS) * lng_ref[...] + lnb_ref[...]
        actbuf[c0:c0 + CONV_ROW_CHUNK, :] = (ln * _sigmoid(ln)).astype(BF16)
    abuf[0, 0:HIST_A, :] = abuf[0, ts:ts + HIST_A, :]

    pbuf[...] = proj(OFF_POOL, D_IN_PROJ - OFF_POOL)

    def pcol(off, width):
        return pbuf[:, off - OFF_POOL:off - OFF_POOL + width]

    pu = pcol(OFF_POOL, D_POOL)
    ubuf[HIST_POOL:HIST_POOL + ts, :] = pu
    merged = _sigmoid(pcol(OFF_G, D_MODEL)) * _dot(actbuf[...], w_a_out_ref[...])

    t_glob = s * ts + lax.broadcasted_iota(jnp.int32, (ts, 1), 0)
    p_parts = []
    for i, w in enumerate(POOL_WINDOWS):
        c0, c1 = i * POOL_GROUP_DIM, (i + 1) * POOL_GROUP_DIM
        tok = pu[:, c0:c1]
        win = ubuf[:, c0:c1]
        shift = 1
        while shift < w:
            win = win + pltpu.roll(win, shift, 0)
            shift *= 2
        win = win[HIST_POOL:]
        cnt = jnp.minimum(t_glob + 1, w).astype(F32)
        pin = (win / cnt - tok).astype(BF16)
        p_parts.append(_dot(pin, w_pool_ref[i]))
    ubuf[0:HIST_POOL, :] = ubuf[ts:ts + HIST_POOL, :]
    merged = merged + _sigmoid(pcol(OFF_G + D_MODEL, D_MODEL)) * (jnp.concatenate(p_parts, axis=-1) * pool_scale_ref[...])

    c_b = pcol(OFF_C + D_SC, D_SC)
    v = pcol(OFF_C + 2 * D_SC, D_SC) * pcol(OFF_C, D_SC)
    vbuf[HIST_C:HIST_C + ts, :] = v
    conv_c = ccw_ref[SC_K - 1:SC_K, :] * v
    for k in range(SC_K - 1):
        off = HIST_C - (SC_K - 1) + k
        conv_c = conv_c + ccw_ref[k:k + 1, :] * vbuf[off:off + ts, :]
    vbuf[0:HIST_C, :] = vbuf[ts:ts + HIST_C, :]
    merged = merged + _sigmoid(pcol(OFF_G + 2 * D_MODEL, D_MODEL)) * _dot((c_b * conv_c).astype(BF16), w_c_out_ref[...])

    o_ref[0] = x + _dot(merged.astype(BF16), w_o_ref[...])


def _mixer(x, moe_delta, layer, g, w_in, caw, cab, lng, lnb, w_a_out, w_pool, pool_scale, ccw,
           w_c_out, w_o):
    b, s, d = x.shape
    ts = SEQ_TILE
    nt = s // ts

    def tile(i):
        return jnp.maximum(i - W_STEPS, 0)

    def small(c):
        return _const_spec(c.shape)

    def slab_spec(w, axis):
        return _weight_slab_spec(w, layer, axis)

    delta = () if moe_delta is None else (moe_delta,)
    delta_specs = [pl.BlockSpec((H_TILES * ts, LANES), lambda i: (tile(i), 0))] * len(delta)
    x_spec = pl.BlockSpec((1, ts, d), lambda i: (tile(i) // nt, tile(i) % nt, 0))
    return pl.pallas_call(
        functools.partial(_mixer_kernel, has_moe_delta=moe_delta is not None, tiles_per_seq=nt),
        out_shape=jax.ShapeDtypeStruct(x.shape, F32),
        grid=(W_STEPS + b * nt,),
        in_specs=[x_spec] + delta_specs
        + [small(g), slab_spec(w_in, 0), small(caw), small(cab), small(lng), small(lnb),
           slab_spec(w_a_out, 0), slab_spec(w_pool, 1), small(pool_scale), small(ccw),
           slab_spec(w_c_out, 0), slab_spec(w_o, 0)],
        out_specs=x_spec,
        scratch_shapes=[pltpu.VMEM((SUBLANES, HIST_A + ts, D_CONV), F32),
                        pltpu.VMEM((HIST_POOL + ts, D_POOL), F32),
                        pltpu.VMEM((HIST_C + ts, D_SC), F32),
                        pltpu.VMEM((ts, D_CONV), BF16),
                        pltpu.VMEM((ts, D_IN_PROJ - OFF_POOL), F32)]
        + [pltpu.VMEM(w.shape[1:], BF16) for w in (w_in, w_a_out, w_pool, w_c_out, w_o)],
        compiler_params=pltpu.CompilerParams(
            dimension_semantics=("arbitrary",), vmem_limit_bytes=VMEM_LIMIT_BYTES),
        name="mixer",
    )(x, *delta, g, w_in, caw, cab, lng, lnb, w_a_out, w_pool, pool_scale, ccw, w_c_out, w_o)


def _kv_kernel(mem_ref, g_ref, w_kv32, kt_ref, v_ref, w_kv_ref):
    step = pl.program_id(0)

    @pl.when(step < W_STEPS)
    def _():
        _store_weight_slab(step, w_kv32, w_kv_ref)

    @pl.when(step >= W_STEPS)
    def _():
        mn = _rmsnorm(mem_ref[0], g_ref[...]).astype(BF16)
        kv = _dot(mn, w_kv_ref[...])
        kt_ref[0] = kv[:, :D_MODEL].T.astype(BF16)
        v_ref[0] = kv[:, D_MODEL:].astype(BF16)


def _kv_proj(mem, g, w_kv, layer):
    b, m, d = mem.shape

    def row(i):
        return (jnp.maximum(i - W_STEPS, 0), 0, 0)

    return pl.pallas_call(
        _kv_kernel,
        out_shape=(jax.ShapeDtypeStruct((b, d, m), BF16), jax.ShapeDtypeStruct((b, m, d), BF16)),
        grid=(W_STEPS + b,),
        in_specs=[pl.BlockSpec((1, m, d), row), _const_spec(g.shape),
                  _weight_slab_spec(w_kv, layer)],
        out_specs=(pl.BlockSpec((1, d, m), row), pl.BlockSpec((1, m, d), row)),
        scratch_shapes=[pltpu.VMEM(w_kv.shape[1:], BF16)],
        compiler_params=pltpu.CompilerParams(
            dimension_semantics=("arbitrary",), vmem_limit_bytes=VMEM_LIMIT_BYTES),
        name="kv_proj",
    )(mem, g, w_kv)


def _route_rows(lt):
    def first_argmax(vals, vmax):
        idx = jnp.full(vmax.shape, len(vals) - 1, jnp.int32)
        for k in range(len(vals) - 2, -1, -1):
            idx = jnp.where(vals[k] == vmax, k, idx)
        return idx

    g = [lt[k:k + 1, :] for k in range(MOE_GROUPS)]
    gmax = functools.reduce(jnp.maximum, g)
    g_idx = first_argmax(g, gmax)
    g_val = 1.0 / sum(jnp.exp(v - gmax) for v in g)
    e = []
    for j in range(EXPERTS_PER_GROUP):
        ej = lt[MOE_GROUPS + j:MOE_GROUPS + j + 1, :]
        for grp in range(1, MOE_GROUPS):
            r = MOE_GROUPS + grp * EXPERTS_PER_GROUP + j
            ej = jnp.where(g_idx == grp, lt[r:r + 1, :], ej)
        e.append(ej)
    m1 = functools.reduce(jnp.maximum, e)
    i1 = first_argmax(e, m1)
    e2 = [jnp.where(i1 == j, -jnp.inf, e[j]) for j in range(EXPERTS_PER_GROUP)]
    m2 = functools.reduce(jnp.maximum, e2)
    i2 = first_argmax(e2, m2)
    esum = sum(jnp.exp(v - m1) for v in e)
    p1 = 1.0 / esum
    p2 = jnp.exp(m2 - m1) / esum
    w1 = g_val * (p1 / (p1 + p2))
    w2 = g_val * (p2 / (p1 + p2))
    lo = jnp.minimum(i1, i2)
    hi = jnp.maximum(i1, i2)
    pair = ((lo * (2 * EXPERTS_PER_GROUP - 1 - lo)) >> 1) + (hi - lo - 1)
    cls = g_idx * PAIRS_PER_GROUP + pair
    first_is_low = i1 < i2
    return cls, jnp.where(first_is_low, w1, w2), jnp.where(first_is_low, w2, w1)


def _xattn_kernel(x_ref, g_ref, w_q32, kt_ref, v_ref, w_o32, gf_ref, w_r_ref, b_r_ref,
                  o_ref, hp_ref, cls_ref, xn_prev, w_q_ref, w_o_ref):
    step = pl.program_id(0)

    @pl.when(step == 0)
    def _():
        xn_prev[...] = jnp.zeros(xn_prev.shape, F32)

    @pl.when(step < W_STEPS)
    def _():
        _store_weight_slab(step, w_q32, w_q_ref)
        _store_weight_slab(step, w_o32, w_o_ref)

    @pl.when(step >= W_STEPS)
    def _():
        _xattn_tile(x_ref, g_ref, w_q_ref, kt_ref, v_ref, w_o_ref, gf_ref, w_r_ref, b_r_ref,
                    o_ref, hp_ref, cls_ref, xn_prev)


def _xattn_tile(x_ref, g_ref, w_q_ref, kt_ref, v_ref, w_o_ref, gf_ref, w_r_ref, b_r_ref,
                o_ref, hp_ref, cls_ref, xn_prev):
    xp = xn_prev[...]
    x = x_ref[0]
    h = _rmsnorm(x, g_ref[...]).astype(BF16)
    q = (_dot(h, w_q_ref[...]) * (XATTN_HEAD_DIM ** -0.5)).astype(BF16)
    heads = []
    for hd in range(XATTN_HEADS):
        c0, c1 = hd * XATTN_HEAD_DIM, (hd + 1) * XATTN_HEAD_DIM
        sc = _dot(q[:, c0:c1], kt_ref[0, c0:c1, :])
        e = jnp.exp(sc - jnp.max(sc, axis=-1, keepdims=True))
        denom = jnp.sum(e, axis=-1, keepdims=True)
        heads.append(_dot(e.astype(BF16), v_ref[0, :, c0:c1]) / denom)
    o = jnp.concatenate(heads, axis=-1).astype(BF16)
    xn = x + _dot(o, w_o_ref[...])
    o_ref[0] = xn
    xn_prev[...] = xn

    hf = _rmsnorm(xp, gf_ref[...])
    h_hi = hf.astype(BF16)
    h_lo = (hf - h_hi.astype(F32)).astype(BF16)
    nt_dims = (((1,), (1,)), ((), ()))
    part = lax.dot_general(w_r_ref[...], h_hi, nt_dims, preferred_element_type=F32)
    lt = (part[:ROUTER_ROWS] + part[ROUTER_ROWS:]
          + lax.dot_general(w_r_ref[:ROUTER_ROWS, :], h_lo, nt_dims, preferred_element_type=F32)
          + b_r_ref[...])
    cls, wa, wb = _route_rows(lt)
    ts = cls.shape[1]
    for c in range(ts // LANES):
        cls_ref[0, c:c + 1, :] = cls[:, c * LANES:(c + 1) * LANES]
    row = lax.broadcasted_iota(jnp.int32, (LANES, ts), 0)

    def weight_lanes(w):
        w_hi = w.astype(BF16).astype(F32)
        return jnp.where(row == 0, w_hi, jnp.where(row == 1, w - w_hi, 0.0)).T

    wt_tile = pltpu.pack_elementwise([weight_lanes(wa), weight_lanes(wb)], packed_dtype=BF16)
    hp_ref[...] = _to_tile_rows(jnp.concatenate([_pack_pair(hf), wt_tile], axis=-1))


def _xattn(x, g, w_q, kt, v, w_o, g_ffn, w_r, b_r, layer):
    b, s, d = x.shape
    ts = XATTN_SEQ_TILE
    nt = s // ts
    n = b * s
    tiles = b * nt

    def attn_tile(step):
        return jnp.clip(step - W_STEPS, 0, tiles - 1)

    def route_tile(step):
        return jnp.maximum(step - W_STEPS - 1, 0)

    return pl.pallas_call(
        _xattn_kernel,
        out_shape=(jax.ShapeDtypeStruct(x.shape, F32),
                   jax.ShapeDtypeStruct((ROW_TILES * n, LANES), jnp.uint32),
                   jax.ShapeDtypeStruct((n // ts, ts // LANES, LANES), jnp.int32)),
        grid=(W_STEPS + tiles + 1,),
        in_specs=[pl.BlockSpec((1, ts, d), lambda i: (attn_tile(i) // nt, attn_tile(i) % nt, 0)),
                  _const_spec(g.shape), _weight_slab_spec(w_q, layer),
                  pl.BlockSpec((1, d, N_MEM), lambda i: (attn_tile(i) // nt, 0, 0)),
                  pl.BlockSpec((1, N_MEM, d), lambda i: (attn_tile(i) // nt, 0, 0)),
                  _weight_slab_spec(w_o, layer), _const_spec(g_ffn.shape),
                  _const_spec(w_r.shape), _const_spec(b_r.shape)],
        out_specs=(pl.BlockSpec((1, ts, d), lambda i: (attn_tile(i) // nt, attn_tile(i) % nt, 0)),
                   pl.BlockSpec((ROW_TILES * ts, LANES), lambda i: (route_tile(i), 0)),
                   pl.BlockSpec((1, ts // LANES, LANES), lambda i: (route_tile(i), 0, 0))),
        scratch_shapes=[pltpu.VMEM((ts, d), F32), pltpu.VMEM(w_q.shape[1:], BF16),
                        pltpu.VMEM(w_o.shape[1:], BF16)],
        compiler_params=pltpu.CompilerParams(
            dimension_semantics=("arbitrary",), vmem_limit_bytes=VMEM_LIMIT_BYTES),
        name="xattn",
    )(x, g, w_q, kt, v, w_o, g_ffn, w_r, b_r)


def _plan_kernel(cls_ref, qs_ref, qy_ref, ea_ref, eb_ref, nu_ref):
    cls = cls_ref[...]
    nr, nl = cls.shape
    r = lax.broadcasted_iota(jnp.int32, (nl, nl), 0)
    c = lax.broadcasted_iota(jnp.int32, (nl, nl), 1)
    lanes_before = (r < c).astype(BF16)
    rr = lax.broadcasted_iota(jnp.int32, (nr, nr), 0)
    rc = lax.broadcasted_iota(jnp.int32, (nr, nr), 1)
    rows_before = (rc < rr).astype(BF16)
    ones_l = jnp.ones((nl, nl), BF16)
    ones_r = jnp.ones((nr, nr), BF16)
    n_tab = ea_ref.shape[1]
    tile_i = lax.broadcasted_iota(jnp.int32, (SUBLANES, n_tab), 1).astype(F32)
    run = jnp.zeros((nr, nl), F32)
    pos = jnp.zeros((nr, nl), F32)
    tile_cls = jnp.zeros((SUBLANES, n_tab), F32)
    for k in range(N_CLASSES):
        m = cls == k
        mb = jnp.where(m, 1.0, 0.0).astype(BF16)
        rowtot = _dot(mb, ones_l).astype(BF16)
        rank = _dot(mb, lanes_before) + _dot(rows_before, rowtot)
        total = _dot(ones_r, rowtot)
        pos = jnp.where(m, run * EXPERT_ROW_TILE + rank, pos)
        run = run + jnp.floor((total + (EXPERT_ROW_TILE - 1)) / EXPERT_ROW_TILE)
        end = jnp.concatenate([run[:SUBLANES, :]] * (n_tab // nl), axis=1)
        tile_cls = tile_cls + jnp.where(end <= tile_i, 1.0, 0.0)
    qs_ref[...] = _tile_row_offset(pos, ROW_TILES).astype(jnp.int32)
    qy_ref[...] = _tile_row_offset(pos, H_TILES).astype(jnp.int32)
    tc = jnp.minimum(tile_cls, N_CLASSES - 1.0).astype(jnp.int32)
    grp = sum(jnp.where(tc >= g * PAIRS_PER_GROUP, 1, 0) for g in range(1, MOE_GROUPS))
    pair = tc - grp * PAIRS_PER_GROUP
    pair_lo = jnp.where(pair >= 3, 1, 0) + jnp.where(pair >= 5, 1, 0)
    pair_hi = pair + 1 - jnp.where(pair >= 5, 3, jnp.where(pair >= 3, 2, 0))
    ea_ref[...] = grp * EXPERTS_PER_GROUP + pair_lo
    eb_ref[...] = grp * EXPERTS_PER_GROUP + pair_hi
    nu_ref[...] = run[:SUBLANES, :].astype(jnp.int32)


def _plan(cls2):
    nr, nl = cls2.shape
    n_tiles = _sorted_rows(nr * nl) // EXPERT_ROW_TILE
    n_tab = -(-n_tiles // nl) * nl
    vmem = pl.BlockSpec(memory_space=pltpu.VMEM)
    qs, qy, ea, eb, nu = pl.pallas_call(
        _plan_kernel,
        out_shape=(jax.ShapeDtypeStruct((nr, nl), jnp.int32),
                   jax.ShapeDtypeStruct((nr, nl), jnp.int32),
                   jax.ShapeDtypeStruct((SUBLANES, n_tab), jnp.int32),
                   jax.ShapeDtypeStruct((SUBLANES, n_tab), jnp.int32),
                   jax.ShapeDtypeStruct((SUBLANES, nl), jnp.int32)),
        in_specs=[vmem],
        out_specs=(vmem, vmem, vmem, vmem, vmem),
        name="moe_plan",
    )(cls2)
    return (qs.reshape(nr * nl), qy.reshape(nr * nl),
            ea[0, :n_tiles], eb[0, :n_tiles], nu[0, :1])


def _scatter_kernel(qs_ref, hp_ref, s_ref):
    i = pl.program_id(0)
    tn = hp_ref.shape[0] // ROW_TILES
    group = SUBLANES * ROW_TILES

    @pl.when(i == 0)
    def _():
        zero_rows = _pack_pair(jnp.zeros((hp_ref.shape[0], 2 * LANES), F32))

        def fill(c, carry):
            start = pl.multiple_of(c * hp_ref.shape[0], SUBLANES)
            s_ref[pl.ds(start, hp_ref.shape[0]), :] = zero_rows
            return carry

        lax.fori_loop(0, s_ref.shape[0] // hp_ref.shape[0], fill, 0)

    def body(j, carry):
        src = pl.multiple_of(j * group, SUBLANES)
        for k in range(SUBLANES):
            q = qs_ref[i * tn + j * SUBLANES + k]
            s_ref[pl.ds(q, ROW_TILES, stride=SUBLANES), :] = (
                hp_ref[pl.ds(src + k, ROW_TILES, stride=SUBLANES), :])
        return carry

    lax.fori_loop(0, tn // SUBLANES, body, 0)


def _scatter(qs, hp):
    n = qs.shape[0]
    tn = ROW_MOVE_TILE
    rows = ROW_TILES * _sorted_rows(n)
    grid_spec = pltpu.PrefetchScalarGridSpec(
        num_scalar_prefetch=1,
        grid=(n // tn,),
        in_specs=[pl.BlockSpec((ROW_TILES * tn, LANES), lambda i, qs: (i, 0))],
        out_specs=pl.BlockSpec((rows, LANES), lambda i, qs: (0, 0), pipeline_mode=pl.Buffered(1)),
    )
    return pl.pallas_call(
        _scatter_kernel,
        out_shape=jax.ShapeDtypeStruct((rows, LANES), jnp.uint32),
        grid_spec=grid_spec,
        compiler_params=pltpu.CompilerParams(
            dimension_semantics=("arbitrary",), vmem_limit_bytes=VMEM_LIMIT_RESIDENT_BYTES),
        name="moe_scatter",
    )(qs, hp)


def _expert_kernel(ea_ref, eb_ref, nu_ref, s_ref, wg32_ref, wu32_ref, wd32_ref, y_ref,
                   wg_ref, wu_ref, wd_ref):
    step = pl.program_id(0)

    @pl.when(step < N_EXPERTS)
    def _():
        wg_ref[step] = wg32_ref[...].astype(BF16)
        wu_ref[step] = wu32_ref[...].astype(BF16)
        wd_ref[step] = wd32_ref[...].astype(BF16)

    first = (step - N_EXPERTS) * EXPERT_STEP_TILES
    rec = EXPERT_ROW_TILE * ROW_TILES
    out = EXPERT_ROW_TILE * H_TILES

    @pl.when((step >= N_EXPERTS) & (first < nu_ref[0]))
    def _():
        acts = []
        for j in range(EXPERT_STEP_TILES):
            t = first + j
            tiles = _from_tile_rows(s_ref[j * rec:(j + 1) * rec, :], ROW_TILES)
            hb = _unpack_pair(jnp.concatenate(tiles[:H_TILES], axis=-1)).astype(BF16)
            for e_ref, idx in ((ea_ref, 0), (eb_ref, 1)):
                e = e_ref[t]
                wt = pltpu.unpack_elementwise(tiles[H_TILES], index=idx, packed_dtype=BF16,
                                              unpacked_dtype=F32)
                wt = wt[:, 0:1] + wt[:, 1:2]
                gt = _dot(hb, wg_ref[e])
                acts.append((gt * _sigmoid(gt) * _dot(hb, wu_ref[e]) * wt).astype(BF16))
        for j in range(EXPERT_STEP_TILES):
            t = first + j
            y = (_dot(acts[2 * j], wd_ref[ea_ref[t]])
                 + _dot(acts[2 * j + 1], wd_ref[eb_ref[t]]))
            y_ref[j * out:(j + 1) * out, :] = _to_tile_rows(_pack_pair(y))

    @pl.when((step >= N_EXPERTS) & (first >= nu_ref[0]))
    def _():
        y_ref[...] = _pack_pair(jnp.zeros((y_ref.shape[0], 2 * LANES), F32))


def _experts(tile_ea, tile_eb, n_used, s, wg, wu, wd, layer):
    rows = s.shape[0] // ROW_TILES
    tb = EXPERT_ROW_TILE * EXPERT_STEP_TILES

    def row_block(i, *_):
        return (jnp.maximum(i - N_EXPERTS, 0), 0)

    def expert_block(i, *_):
        return (layer, jnp.minimum(i, N_EXPERTS - 1), 0, 0)

    grid_spec = pltpu.PrefetchScalarGridSpec(
        num_scalar_prefetch=3,
        grid=(N_EXPERTS + rows // tb,),
        in_specs=[pl.BlockSpec((ROW_TILES * tb, LANES), row_block),
                  pl.BlockSpec((None, None) + wg.shape[2:], expert_block),
                  pl.BlockSpec((None, None) + wu.shape[2:], expert_block),
                  pl.BlockSpec((None, None) + wd.shape[2:], expert_block)],
        out_specs=pl.BlockSpec((H_TILES * tb, LANES), row_block),
        scratch_shapes=[pltpu.VMEM(wg.shape[1:], BF16), pltpu.VMEM(wu.shape[1:], BF16),
                        pltpu.VMEM(wd.shape[1:], BF16)],
    )
    return pl.pallas_call(
        _expert_kernel,
        out_shape=jax.ShapeDtypeStruct((H_TILES * rows, LANES), jnp.uint32),
        grid_spec=grid_spec,
        compiler_params=pltpu.CompilerParams(
            dimension_semantics=("arbitrary",), vmem_limit_bytes=VMEM_LIMIT_BYTES),
        name="moe_experts",
    )(tile_ea, tile_eb, n_used, s, wg, wu, wd)


def _unsort_kernel(qy_ref, y_ref, o_ref):
    i = pl.program_id(0)
    tn = o_ref.shape[0] // H_TILES
    group = SUBLANES * H_TILES

    def body(j, carry):
        dst = pl.multiple_of(j * group, SUBLANES)
        for k in range(SUBLANES):
            q = qy_ref[i * tn + j * SUBLANES + k]
            o_ref[pl.ds(dst + k, H_TILES, stride=SUBLANES), :] = (
                y_ref[pl.ds(q, H_TILES, stride=SUBLANES), :])
        return carry

    lax.fori_loop(0, tn // SUBLANES, body, 0)


def _unsort(qy, y):
    n = qy.shape[0]
    tn = ROW_MOVE_TILE
    grid_spec = pltpu.PrefetchScalarGridSpec(
        num_scalar_prefetch=1,
        grid=(n // tn,),
        in_specs=[pl.BlockSpec(y.shape, lambda i, qy: (0, 0), pipeline_mode=pl.Buffered(1))],
        out_specs=pl.BlockSpec((H_TILES * tn, LANES), lambda i, qy: (i, 0)),
    )
    return pl.pallas_call(
        _unsort_kernel,
        out_shape=jax.ShapeDtypeStruct((H_TILES * n, LANES), jnp.uint32),
        grid_spec=grid_spec,
        compiler_params=pltpu.CompilerParams(
            dimension_semantics=("arbitrary",), vmem_limit_bytes=VMEM_LIMIT_RESIDENT_BYTES),
        name="moe_unsort",
    )(qy, y)


def _gather_kernel(qy_ref, x_ref, y_ref, gn_ref, o_ref, rows_ref):
    i = pl.program_id(0)
    tn = x_ref.shape[0]
    group = SUBLANES * H_TILES

    def body(j, carry):
        dst = pl.multiple_of(j * group, SUBLANES)
        for k in range(SUBLANES):
            q = qy_ref[i * tn + j * SUBLANES + k]
            rows_ref[pl.ds(dst + k, H_TILES, stride=SUBLANES), :] = (
                y_ref[pl.ds(q, H_TILES, stride=SUBLANES), :])
        return carry

    lax.fori_loop(0, tn // SUBLANES, body, 0)
    packed = jnp.concatenate(_from_tile_rows(rows_ref[...], H_TILES), axis=-1)
    o_ref[...] = _rmsnorm(x_ref[...] + _unpack_pair(packed), gn_ref[...])


def _gather(qy, x2, y, g_final):
    n, d = x2.shape
    tn = ROW_MOVE_TILE
    grid_spec = pltpu.PrefetchScalarGridSpec(
        num_scalar_prefetch=1,
        grid=(n // tn,),
        in_specs=[pl.BlockSpec((tn, d), lambda i, qy: (i, 0)),
                  pl.BlockSpec(y.shape, lambda i, qy: (0, 0), pipeline_mode=pl.Buffered(1)),
                  pl.BlockSpec(g_final.shape, lambda i, qy: (0, 0), pipeline_mode=pl.Buffered(1))],
        out_specs=pl.BlockSpec((tn, d), lambda i, qy: (i, 0)),
        scratch_shapes=[pltpu.VMEM((H_TILES * tn, LANES), jnp.uint32)],
    )
    return pl.pallas_call(
        _gather_kernel,
        out_shape=jax.ShapeDtypeStruct(x2.shape, F32),
        grid_spec=grid_spec,
        compiler_params=pltpu.CompilerParams(
            dimension_semantics=("arbitrary",), vmem_limit_bytes=VMEM_LIMIT_RESIDENT_BYTES),
        name="moe_gather",
    )(qy, x2, y, g_final)


def _row(v):
    return v.reshape(1, -1)


def _router_params(w_rg, b_rg, w_re, b_re):
    pad = ROUTER_ROWS - MOE_GROUPS - N_EXPERTS
    wt = jnp.pad(jnp.concatenate([w_rg, w_re], axis=1).T, ((0, pad), (0, 0)))
    w_hi = wt.astype(BF16)
    w_lo = (wt - w_hi.astype(F32)).astype(BF16)
    b_col = jnp.pad(jnp.concatenate([b_rg, b_re]), (0, pad)).reshape(ROUTER_ROWS, 1)
    return jnp.concatenate([w_hi, w_lo], axis=0), b_col


def kernel(x, mem, norm_mix_g, w_in, conv_a_w, conv_a_b, ln_a_g, ln_a_b, w_a_out, w_pool_grp, pool_scale, conv_c_w, w_c_out, w_o, norm_x_g, norm_mem_g, w_xq, w_xkv, w_xo, norm_ffn_g, w_rg, b_rg, w_re, b_re, w_e_gate, w_e_up, w_e_down, norm_f_g):
    bsz, seq, d = x.shape
    n = bsz * seq
    depth = w_in.shape[0]
    moe_delta = None
    for l in range(depth):
        caw = jnp.broadcast_to(conv_a_w[l][:, None, :], (CONV_A_K, SUBLANES, D_CONV))
        x = _mixer(x, moe_delta, l, _row(norm_mix_g[l]), w_in, caw, _row(conv_a_b[l]),
                   _row(ln_a_g[l]), _row(ln_a_b[l]), w_a_out, w_pool_grp, _row(pool_scale[l]),
                   conv_c_w[l], w_c_out, w_o)
        kt, v = _kv_proj(mem, _row(norm_mem_g[l]), w_xkv, l)
        w_r, b_r = _router_params(w_rg[l], b_rg[l], w_re[l], b_re[l])
        x, hp, cls = _xattn(x, _row(norm_x_g[l]), w_xq, kt, v, w_xo, _row(norm_ffn_g[l]),
                            w_r, b_r, l)
        qs, qy, tile_ea, tile_eb, n_used = _plan(cls.reshape(n // LANES, LANES))
        y = _experts(tile_ea, tile_eb, n_used, _scatter(qs, hp), w_e_gate, w_e_up, w_e_down, l)
        if l < depth - 1:
            moe_delta = _unsort(qy, y)
    return _gather(qy, x.reshape(n, d), y, _row(norm_f_g)).reshape(bsz, seq, d)
```

```python
import functools

import jax
import jax.numpy as jnp
from jax import lax
from jax.experimental import pallas as pl
from jax.experimental.pallas import tpu as pltpu

D_MODEL = 1024
D_CONV = D_MODEL // 2
D_POOL = D_MODEL // 2
D_SC = D_MODEL // 2
CONV_A_K = 31
SC_K = 3
POOL_WINDOWS = (2, 4, 8, 16)
POOL_GROUP_DIM = D_POOL // len(POOL_WINDOWS)
POOL_OUT_DIM = D_MODEL // len(POOL_WINDOWS)
N_MEM = 256
XATTN_HEADS = 4
XATTN_HEAD_DIM = D_MODEL // XATTN_HEADS
MOE_GROUPS = 4
EXPERTS_PER_GROUP = 4
N_EXPERTS = MOE_GROUPS * EXPERTS_PER_GROUP
D_EXPERT = D_MODEL // 4
EPS = 1e-6

OFF_A = 0
OFF_POOL = 2 * D_CONV
OFF_C = OFF_POOL + D_POOL
OFF_G = OFF_C + 3 * D_SC
D_IN_PROJ = OFF_G + 3 * D_MODEL

SUBLANES = 8
LANES = 128
VMEM_LIMIT_BYTES = 56 * 1024 * 1024

HIST_A = 32
HIST_POOL = 16
HIST_C = 8

SEQ_TILE = 512
XATTN_SEQ_TILE = 1024
CONV_ROW_CHUNK = 16
W_STEPS = 8
ROUTER_LANES = LANES

PAIRS_PER_GROUP = EXPERTS_PER_GROUP * (EXPERTS_PER_GROUP - 1) // 2
N_CLASSES = MOE_GROUPS * PAIRS_PER_GROUP
EXPERT_ROW_TILE = 128
EXPERT_STEP_TILES = 8
ROUTER_ROWS = 32
SCATTER_TILE = 1024
UNSORT_TILE = 2048
GATHER_TILE = 512
VMEM_LIMIT_RESIDENT_BYTES = 58 * 1024 * 1024
H_TILES = D_MODEL // 2 // LANES
ROW_TILES = H_TILES + 1

BF16 = jnp.bfloat16
F32 = jnp.float32


def _sorted_rows(n_tokens):
    return n_tokens + N_CLASSES * EXPERT_ROW_TILE


def _sigmoid(v):
    return 0.5 * jnp.tanh(0.5 * v) + 0.5


def _rmsnorm(xf, g):
    return xf * lax.rsqrt(jnp.mean(xf * xf, axis=-1, keepdims=True) + EPS) * g


def _dot(a, b):
    return jnp.dot(a, b, preferred_element_type=F32)


def _const_spec(shape):
    nd = len(shape)
    return pl.BlockSpec(shape, lambda *_: (0,) * nd, pipeline_mode=pl.Buffered(1))


def _weight_slab_spec(w, layer, axis=0):
    block = list(w.shape[1:])
    block[axis] //= W_STEPS
    return pl.BlockSpec(
        (None, *block),
        lambda i, *_: (layer,) + tuple(
            jnp.minimum(i, W_STEPS - 1) if a == axis else 0 for a in range(len(block))))


def _store_weight_slab(step, src_ref, dst_ref, axis=0):
    rows = src_ref.shape[axis]
    at = pl.ds(pl.multiple_of(step * rows, rows), rows)
    idx = tuple(at if a == axis else slice(None) for a in range(len(src_ref.shape)))
    dst_ref[idx] = src_ref[...].astype(BF16)


def _unpack_pair(q):
    return jnp.concatenate(
        [pltpu.unpack_elementwise(q, index=0, packed_dtype=BF16, unpacked_dtype=F32),
         pltpu.unpack_elementwise(q, index=1, packed_dtype=BF16, unpacked_dtype=F32)], axis=-1)


def _pack_pair(v):
    half = v.shape[-1] // 2
    return pltpu.pack_elementwise([v[:, :half], v[:, half:]], packed_dtype=BF16)


def _to_tile_rows(v):
    t, w = v.shape
    nj = w // LANES
    parts = [v[:, j * LANES:(j + 1) * LANES].reshape(t // SUBLANES, 1, SUBLANES, LANES)
             for j in range(nj)]
    return jnp.concatenate(parts, axis=1).reshape(nj * t, LANES)


def _from_tile_rows(v2, nj):
    t = v2.shape[0] // nj
    v4 = v2.reshape(t // SUBLANES, nj, SUBLANES, LANES)
    return [v4[:, j].reshape(t, LANES) for j in range(nj)]


def _tile_row_offset(r, nj):
    hi = jnp.floor(r / SUBLANES)
    return hi * (SUBLANES * nj) + (r - hi * SUBLANES)


def _mixer_kernel(x_ref, *refs, has_moe_delta, tiles_per_seq):
    y_ref, refs = (refs[0], refs[1:]) if has_moe_delta else (None, refs)
    (g_ref, w_in32, caw_ref, cab_ref, lng_ref, lnb_ref, w_a_out32, w_pool32, pool_scale_ref,
     ccw_ref, w_c_out32, w_o32, o_ref, abuf, ubuf, vbuf, actbuf, pbuf,
     w_in_ref, w_a_out_ref, w_pool_ref, w_c_out_ref, w_o_ref) = refs
    step = pl.program_id(0)

    @pl.when(step < W_STEPS)
    def _():
        for src, dst in ((w_in32, w_in_ref), (w_a_out32, w_a_out_ref),
                         (w_c_out32, w_c_out_ref), (w_o32, w_o_ref)):
            _store_weight_slab(step, src, dst)
        _store_weight_slab(step, w_pool32, w_pool_ref, axis=1)

    @pl.when(step >= W_STEPS)
    def _():
        _mix_tile((step - W_STEPS) % tiles_per_seq, x_ref, y_ref, g_ref, w_in_ref, caw_ref,
                  cab_ref, lng_ref, lnb_ref, w_a_out_ref, w_pool_ref, pool_scale_ref, ccw_ref,
                  w_c_out_ref, w_o_ref, o_ref, abuf, ubuf, vbuf, actbuf, pbuf)


def _mix_tile(s, x_ref, y_ref, g_ref, w_in_ref, caw_ref, cab_ref, lng_ref, lnb_ref, w_a_out_ref,
              w_pool_ref, pool_scale_ref, ccw_ref, w_c_out_ref, w_o_ref,
              o_ref, abuf, ubuf, vbuf, actbuf, pbuf):
    ts = x_ref.shape[1]

    @pl.when(s == 0)
    def _():
        abuf[0, 0:HIST_A, :] = jnp.zeros((HIST_A, D_CONV), F32)
        ubuf[0:HIST_POOL, :] = jnp.zeros((HIST_POOL, D_POOL), F32)
        vbuf[0:HIST_C, :] = jnp.zeros((HIST_C, D_SC), F32)

    x = x_ref[0]
    if y_ref is not None:
        x = x + _unpack_pair(jnp.concatenate(_from_tile_rows(y_ref[...], H_TILES), axis=-1))
    h = _rmsnorm(x, g_ref[...]).astype(BF16)

    def proj(off, width):
        return _dot(h, w_in_ref[:, off:off + width])

    pa = proj(OFF_A, 2 * D_CONV)
    abuf[0, HIST_A:HIST_A + ts, :] = pa[:, :D_CONV] * _sigmoid(pa[:, D_CONV:])
    shifted_rows = HIST_A + ts - SUBLANES
    for r in range(1, SUBLANES):
        abuf[r, 0:shifted_rows, :] = abuf[0, r:r + shifted_rows, :]
    chunk3 = (CONV_ROW_CHUNK // SUBLANES, SUBLANES, D_CONV)
    for c0 in range(0, ts, CONV_ROW_CHUNK):
        acc = jnp.broadcast_to(cab_ref[...], chunk3)
        for k in range(CONV_A_K):
            q, r = divmod(HIST_A - (CONV_A_K - 1) + k, SUBLANES)
            lo = c0 + q * SUBLANES
            acc = acc + caw_ref[k] * abuf[r, lo:lo + CONV_ROW_CHUNK, :].reshape(chunk3)
        acc = acc.reshape(CONV_ROW_CHUNK, D_CONV)
        mu = jnp.mean(acc, axis=-1, keepdims=True)
        cen = acc - mu
        var = jnp.mean(cen * cen, axis=-1, keepdims=True)
        ln = cen * lax.rsqrt(var + EPS) * lng_ref[...] + lnb_ref[...]
        actbuf[c0:c0 + CONV_ROW_CHUNK, :] = (ln * _sigmoid(ln)).astype(BF16)
    abuf[0, 0:HIST_A, :] = abuf[0, ts:ts + HIST_A, :]

    pbuf[...] = proj(OFF_POOL, D_IN_PROJ - OFF_POOL)

    def pcol(off, width):
        return pbuf[:, off - OFF_POOL:off - OFF_POOL + width]

    pu = pcol(OFF_POOL, D_POOL)
    ubuf[HIST_POOL:HIST_POOL + ts, :] = pu
    merged = _sigmoid(pcol(OFF_G, D_MODEL)) * _dot(actbuf[...], w_a_out_ref[...])

    t_glob = s * ts + lax.broadcasted_iota(jnp.int32, (ts, 1), 0)
    p_parts = []
    for i, w in enumerate(POOL_WINDOWS):
        c0, c1 = i * POOL_GROUP_DIM, (i + 1) * POOL_GROUP_DIM
        tok = pu[:, c0:c1]
        win = ubuf[:, c0:c1]
        shift = 1
        while shift < w:
            win = win + pltpu.roll(win, shift, 0)
            shift *= 2
        win = win[HIST_POOL:]
        cnt = jnp.minimum(t_glob + 1, w).astype(F32)
        pin = (win / cnt - tok).astype(BF16)
        p_parts.append(_dot(pin, w_pool_ref[i]))
    ubuf[0:HIST_POOL, :] = ubuf[ts:ts + HIST_POOL, :]
    merged = merged + _sigmoid(pcol(OFF_G + D_MODEL, D_MODEL)) * (jnp.concatenate(p_parts, axis=-1) * pool_scale_ref[...])

    c_b = pcol(OFF_C + D_SC, D_SC)
    v = pcol(OFF_C + 2 * D_SC, D_SC) * pcol(OFF_C, D_SC)
    vbuf[HIST_C:HIST_C + ts, :] = v
    conv_c = ccw_ref[SC_K - 1:SC_K, :] * v
    for k in range(SC_K - 1):
        off = HIST_C - (SC_K - 1) + k
        conv_c = conv_c + ccw_ref[k:k + 1, :] * vbuf[off:off + ts, :]
    vbuf[0:HIST_C, :] = vbuf[ts:ts + HIST_C, :]
    merged = merged + _sigmoid(pcol(OFF_G + 2 * D_MODEL, D_MODEL)) * _dot((c_b * conv_c).astype(BF16), w_c_out_ref[...])

    o_ref[0] = x + _dot(merged.astype(BF16), w_o_ref[...])


def _mixer(x, moe_delta, layer, g, w_in, caw, cab, lng, lnb, w_a_out, w_pool, pool_scale, ccw,
           w_c_out, w_o):
    b, s, d = x.shape
    ts = SEQ_TILE
    nt = s // ts

    def tile(i):
        return jnp.maximum(i - W_STEPS, 0)

    def small(c):
        return _const_spec(c.shape)

    def slab_spec(w, axis):
        return _weight_slab_spec(w, layer, axis)

    delta = () if moe_delta is None else (moe_delta,)
    delta_specs = [pl.BlockSpec((H_TILES * ts, LANES), lambda i: (tile(i), 0))] * len(delta)
    x_spec = pl.BlockSpec((1, ts, d), lambda i: (tile(i) // nt, tile(i) % nt, 0))
    return pl.pallas_call(
        functools.partial(_mixer_kernel, has_moe_delta=moe_delta is not None, tiles_per_seq=nt),
        out_shape=jax.ShapeDtypeStruct(x.shape, F32),
        grid=(W_STEPS + b * nt,),
        in_specs=[x_spec] + delta_specs
        + [small(g), slab_spec(w_in, 0), small(caw), small(cab), small(lng), small(lnb),
           slab_spec(w_a_out, 0), slab_spec(w_pool, 1), small(pool_scale), small(ccw),
           slab_spec(w_c_out, 0), slab_spec(w_o, 0)],
        out_specs=x_spec,
        scratch_shapes=[pltpu.VMEM((SUBLANES, HIST_A + ts, D_CONV), F32),
                        pltpu.VMEM((HIST_POOL + ts, D_POOL), F32),
                        pltpu.VMEM((HIST_C + ts, D_SC), F32),
                        pltpu.VMEM((ts, D_CONV), BF16),
                        pltpu.VMEM((ts, D_IN_PROJ - OFF_POOL), F32)]
        + [pltpu.VMEM(w.shape[1:], BF16) for w in (w_in, w_a_out, w_pool, w_c_out, w_o)],
        compiler_params=pltpu.CompilerParams(
            dimension_semantics=("arbitrary",), vmem_limit_bytes=VMEM_LIMIT_BYTES),
        name="mixer",
    )(x, *delta, g, w_in, caw, cab, lng, lnb, w_a_out, w_pool, pool_scale, ccw, w_c_out, w_o)


def _kv_kernel(mem_ref, g_ref, w_kv32, kt_ref, v_ref, w_kv_ref):
    step = pl.program_id(0)

    @pl.when(step < W_STEPS)
    def _():
        _store_weight_slab(step, w_kv32, w_kv_ref)

    @pl.when(step >= W_STEPS)
    def _():
        mn = _rmsnorm(mem_ref[0], g_ref[...]).astype(BF16)
        kv = _dot(mn, w_kv_ref[...])
        kt_ref[0] = kv[:, :D_MODEL].T.astype(BF16)
        v_ref[0] = kv[:, D_MODEL:].astype(BF16)


def _kv_proj(mem, g, w_kv, layer):
    b, m, d = mem.shape

    def row(i):
        return (jnp.maximum(i - W_STEPS, 0), 0, 0)

    return pl.pallas_call(
        _kv_kernel,
        out_shape=(jax.ShapeDtypeStruct((b, d, m), BF16), jax.ShapeDtypeStruct((b, m, d), BF16)),
        grid=(W_STEPS + b,),
        in_specs=[pl.BlockSpec((1, m, d), row), _const_spec(g.shape),
                  _weight_slab_spec(w_kv, layer)],
        out_specs=(pl.BlockSpec((1, d, m), row), pl.BlockSpec((1, m, d), row)),
        scratch_shapes=[pltpu.VMEM(w_kv.shape[1:], BF16)],
        compiler_params=pltpu.CompilerParams(
            dimension_semantics=("arbitrary",), vmem_limit_bytes=VMEM_LIMIT_BYTES),
        name="kv_proj",
    )(mem, g, w_kv)


def _route_rows(lt):
    def first_argmax(vals, vmax):
        idx = jnp.full(vmax.shape, len(vals) - 1, jnp.int32)
        for k in range(len(vals) - 2, -1, -1):
            idx = jnp.where(vals[k] == vmax, k, idx)
        return idx

    g = [lt[k:k + 1, :] for k in range(MOE_GROUPS)]
    gmax = functools.reduce(jnp.maximum, g)
    g_idx = first_argmax(g, gmax)
    g_val = 1.0 / sum(jnp.exp(v - gmax) for v in g)
    e = []
    for j in range(EXPERTS_PER_GROUP):
        ej = lt[MOE_GROUPS + j:MOE_GROUPS + j + 1, :]
        for grp in range(1, MOE_GROUPS):
            r = MOE_GROUPS + grp * EXPERTS_PER_GROUP + j
            ej = jnp.where(g_idx == grp, lt[r:r + 1, :], ej)
        e.append(ej)
    m1 = functools.reduce(jnp.maximum, e)
    i1 = first_argmax(e, m1)
    e2 = [jnp.where(i1 == j, -jnp.inf, e[j]) for j in range(EXPERTS_PER_GROUP)]
    m2 = functools.reduce(jnp.maximum, e2)
    i2 = first_argmax(e2, m2)
    esum = sum(jnp.exp(v - m1) for v in e)
    p1 = 1.0 / esum
    p2 = jnp.exp(m2 - m1) / esum
    w1 = g_val * (p1 / (p1 + p2))
    w2 = g_val * (p2 / (p1 + p2))
    lo = jnp.minimum(i1, i2)
    hi = jnp.maximum(i1, i2)
    pair = ((lo * (2 * EXPERTS_PER_GROUP - 1 - lo)) >> 1) + (hi - lo - 1)
    cls = g_idx * PAIRS_PER_GROUP + pair
    first_is_low = i1 < i2
    return cls, jnp.where(first_is_low, w1, w2), jnp.where(first_is_low, w2, w1)


def _xattn_kernel(x_ref, g_ref, w_q32, kt_ref, v_ref, w_o32, gf_ref, w_r_ref, b_r_ref,
                  o_ref, hp_ref, cls_ref, xn_prev, w_q_ref, w_o_ref):
    step = pl.program_id(0)

    @pl.when(step == 0)
    def _():
        xn_prev[...] = jnp.zeros(xn_prev.shape, F32)

    @pl.when(step < W_STEPS)
    def _():
        _store_weight_slab(step, w_q32, w_q_ref)
        _store_weight_slab(step, w_o32, w_o_ref)

    @pl.when(step >= W_STEPS)
    def _():
        _xattn_tile(x_ref, g_ref, w_q_ref, kt_ref, v_ref, w_o_ref, gf_ref, w_r_ref, b_r_ref,
                    o_ref, hp_ref, cls_ref, xn_prev)


def _xattn_tile(x_ref, g_ref, w_q_ref, kt_ref, v_ref, w_o_ref, gf_ref, w_r_ref, b_r_ref,
                o_ref, hp_ref, cls_ref, xn_prev):
    hf = _rmsnorm(xn_prev[...], gf_ref[...])
    h_hi = hf.astype(BF16)
    h_lo = (hf - h_hi.astype(F32)).astype(BF16)
    nt_dims = (((1,), (1,)), ((), ()))
    part = lax.dot_general(w_r_ref[...], h_hi, nt_dims, preferred_element_type=F32)
    lt = (part[:ROUTER_ROWS] + part[ROUTER_ROWS:]
          + lax.dot_general(w_r_ref[:ROUTER_ROWS, :], h_lo, nt_dims, preferred_element_type=F32)
          + b_r_ref[...])
    cls, wa, wb = _route_rows(lt)
    ts = cls.shape[1]
    for c in range(ts // LANES):
        cls_ref[0, c:c + 1, :] = cls[:, c * LANES:(c + 1) * LANES]
    row = lax.broadcasted_iota(jnp.int32, (LANES, ts), 0)

    def weight_lanes(w):
        w_hi = w.astype(BF16).astype(F32)
        return jnp.where(row == 0, w_hi, jnp.where(row == 1, w - w_hi, 0.0)).T

    wt_tile = pltpu.pack_elementwise([weight_lanes(wa), weight_lanes(wb)], packed_dtype=BF16)
    hp_ref[...] = _to_tile_rows(jnp.concatenate([_pack_pair(hf), wt_tile], axis=-1))

    x = x_ref[0]
    h = _rmsnorm(x, g_ref[...]).astype(BF16)
    q = (_dot(h, w_q_ref[...]) * (XATTN_HEAD_DIM ** -0.5)).astype(BF16)
    heads = []
    for hd in range(XATTN_HEADS):
        c0, c1 = hd * XATTN_HEAD_DIM, (hd + 1) * XATTN_HEAD_DIM
        sc = _dot(q[:, c0:c1], kt_ref[0, c0:c1, :])
        e = jnp.exp(sc - jnp.max(sc, axis=-1, keepdims=True))
        denom = jnp.sum(e, axis=-1, keepdims=True)
        heads.append(_dot(e.astype(BF16), v_ref[0, :, c0:c1]) / denom)
    o = jnp.concatenate(heads, axis=-1).astype(BF16)
    xn = x + _dot(o, w_o_ref[...])
    o_ref[0] = xn
    xn_prev[...] = xn


def _xattn(x, g, w_q, kt, v, w_o, g_ffn, w_r, b_r, layer):
    b, s, d = x.shape
    ts = XATTN_SEQ_TILE
    nt = s // ts
    n = b * s
    tiles = b * nt

    def attn_tile(step):
        return jnp.clip(step - W_STEPS, 0, tiles - 1)

    def route_tile(step):
        return jnp.maximum(step - W_STEPS - 1, 0)

    return pl.pallas_call(
        _xattn_kernel,
        out_shape=(jax.ShapeDtypeStruct(x.shape, F32),
                   jax.ShapeDtypeStruct((ROW_TILES * n, LANES), jnp.uint32),
                   jax.ShapeDtypeStruct((n // ts, ts // LANES, LANES), jnp.int32)),
        grid=(W_STEPS + tiles + 1,),
        in_specs=[pl.BlockSpec((1, ts, d), lambda i: (attn_tile(i) // nt, attn_tile(i) % nt, 0)),
                  _const_spec(g.shape), _weight_slab_spec(w_q, layer),
                  pl.BlockSpec((1, d, N_MEM), lambda i: (attn_tile(i) // nt, 0, 0)),
                  pl.BlockSpec((1, N_MEM, d), lambda i: (attn_tile(i) // nt, 0, 0)),
                  _weight_slab_spec(w_o, layer), _const_spec(g_ffn.shape),
                  _const_spec(w_r.shape), _const_spec(b_r.shape)],
        out_specs=(pl.BlockSpec((1, ts, d), lambda i: (attn_tile(i) // nt, attn_tile(i) % nt, 0)),
                   pl.BlockSpec((ROW_TILES * ts, LANES), lambda i: (route_tile(i), 0)),
                   pl.BlockSpec((1, ts // LANES, LANES), lambda i: (route_tile(i), 0, 0))),
        scratch_shapes=[pltpu.VMEM((ts, d), F32), pltpu.VMEM(w_q.shape[1:], BF16),
                        pltpu.VMEM(w_o.shape[1:], BF16)],
        compiler_params=pltpu.CompilerParams(
            dimension_semantics=("arbitrary",), vmem_limit_bytes=VMEM_LIMIT_BYTES),
        name="xattn",
    )(x, g, w_q, kt, v, w_o, g_ffn, w_r, b_r)


def _plan_kernel(cls_ref, qs_ref, qy_ref, ea_ref, eb_ref, nu_ref):
    cls = cls_ref[...]
    nr, nl = cls.shape
    r = lax.broadcasted_iota(jnp.int32, (nl, nl), 0)
    c = lax.broadcasted_iota(jnp.int32, (nl, nl), 1)
    lanes_before = (r < c).astype(BF16)
    rr = lax.broadcasted_iota(jnp.int32, (nr, nr), 0)
    rc = lax.broadcasted_iota(jnp.int32, (nr, nr), 1)
    rows_before = (rc < rr).astype(BF16)
    ones_l = jnp.ones((nl, nl), BF16)
    ones_r = jnp.ones((nr, nr), BF16)
    n_tab = ea_ref.shape[1]
    tile_i = lax.broadcasted_iota(jnp.int32, (SUBLANES, n_tab), 1).astype(F32)
    run = jnp.zeros((nr, nl), F32)
    pos = jnp.zeros((nr, nl), F32)
    tile_cls = jnp.zeros((SUBLANES, n_tab), F32)
    for k in range(N_CLASSES):
        m = cls == k
        mb = jnp.where(m, 1.0, 0.0).astype(BF16)
        rowtot = _dot(mb, ones_l).astype(BF16)
        rank = _dot(mb, lanes_before) + _dot(rows_before, rowtot)
        total = _dot(ones_r, rowtot)
        pos = jnp.where(m, run * EXPERT_ROW_TILE + rank, pos)
        run = run + jnp.floor((total + (EXPERT_ROW_TILE - 1)) / EXPERT_ROW_TILE)
        end = jnp.concatenate([run[:SUBLANES, :]] * (n_tab // nl), axis=1)
        tile_cls = tile_cls + jnp.where(end <= tile_i, 1.0, 0.0)
    qs_ref[...] = _tile_row_offset(pos, ROW_TILES).astype(jnp.int32)
    qy_ref[...] = _tile_row_offset(pos, H_TILES).astype(jnp.int32)
    tc = jnp.minimum(tile_cls, N_CLASSES - 1.0).astype(jnp.int32)
    grp = sum(jnp.where(tc >= g * PAIRS_PER_GROUP, 1, 0) for g in range(1, MOE_GROUPS))
    pair = tc - grp * PAIRS_PER_GROUP
    pair_lo = jnp.where(pair >= 3, 1, 0) + jnp.where(pair >= 5, 1, 0)
    pair_hi = pair + 1 - jnp.where(pair >= 5, 3, jnp.where(pair >= 3, 2, 0))
    ea_ref[...] = grp * EXPERTS_PER_GROUP + pair_lo
    eb_ref[...] = grp * EXPERTS_PER_GROUP + pair_hi
    nu_ref[...] = run[:SUBLANES, :].astype(jnp.int32)


def _plan(cls2):
    nr, nl = cls2.shape
    n_tiles = _sorted_rows(nr * nl) // EXPERT_ROW_TILE
    n_tab = -(-n_tiles // nl) * nl
    vmem = pl.BlockSpec(memory_space=pltpu.VMEM)
    qs, qy, ea, eb, nu = pl.pallas_call(
        _plan_kernel,
        out_shape=(jax.ShapeDtypeStruct((nr, nl), jnp.int32),
                   jax.ShapeDtypeStruct((nr, nl), jnp.int32),
                   jax.ShapeDtypeStruct((SUBLANES, n_tab), jnp.int32),
                   jax.ShapeDtypeStruct((SUBLANES, n_tab), jnp.int32),
                   jax.ShapeDtypeStruct((SUBLANES, nl), jnp.int32)),
        in_specs=[vmem],
        out_specs=(vmem, vmem, vmem, vmem, vmem),
        name="moe_plan",
    )(cls2)
    return (qs.reshape(nr * nl), qy.reshape(nr * nl),
            ea[0, :n_tiles], eb[0, :n_tiles], nu[0, :1])


def _scatter_kernel(qs_ref, hp_ref, s_ref):
    i = pl.program_id(0)
    tn = hp_ref.shape[0] // ROW_TILES
    group = SUBLANES * ROW_TILES

    @pl.when(i == 0)
    def _():
        zero_rows = _pack_pair(jnp.zeros((hp_ref.shape[0], 2 * LANES), F32))

        def fill(c, carry):
            start = pl.multiple_of(c * hp_ref.shape[0], SUBLANES)
            s_ref[pl.ds(start, hp_ref.shape[0]), :] = zero_rows
            return carry

        lax.fori_loop(0, s_ref.shape[0] // hp_ref.shape[0], fill, 0)

    def body(j, carry):
        src = pl.multiple_of(j * group, SUBLANES)
        for k in range(SUBLANES):
            q = qs_ref[i * tn + j * SUBLANES + k]
            s_ref[pl.ds(q, ROW_TILES, stride=SUBLANES), :] = (
                hp_ref[pl.ds(src + k, ROW_TILES, stride=SUBLANES), :])
        return carry

    lax.fori_loop(0, tn // SUBLANES, body, 0)


def _scatter(qs, hp):
    n = qs.shape[0]
    tn = SCATTER_TILE
    rows = ROW_TILES * _sorted_rows(n)
    grid_spec = pltpu.PrefetchScalarGridSpec(
        num_scalar_prefetch=1,
        grid=(n // tn,),
        in_specs=[pl.BlockSpec((ROW_TILES * tn, LANES), lambda i, qs: (i, 0))],
        out_specs=pl.BlockSpec((rows, LANES), lambda i, qs: (0, 0), pipeline_mode=pl.Buffered(1)),
    )
    return pl.pallas_call(
        _scatter_kernel,
        out_shape=jax.ShapeDtypeStruct((rows, LANES), jnp.uint32),
        grid_spec=grid_spec,
        compiler_params=pltpu.CompilerParams(
            dimension_semantics=("arbitrary",), vmem_limit_bytes=VMEM_LIMIT_RESIDENT_BYTES),
        name="moe_scatter",
    )(qs, hp)


def _expert_kernel(ea_ref, eb_ref, nu_ref, s_ref, wg32_ref, wu32_ref, wd32_ref, y_ref,
                   wg_ref, wu_ref, wd_ref):
    step = pl.program_id(0)

    @pl.when(step < N_EXPERTS)
    def _():
        wg_ref[step] = wg32_ref[...].astype(BF16)
        wu_ref[step] = wu32_ref[...].astype(BF16)
        wd_ref[step] = wd32_ref[...].astype(BF16)

    first = (step - N_EXPERTS) * EXPERT_STEP_TILES
    rec = EXPERT_ROW_TILE * ROW_TILES
    out = EXPERT_ROW_TILE * H_TILES

    @pl.when((step >= N_EXPERTS) & (first < nu_ref[0]))
    def _():
        acts = []
        for j in range(EXPERT_STEP_TILES):
            t = first + j
            tiles = _from_tile_rows(s_ref[j * rec:(j + 1) * rec, :], ROW_TILES)
            hb = _unpack_pair(jnp.concatenate(tiles[:H_TILES], axis=-1)).astype(BF16)
            for e_ref, idx in ((ea_ref, 0), (eb_ref, 1)):
                e = e_ref[t]
                wt = pltpu.unpack_elementwise(tiles[H_TILES], index=idx, packed_dtype=BF16,
                                              unpacked_dtype=F32)
                wt = wt[:, 0:1] + wt[:, 1:2]
                gt = _dot(hb, wg_ref[e])
                acts.append((gt * _sigmoid(gt) * _dot(hb, wu_ref[e]) * wt).astype(BF16))
        for j in range(EXPERT_STEP_TILES):
            t = first + j
            y = (_dot(acts[2 * j], wd_ref[ea_ref[t]])
                 + _dot(acts[2 * j + 1], wd_ref[eb_ref[t]]))
            y_ref[j * out:(j + 1) * out, :] = _to_tile_rows(_pack_pair(y))

    @pl.when((step >= N_EXPERTS) & (first >= nu_ref[0]))
    def _():
        y_ref[...] = _pack_pair(jnp.zeros((y_ref.shape[0], 2 * LANES), F32))


def _experts(tile_ea, tile_eb, n_used, s, wg, wu, wd, layer):
    rows = s.shape[0] // ROW_TILES
    tb = EXPERT_ROW_TILE * EXPERT_STEP_TILES

    def row_block(i, *_):
        return (jnp.maximum(i - N_EXPERTS, 0), 0)

    def expert_block(i, *_):
        return (layer, jnp.minimum(i, N_EXPERTS - 1), 0, 0)

    grid_spec = pltpu.PrefetchScalarGridSpec(
        num_scalar_prefetch=3,
        grid=(N_EXPERTS + rows // tb,),
        in_specs=[pl.BlockSpec((ROW_TILES * tb, LANES), row_block),
                  pl.BlockSpec((None, None) + wg.shape[2:], expert_block),
                  pl.BlockSpec((None, None) + wu.shape[2:], expert_block),
                  pl.BlockSpec((None, None) + wd.shape[2:], expert_block)],
        out_specs=pl.BlockSpec((H_TILES * tb, LANES), row_block),
        scratch_shapes=[pltpu.VMEM(wg.shape[1:], BF16), pltpu.VMEM(wu.shape[1:], BF16),
                        pltpu.VMEM(wd.shape[1:], BF16)],
    )
    return pl.pallas_call(
        _expert_kernel,
        out_shape=jax.ShapeDtypeStruct((H_TILES * rows, LANES), jnp.uint32),
        grid_spec=grid_spec,
        compiler_params=pltpu.CompilerParams(
            dimension_semantics=("arbitrary",), vmem_limit_bytes=VMEM_LIMIT_BYTES),
        name="moe_experts",
    )(tile_ea, tile_eb, n_used, s, wg, wu, wd)


def _unsort_kernel(qy_ref, y_ref, o_ref):
    i = pl.program_id(0)
    tn = o_ref.shape[0] // H_TILES
    group = SUBLANES * H_TILES

    def body(j, carry):
        dst = pl.multiple_of(j * group, SUBLANES)
        for k in range(SUBLANES):
            q = qy_ref[i * tn + j * SUBLANES + k]
            o_ref[pl.ds(dst + k, H_TILES, stride=SUBLANES), :] = (
                y_ref[pl.ds(q, H_TILES, stride=SUBLANES), :])
        return carry

    lax.fori_loop(0, tn // SUBLANES, body, 0)


def _unsort(qy, y):
    n = qy.shape[0]
    tn = UNSORT_TILE
    grid_spec = pltpu.PrefetchScalarGridSpec(
        num_scalar_prefetch=1,
        grid=(n // tn,),
        in_specs=[pl.BlockSpec(y.shape, lambda i, qy: (0, 0), pipeline_mode=pl.Buffered(1))],
        out_specs=pl.BlockSpec((H_TILES * tn, LANES), lambda i, qy: (i, 0)),
    )
    return pl.pallas_call(
        _unsort_kernel,
        out_shape=jax.ShapeDtypeStruct((H_TILES * n, LANES), jnp.uint32),
        grid_spec=grid_spec,
        compiler_params=pltpu.CompilerParams(
            dimension_semantics=("arbitrary",), vmem_limit_bytes=VMEM_LIMIT_RESIDENT_BYTES),
        name="moe_unsort",
    )(qy, y)


def _gather_kernel(qy_ref, x_ref, y_ref, gn_ref, o_ref, rows_ref):
    i = pl.program_id(0)
    tn = x_ref.shape[0]
    group = SUBLANES * H_TILES

    def body(j, carry):
        dst = pl.multiple_of(j * group, SUBLANES)
        for k in range(SUBLANES):
            q = qy_ref[i * tn + j * SUBLANES + k]
            rows_ref[pl.ds(dst + k, H_TILES, stride=SUBLANES), :] = (
                y_ref[pl.ds(q, H_TILES, stride=SUBLANES), :])
        return carry

    lax.fori_loop(0, tn // SUBLANES, body, 0)
    packed = jnp.concatenate(_from_tile_rows(rows_ref[...], H_TILES), axis=-1)
    o_ref[...] = _rmsnorm(x_ref[...] + _unpack_pair(packed), gn_ref[...])


def _gather(qy, x2, y, g_final):
    n, d = x2.shape
    tn = GATHER_TILE
    grid_spec = pltpu.PrefetchScalarGridSpec(
        num_scalar_prefetch=1,
        grid=(n // tn,),
        in_specs=[pl.BlockSpec((tn, d), lambda i, qy: (i, 0)),
                  pl.BlockSpec(y.shape, lambda i, qy: (0, 0), pipeline_mode=pl.Buffered(1)),
                  pl.BlockSpec(g_final.shape, lambda i, qy: (0, 0), pipeline_mode=pl.Buffered(1))],
        out_specs=pl.BlockSpec((tn, d), lambda i, qy: (i, 0)),
        scratch_shapes=[pltpu.VMEM((H_TILES * tn, LANES), jnp.uint32)],
    )
    return pl.pallas_call(
        _gather_kernel,
        out_shape=jax.ShapeDtypeStruct(x2.shape, F32),
        grid_spec=grid_spec,
        compiler_params=pltpu.CompilerParams(
            dimension_semantics=("arbitrary",), vmem_limit_bytes=VMEM_LIMIT_RESIDENT_BYTES),
        name="moe_gather",
    )(qy, x2, y, g_final)


def _row(v):
    return v.reshape(1, -1)


def _router_params(w_rg, b_rg, w_re, b_re):
    pad = ROUTER_ROWS - MOE_GROUPS - N_EXPERTS
    wt = jnp.pad(jnp.concatenate([w_rg, w_re], axis=1).T, ((0, pad), (0, 0)))
    w_hi = wt.astype(BF16)
    w_lo = (wt - w_hi.astype(F32)).astype(BF16)
    b_col = jnp.pad(jnp.concatenate([b_rg, b_re]), (0, pad)).reshape(ROUTER_ROWS, 1)
    return jnp.concatenate([w_hi, w_lo], axis=0), b_col


def kernel(x, mem, norm_mix_g, w_in, conv_a_w, conv_a_b, ln_a_g, ln_a_b, w_a_out, w_pool_grp, pool_scale, conv_c_w, w_c_out, w_o, norm_x_g, norm_mem_g, w_xq, w_xkv, w_xo, norm_ffn_g, w_rg, b_rg, w_re, b_re, w_e_gate, w_e_up, w_e_down, norm_f_g):
    bsz, seq, d = x.shape
    n = bsz * seq
    depth = w_in.shape[0]
    moe_delta = None
    for l in range(depth):
        caw = jnp.broadcast_to(conv_a_w[l][:, None, :], (CONV_A_K, SUBLANES, D_CONV))
        x = _mixer(x, moe_delta, l, _row(norm_mix_g[l]), w_in, caw, _row(conv_a_b[l]),
                   _row(ln_a_g[l]), _row(ln_a_b[l]), w_a_out, w_pool_grp, _row(pool_scale[l]),
                   conv_c_w[l], w_c_out, w_o)
        kt, v = _kv_proj(mem, _row(norm_mem_g[l]), w_xkv, l)
        w_r, b_r = _router_params(w_rg[l], b_rg[l], w_re[l], b_re[l])
        x, hp, cls = _xattn(x, _row(norm_x_g[l]), w_xq, kt, v, w_xo, _row(norm_ffn_g[l]),
                            w_r, b_r, l)
        qs, qy, tile_ea, tile_eb, n_used = _plan(cls.reshape(n // LANES, LANES))
        y = _experts(tile_ea, tile_eb, n_used, _scatter(qs, hp), w_e_gate, w_e_up, w_e_down, l)
        if l < depth - 1:
            moe_delta = _unsort(qy, y)
    return _gather(qy, x.reshape(n, d), y, _row(norm_f_g)).reshape(bsz, seq, d)
```

```python
import functools

import jax
import jax.numpy as jnp
from jax import lax
from jax.experimental import pallas as pl
from jax.experimental.pallas import tpu as pltpu

D_MODEL = 1024
D_CONV = D_MODEL // 2
D_POOL = D_MODEL // 2
D_SC = D_MODEL // 2
CONV_A_K = 31
SC_K = 3
POOL_WINDOWS = (2, 4, 8, 16)
POOL_GROUP_DIM = D_POOL // len(POOL_WINDOWS)
XATTN_HEADS = 4
XATTN_HEAD_DIM = D_MODEL // XATTN_HEADS
MOE_GROUPS = 4
EXPERTS_PER_GROUP = 4
N_EXPERTS = MOE_GROUPS * EXPERTS_PER_GROUP
EPS = 1e-6

OFF_A = 0
OFF_POOL = 2 * D_CONV
OFF_C = OFF_POOL + D_POOL
OFF_G = OFF_C + 3 * D_SC
D_IN_PROJ = OFF_G + 3 * D_MODEL

SUBLANES = 8
LANES = 128
VMEM_LIMIT_BYTES = 56 * 1024 * 1024

HIST_A = 32
HIST_POOL = 16
HIST_C = 8

SEQ_TILE = 512
XATTN_SEQ_TILE = 1024
CONV_ROW_CHUNK = 16
W_STEPS = 8

PAIRS_PER_GROUP = EXPERTS_PER_GROUP * (EXPERTS_PER_GROUP - 1) // 2
N_CLASSES = MOE_GROUPS * PAIRS_PER_GROUP
EXPERT_ROW_TILE = 128
EXPERT_STEP_TILES = 8
ROUTER_ROWS = 32
SCATTER_TILE = 1024
UNSORT_TILE = 2048
GATHER_TILE = 512
VMEM_LIMIT_RESIDENT_BYTES = 58 * 1024 * 1024
H_TILES = D_MODEL // 2 // LANES
ROW_TILES = H_TILES + 1

BF16 = jnp.bfloat16
F32 = jnp.float32


def _sorted_rows(n_tokens):
    return n_tokens + N_CLASSES * EXPERT_ROW_TILE


def _sigmoid(v):
    return 0.5 * jnp.tanh(0.5 * v) + 0.5


def _rmsnorm(xf, g):
    return xf * lax.rsqrt(jnp.mean(xf * xf, axis=-1, keepdims=True) + EPS) * g


def _dot(a, b):
    return jnp.dot(a, b, preferred_element_type=F32)


def _const_spec(shape):
    nd = len(shape)
    return pl.BlockSpec(shape, lambda *_: (0,) * nd, pipeline_mode=pl.Buffered(1))


def _weight_slab_spec(w, layer, axis=0):
    block = list(w.shape[1:])
    block[axis] //= W_STEPS
    return pl.BlockSpec(
        (None, *block),
        lambda i, *_: (layer,) + tuple(
            jnp.minimum(i, W_STEPS - 1) if a == axis else 0 for a in range(len(block))))


def _store_weight_slab(step, src_ref, dst_ref, axis=0):
    rows = src_ref.shape[axis]
    at = pl.ds(pl.multiple_of(step * rows, rows), rows)
    idx = tuple(at if a == axis else slice(None) for a in range(len(src_ref.shape)))
    dst_ref[idx] = src_ref[...].astype(BF16)


def _unpack_pair(q):
    return jnp.concatenate(
        [pltpu.unpack_elementwise(q, index=0, packed_dtype=BF16, unpacked_dtype=F32),
         pltpu.unpack_elementwise(q, index=1, packed_dtype=BF16, unpacked_dtype=F32)], axis=-1)


def _pack_pair(v):
    half = v.shape[-1] // 2
    return pltpu.pack_elementwise([v[:, :half], v[:, half:]], packed_dtype=BF16)


def _to_tile_rows(v):
    t, w = v.shape
    nj = w // LANES
    parts = [v[:, j * LANES:(j + 1) * LANES].reshape(t // SUBLANES, 1, SUBLANES, LANES)
             for j in range(nj)]
    return jnp.concatenate(parts, axis=1).reshape(nj * t, LANES)


def _from_tile_rows(v2, nj):
    t = v2.shape[0] // nj
    v4 = v2.reshape(t // SUBLANES, nj, SUBLANES, LANES)
    return [v4[:, j].reshape(t, LANES) for j in range(nj)]


def _tile_row_offset(r, nj):
    hi = jnp.floor(r / SUBLANES)
    return hi * (SUBLANES * nj) + (r - hi * SUBLANES)


def _mixer_kernel(x_ref, *refs, has_moe_delta, tiles_per_seq):
    y_ref, refs = (refs[0], refs[1:]) if has_moe_delta else (None, refs)
    (g_ref, w_in32, caw_ref, cab_ref, lng_ref, lnb_ref, w_a_out32, w_pool32, pool_scale_ref,
     ccw_ref, w_c_out32, w_o32, o_ref, abuf, ubuf, vbuf, actbuf, pbuf,
     w_in_ref, w_a_out_ref, w_pool_ref, w_c_out_ref, w_o_ref) = refs
    step = pl.program_id(0)

    @pl.when(step < W_STEPS)
    def _():
        for src, dst in ((w_in32, w_in_ref), (w_a_out32, w_a_out_ref),
                         (w_c_out32, w_c_out_ref), (w_o32, w_o_ref)):
            _store_weight_slab(step, src, dst)
        _store_weight_slab(step, w_pool32, w_pool_ref, axis=1)

    @pl.when(step >= W_STEPS)
    def _():
        _mix_tile((step - W_STEPS) % tiles_per_seq, x_ref, y_ref, g_ref, w_in_ref, caw_ref,
                  cab_ref, lng_ref, lnb_ref, w_a_out_ref, w_pool_ref, pool_scale_ref, ccw_ref,
                  w_c_out_ref, w_o_ref, o_ref, abuf, ubuf, vbuf, actbuf, pbuf)


def _mix_tile(s, x_ref, y_ref, g_ref, w_in_ref, caw_ref, cab_ref, lng_ref, lnb_ref, w_a_out_ref,
              w_pool_ref, pool_scale_ref, ccw_ref, w_c_out_ref, w_o_ref,
              o_ref, abuf, ubuf, vbuf, actbuf, pbuf):
    ts = x_ref.shape[1]

    @pl.when(s == 0)
    def _():
        abuf[0, 0:HIST_A, :] = jnp.zeros((HIST_A, D_CONV), F32)
        ubuf[0:HIST_POOL, :] = jnp.zeros((HIST_POOL, D_POOL), F32)
        vbuf[0:HIST_C, :] = jnp.zeros((HIST_C, D_SC), F32)

    x = x_ref[0]
    if y_ref is not None:
        x = x + _unpack_pair(jnp.concatenate(_from_tile_rows(y_ref[...], H_TILES), axis=-1))
    h = _rmsnorm(x, g_ref[...]).astype(BF16)

    def proj(off, width):
        return _dot(h, w_in_ref[:, off:off + width])

    pa = proj(OFF_A, 2 * D_CONV)
    abuf[0, HIST_A:HIST_A + ts, :] = pa[:, :D_CONV] * _sigmoid(pa[:, D_CONV:])
    shifted_rows = HIST_A + ts - SUBLANES
    for r in range(1, SUBLANES):
        abuf[r, 0:shifted_rows, :] = abuf[0, r:r + shifted_rows, :]
    chunk3 = (CONV_ROW_CHUNK // SUBLANES, SUBLANES, D_CONV)
    for c0 in range(0, ts, CONV_ROW_CHUNK):
        acc = jnp.broadcast_to(cab_ref[...], chunk3)
        for k in range(CONV_A_K):
            q, r = divmod(HIST_A - (CONV_A_K - 1) + k, SUBLANES)
            lo = c0 + q * SUBLANES
            acc = acc + caw_ref[k] * abuf[r, lo:lo + CONV_ROW_CHUNK, :].reshape(chunk3)
        acc = acc.reshape(CONV_ROW_CHUNK, D_CONV)
        mu = jnp.mean(acc, axis=-1, keepdims=True)
        cen = acc - mu
        var = jnp.mean(cen * cen, axis=-1, keepdims=True)
        ln = cen * lax.rsqrt(var + EPS) * lng_ref[...] + lnb_ref[...]
        actbuf[c0:c0 + CONV_ROW_CHUNK, :] = (ln * _sigmoid(ln)).astype(BF16)
    abuf[0, 0:HIST_A, :] = abuf[0, ts:ts + HIST_A, :]

    pbuf[...] = proj(OFF_POOL, D_IN_PROJ - OFF_POOL)

    def pcol(off, width):
        return pbuf[:, off - OFF_POOL:off - OFF_POOL + width]

    pu = pcol(OFF_POOL, D_POOL)
    ubuf[HIST_POOL:HIST_POOL + ts, :] = pu
    merged = _sigmoid(pcol(OFF_G, D_MODEL)) * _dot(actbuf[...], w_a_out_ref[...])

    t_glob = s * ts + lax.broadcasted_iota(jnp.int32, (ts, 1), 0)
    p_parts = []
    for i, w in enumerate(POOL_WINDOWS):
        c0, c1 = i * POOL_GROUP_DIM, (i + 1) * POOL_GROUP_DIM
        tok = pu[:, c0:c1]
        win = ubuf[:, c0:c1]
        shift = 1
        while shift < w:
            win = win + pltpu.roll(win, shift, 0)
            shift *= 2
        win = win[HIST_POOL:]
        cnt = jnp.minimum(t_glob + 1, w).astype(F32)
        pin = (win / cnt - tok).astype(BF16)
        p_parts.append(_dot(pin, w_pool_ref[i]))
    ubuf[0:HIST_POOL, :] = ubuf[ts:ts + HIST_POOL, :]
    merged = merged + _sigmoid(pcol(OFF_G + D_MODEL, D_MODEL)) * (jnp.concatenate(p_parts, axis=-1) * pool_scale_ref[...])

    c_b = pcol(OFF_C + D_SC, D_SC)
    v = pcol(OFF_C + 2 * D_SC, D_SC) * pcol(OFF_C, D_SC)
    vbuf[HIST_C:HIST_C + ts, :] = v
    conv_c = ccw_ref[SC_K - 1:SC_K, :] * v
    for k in range(SC_K - 1):
        off = HIST_C - (SC_K - 1) + k
        conv_c = conv_c + ccw_ref[k:k + 1, :] * vbuf[off:off + ts, :]
    vbuf[0:HIST_C, :] = vbuf[ts:ts + HIST_C, :]
    merged = merged + _sigmoid(pcol(OFF_G + 2 * D_MODEL, D_MODEL)) * _dot((c_b * conv_c).astype(BF16), w_c_out_ref[...])

    o_ref[0] = x + _dot(merged.astype(BF16), w_o_ref[...])


def _mixer(x, moe_delta, layer, g, w_in, caw, cab, lng, lnb, w_a_out, w_pool, pool_scale, ccw,
           w_c_out, w_o):
    b, s, d = x.shape
    ts = SEQ_TILE
    nt = s // ts

    def tile(i):
        return jnp.maximum(i - W_STEPS, 0)

    def small(c):
        return _const_spec(c.shape)

    def slab_spec(w, axis):
        return _weight_slab_spec(w, layer, axis)

    delta = () if moe_delta is None else (moe_delta,)
    delta_specs = [pl.BlockSpec((H_TILES * ts, LANES), lambda i: (tile(i), 0))] * len(delta)
    x_spec = pl.BlockSpec((1, ts, d), lambda i: (tile(i) // nt, tile(i) % nt, 0))
    return pl.pallas_call(
        functools.partial(_mixer_kernel, has_moe_delta=moe_delta is not None, tiles_per_seq=nt),
        out_shape=jax.ShapeDtypeStruct(x.shape, F32),
        grid=(W_STEPS + b * nt,),
        in_specs=[x_spec] + delta_specs
        + [small(g), slab_spec(w_in, 0), small(caw), small(cab), small(lng), small(lnb),
           slab_spec(w_a_out, 0), slab_spec(w_pool, 1), small(pool_scale), small(ccw),
           slab_spec(w_c_out, 0), slab_spec(w_o, 0)],
        out_specs=x_spec,
        scratch_shapes=[pltpu.VMEM((SUBLANES, HIST_A + ts, D_CONV), F32),
                        pltpu.VMEM((HIST_POOL + ts, D_POOL), F32),
                        pltpu.VMEM((HIST_C + ts, D_SC), F32),
                        pltpu.VMEM((ts, D_CONV), BF16),
                        pltpu.VMEM((ts, D_IN_PROJ - OFF_POOL), F32)]
        + [pltpu.VMEM(w.shape[1:], BF16) for w in (w_in, w_a_out, w_pool, w_c_out, w_o)],
        compiler_params=pltpu.CompilerParams(
            dimension_semantics=("arbitrary",), vmem_limit_bytes=VMEM_LIMIT_BYTES),
        name="mixer",
    )(x, *delta, g, w_in, caw, cab, lng, lnb, w_a_out, w_pool, pool_scale, ccw, w_c_out, w_o)


def _route_rows(lt):
    def first_argmax(vals, vmax):
        idx = jnp.full(vmax.shape, len(vals) - 1, jnp.int32)
        for k in range(len(vals) - 2, -1, -1):
            idx = jnp.where(vals[k] == vmax, k, idx)
        return idx

    g = [lt[k:k + 1, :] for k in range(MOE_GROUPS)]
    gmax = functools.reduce(jnp.maximum, g)
    g_idx = first_argmax(g, gmax)
    g_val = 1.0 / sum(jnp.exp(v - gmax) for v in g)
    e = []
    for j in range(EXPERTS_PER_GROUP):
        ej = lt[MOE_GROUPS + j:MOE_GROUPS + j + 1, :]
        for grp in range(1, MOE_GROUPS):
            r = MOE_GROUPS + grp * EXPERTS_PER_GROUP + j
            ej = jnp.where(g_idx == grp, lt[r:r + 1, :], ej)
        e.append(ej)
    m1 = functools.reduce(jnp.maximum, e)
    i1 = first_argmax(e, m1)
    e2 = [jnp.where(i1 == j, -jnp.inf, e[j]) for j in range(EXPERTS_PER_GROUP)]
    m2 = functools.reduce(jnp.maximum, e2)
    i2 = first_argmax(e2, m2)
    esum = sum(jnp.exp(v - m1) for v in e)
    p1 = 1.0 / esum
    p2 = jnp.exp(m2 - m1) / esum
    w1 = g_val * (p1 / (p1 + p2))
    w2 = g_val * (p2 / (p1 + p2))
    lo = jnp.minimum(i1, i2)
    hi = jnp.maximum(i1, i2)
    pair = ((lo * (2 * EXPERTS_PER_GROUP - 1 - lo)) >> 1) + (hi - lo - 1)
    cls = g_idx * PAIRS_PER_GROUP + pair
    first_is_low = i1 < i2
    return cls, jnp.where(first_is_low, w1, w2), jnp.where(first_is_low, w2, w1)


def _xattn_kernel(x_ref, g_ref, w_q32, mem_ref, gm_ref, w_kv32, w_o32, gf_ref, w_r_ref, b_r_ref,
                  o_ref, hp_ref, cls_ref, xn_prev, w_q_ref, w_kv_ref, w_o_ref, kt_ref, v_ref,
                  *, tiles, tiles_per_seq):
    step = pl.program_id(0)

    @pl.when(step == 0)
    def _():
        xn_prev[...] = jnp.zeros(xn_prev.shape, F32)

    @pl.when(step < W_STEPS)
    def _():
        _store_weight_slab(step, w_q32, w_q_ref)
        _store_weight_slab(step, w_kv32, w_kv_ref)
        _store_weight_slab(step, w_o32, w_o_ref)

    @pl.when((step >= W_STEPS) & (step < W_STEPS + tiles)
             & ((step - W_STEPS) % tiles_per_seq == 0))
    def _():
        mn = _rmsnorm(mem_ref[0], gm_ref[...]).astype(BF16)
        kv = _dot(mn, w_kv_ref[...])
        kt_ref[...] = kv[:, :D_MODEL].T.astype(BF16)
        v_ref[...] = kv[:, D_MODEL:].astype(BF16)

    @pl.when(step >= W_STEPS)
    def _():
        _xattn_tile(x_ref, g_ref, w_q_ref, kt_ref, v_ref, w_o_ref, gf_ref, w_r_ref, b_r_ref,
                    o_ref, hp_ref, cls_ref, xn_prev)


def _xattn_tile(x_ref, g_ref, w_q_ref, kt_ref, v_ref, w_o_ref, gf_ref, w_r_ref, b_r_ref,
                o_ref, hp_ref, cls_ref, xn_prev):
    hf = _rmsnorm(xn_prev[...], gf_ref[...])
    h_hi = hf.astype(BF16)
    h_lo = (hf - h_hi.astype(F32)).astype(BF16)
    nt_dims = (((1,), (1,)), ((), ()))
    part = lax.dot_general(w_r_ref[...], h_hi, nt_dims, preferred_element_type=F32)
    lt = (part[:ROUTER_ROWS] + part[ROUTER_ROWS:]
          + lax.dot_general(w_r_ref[:ROUTER_ROWS, :], h_lo, nt_dims, preferred_element_type=F32)
          + b_r_ref[...])
    cls, wa, wb = _route_rows(lt)
    ts = cls.shape[1]
    for c in range(ts // LANES):
        cls_ref[0, c:c + 1, :] = cls[:, c * LANES:(c + 1) * LANES]
    row = lax.broadcasted_iota(jnp.int32, (LANES, ts), 0)

    def weight_lanes(w):
        w_hi = w.astype(BF16).astype(F32)
        return jnp.where(row == 0, w_hi, jnp.where(row == 1, w - w_hi, 0.0)).T

    wt_tile = pltpu.pack_elementwise([weight_lanes(wa), weight_lanes(wb)], packed_dtype=BF16)
    hp_ref[...] = _to_tile_rows(jnp.concatenate([_pack_pair(hf), wt_tile], axis=-1))

    x = x_ref[0]
    h = _rmsnorm(x, g_ref[...]).astype(BF16)
    q = (_dot(h, w_q_ref[...]) * (XATTN_HEAD_DIM ** -0.5)).astype(BF16)
    heads = []
    for hd in range(XATTN_HEADS):
        c0, c1 = hd * XATTN_HEAD_DIM, (hd + 1) * XATTN_HEAD_DIM
        sc = _dot(q[:, c0:c1], kt_ref[c0:c1, :])
        e = jnp.exp(sc - jnp.max(sc, axis=-1, keepdims=True))
        denom = jnp.sum(e, axis=-1, keepdims=True)
        heads.append(_dot(e.astype(BF16), v_ref[:, c0:c1]) / denom)
    o = jnp.concatenate(heads, axis=-1).astype(BF16)
    xn = x + _dot(o, w_o_ref[...])
    o_ref[0] = xn
    xn_prev[...] = xn


def _xattn(x, g, w_q, mem, g_mem, w_kv, w_o, g_ffn, w_r, b_r, layer):
    b, s, d = x.shape
    n_mem = mem.shape[1]
    ts = XATTN_SEQ_TILE
    nt = s // ts
    n = b * s
    tiles = b * nt

    def attn_tile(step):
        return jnp.clip(step - W_STEPS, 0, tiles - 1)

    def route_tile(step):
        return jnp.maximum(step - W_STEPS - 1, 0)

    return pl.pallas_call(
        functools.partial(_xattn_kernel, tiles=tiles, tiles_per_seq=nt),
        out_shape=(jax.ShapeDtypeStruct(x.shape, F32),
                   jax.ShapeDtypeStruct((ROW_TILES * n, LANES), jnp.uint32),
                   jax.ShapeDtypeStruct((n // ts, ts // LANES, LANES), jnp.int32)),
        grid=(W_STEPS + tiles + 1,),
        in_specs=[pl.BlockSpec((1, ts, d), lambda i: (attn_tile(i) // nt, attn_tile(i) % nt, 0)),
                  _const_spec(g.shape), _weight_slab_spec(w_q, layer),
                  pl.BlockSpec((1, n_mem, d), lambda i: (attn_tile(i) // nt, 0, 0)),
                  _const_spec(g_mem.shape), _weight_slab_spec(w_kv, layer),
                  _weight_slab_spec(w_o, layer), _const_spec(g_ffn.shape),
                  _const_spec(w_r.shape), _const_spec(b_r.shape)],
        out_specs=(pl.BlockSpec((1, ts, d), lambda i: (attn_tile(i) // nt, attn_tile(i) % nt, 0)),
                   pl.BlockSpec((ROW_TILES * ts, LANES), lambda i: (route_tile(i), 0)),
                   pl.BlockSpec((1, ts // LANES, LANES), lambda i: (route_tile(i), 0, 0))),
        scratch_shapes=[pltpu.VMEM((ts, d), F32), pltpu.VMEM(w_q.shape[1:], BF16),
                        pltpu.VMEM(w_kv.shape[1:], BF16), pltpu.VMEM(w_o.shape[1:], BF16),
                        pltpu.VMEM((d, n_mem), BF16), pltpu.VMEM((n_mem, d), BF16)],
        compiler_params=pltpu.CompilerParams(
            dimension_semantics=("arbitrary",), vmem_limit_bytes=VMEM_LIMIT_BYTES),
        name="xattn",
    )(x, g, w_q, mem, g_mem, w_kv, w_o, g_ffn, w_r, b_r)


def _plan_kernel(cls_ref, qs_ref, qy_ref, ea_ref, eb_ref, nu_ref):
    cls = cls_ref[...]
    nr, nl = cls.shape
    r = lax.broadcasted_iota(jnp.int32, (nl, nl), 0)
    c = lax.broadcasted_iota(jnp.int32, (nl, nl), 1)
    lanes_before = (r < c).astype(BF16)
    rr = lax.broadcasted_iota(jnp.int32, (nr, nr), 0)
    rc = lax.broadcasted_iota(jnp.int32, (nr, nr), 1)
    rows_before = (rc < rr).astype(BF16)
    ones_l = jnp.ones((nl, nl), BF16)
    ones_r = jnp.ones((nr, nr), BF16)
    n_tab = ea_ref.shape[1]
    tile_i = lax.broadcasted_iota(jnp.int32, (SUBLANES, n_tab), 1).astype(F32)
    run = jnp.zeros((nr, nl), F32)
    pos = jnp.zeros((nr, nl), F32)
    tile_cls = jnp.zeros((SUBLANES, n_tab), F32)
    for k in range(N_CLASSES):
        m = cls == k
        mb = jnp.where(m, 1.0, 0.0).astype(BF16)
        rowtot = _dot(mb, ones_l).astype(BF16)
        rank = _dot(mb, lanes_before) + _dot(rows_before, rowtot)
        total = _dot(ones_r, rowtot)
        pos = jnp.where(m, run * EXPERT_ROW_TILE + rank, pos)
        run = run + jnp.floor((total + (EXPERT_ROW_TILE - 1)) / EXPERT_ROW_TILE)
        end = jnp.concatenate([run[:SUBLANES, :]] * (n_tab // nl), axis=1)
        tile_cls = tile_cls + jnp.where(end <= tile_i, 1.0, 0.0)
    qs_ref[...] = _tile_row_offset(pos, ROW_TILES).astype(jnp.int32)
    qy_ref[...] = _tile_row_offset(pos, H_TILES).astype(jnp.int32)
    tc = jnp.minimum(tile_cls, N_CLASSES - 1.0).astype(jnp.int32)
    grp = sum(jnp.where(tc >= g * PAIRS_PER_GROUP, 1, 0) for g in range(1, MOE_GROUPS))
    pair = tc - grp * PAIRS_PER_GROUP
    pair_lo = jnp.where(pair >= 3, 1, 0) + jnp.where(pair >= 5, 1, 0)
    pair_hi = pair + 1 - jnp.where(pair >= 5, 3, jnp.where(pair >= 3, 2, 0))
    ea_ref[...] = grp * EXPERTS_PER_GROUP + pair_lo
    eb_ref[...] = grp * EXPERTS_PER_GROUP + pair_hi
    nu_ref[...] = run[:SUBLANES, :].astype(jnp.int32)


def _plan(cls2):
    nr, nl = cls2.shape
    n_tiles = _sorted_rows(nr * nl) // EXPERT_ROW_TILE
    n_tab = -(-n_tiles // nl) * nl
    vmem = pl.BlockSpec(memory_space=pltpu.VMEM)
    qs, qy, ea, eb, nu = pl.pallas_call(
        _plan_kernel,
        out_shape=(jax.ShapeDtypeStruct((nr, nl), jnp.int32),
                   jax.ShapeDtypeStruct((nr, nl), jnp.int32),
                   jax.ShapeDtypeStruct((SUBLANES, n_tab), jnp.int32),
                   jax.ShapeDtypeStruct((SUBLANES, n_tab), jnp.int32),
                   jax.ShapeDtypeStruct((SUBLANES, nl), jnp.int32)),
        in_specs=[vmem],
        out_specs=(vmem, vmem, vmem, vmem, vmem),
        name="moe_plan",
    )(cls2)
    return (qs.reshape(nr * nl), qy.reshape(nr * nl),
            ea[0, :n_tiles], eb[0, :n_tiles], nu[0, :1])


def _scatter_kernel(qs_ref, hp_ref, s_ref):
    i = pl.program_id(0)
    tn = hp_ref.shape[0] // ROW_TILES
    group = SUBLANES * ROW_TILES

    @pl.when(i == 0)
    def _():
        zero_rows = _pack_pair(jnp.zeros((hp_ref.shape[0], 2 * LANES), F32))

        def fill(c, carry):
            start = pl.multiple_of(c * hp_ref.shape[0], SUBLANES)
            s_ref[pl.ds(start, hp_ref.shape[0]), :] = zero_rows
            return carry

        lax.fori_loop(0, s_ref.shape[0] // hp_ref.shape[0], fill, 0)

    def body(j, carry):
        src = pl.multiple_of(j * group, SUBLANES)
        for k in range(SUBLANES):
            q = qs_ref[i * tn + j * SUBLANES + k]
            s_ref[pl.ds(q, ROW_TILES, stride=SUBLANES), :] = (
                hp_ref[pl.ds(src + k, ROW_TILES, stride=SUBLANES), :])
        return carry

    lax.fori_loop(0, tn // SUBLANES, body, 0)


def _scatter(qs, hp):
    n = qs.shape[0]
    tn = SCATTER_TILE
    rows = ROW_TILES * _sorted_rows(n)
    grid_spec = pltpu.PrefetchScalarGridSpec(
        num_scalar_prefetch=1,
        grid=(n // tn,),
        in_specs=[pl.BlockSpec((ROW_TILES * tn, LANES), lambda i, qs: (i, 0))],
        out_specs=pl.BlockSpec((rows, LANES), lambda i, qs: (0, 0), pipeline_mode=pl.Buffered(1)),
    )
    return pl.pallas_call(
        _scatter_kernel,
        out_shape=jax.ShapeDtypeStruct((rows, LANES), jnp.uint32),
        grid_spec=grid_spec,
        compiler_params=pltpu.CompilerParams(
            dimension_semantics=("arbitrary",), vmem_limit_bytes=VMEM_LIMIT_RESIDENT_BYTES),
        name="moe_scatter",
    )(qs, hp)


def _expert_kernel(ea_ref, eb_ref, nu_ref, s_ref, wg32_ref, wu32_ref, wd32_ref, y_ref,
                   wg_ref, wu_ref, wd_ref):
    step = pl.program_id(0)

    @pl.when(step < N_EXPERTS)
    def _():
        wg_ref[step] = wg32_ref[...].astype(BF16)
        wu_ref[step] = wu32_ref[...].astype(BF16)
        wd_ref[step] = wd32_ref[...].astype(BF16)

    first = (step - N_EXPERTS) * EXPERT_STEP_TILES
    rec = EXPERT_ROW_TILE * ROW_TILES
    out = EXPERT_ROW_TILE * H_TILES

    @pl.when((step >= N_EXPERTS) & (first < nu_ref[0]))
    def _():
        acts = []
        for j in range(EXPERT_STEP_TILES):
            t = first + j
            tiles = _from_tile_rows(s_ref[j * rec:(j + 1) * rec, :], ROW_TILES)
            hb = _unpack_pair(jnp.concatenate(tiles[:H_TILES], axis=-1)).astype(BF16)
            for e_ref, idx in ((ea_ref, 0), (eb_ref, 1)):
                e = e_ref[t]
                wt = pltpu.unpack_elementwise(tiles[H_TILES], index=idx, packed_dtype=BF16,
                                              unpacked_dtype=F32)
                wt = wt[:, 0:1] + wt[:, 1:2]
                gt = _dot(hb, wg_ref[e])
                acts.append((gt * _sigmoid(gt) * _dot(hb, wu_ref[e]) * wt).astype(BF16))
        for j in range(EXPERT_STEP_TILES):
            t = first + j
            y = (_dot(acts[2 * j], wd_ref[ea_ref[t]])
                 + _dot(acts[2 * j + 1], wd_ref[eb_ref[t]]))
            y_ref[j * out:(j + 1) * out, :] = _to_tile_rows(_pack_pair(y))

    @pl.when((step >= N_EXPERTS) & (first >= nu_ref[0]))
    def _():
        y_ref[...] = _pack_pair(jnp.zeros((y_ref.shape[0], 2 * LANES), F32))


def _experts(tile_ea, tile_eb, n_used, s, wg, wu, wd, layer):
    rows = s.shape[0] // ROW_TILES
    tb = EXPERT_ROW_TILE * EXPERT_STEP_TILES

    def row_block(i, *_):
        return (jnp.maximum(i - N_EXPERTS, 0), 0)

    def expert_block(i, *_):
        return (layer, jnp.minimum(i, N_EXPERTS - 1), 0, 0)

    grid_spec = pltpu.PrefetchScalarGridSpec(
        num_scalar_prefetch=3,
        grid=(N_EXPERTS + rows // tb,),
        in_specs=[pl.BlockSpec((ROW_TILES * tb, LANES), row_block),
                  pl.BlockSpec((None, None) + wg.shape[2:], expert_block),
                  pl.BlockSpec((None, None) + wu.shape[2:], expert_block),
                  pl.BlockSpec((None, None) + wd.shape[2:], expert_block)],
        out_specs=pl.BlockSpec((H_TILES * tb, LANES), row_block),
        scratch_shapes=[pltpu.VMEM(wg.shape[1:], BF16), pltpu.VMEM(wu.shape[1:], BF16),
                        pltpu.VMEM(wd.shape[1:], BF16)],
    )
    return pl.pallas_call(
        _expert_kernel,
        out_shape=jax.ShapeDtypeStruct((H_TILES * rows, LANES), jnp.uint32),
        grid_spec=grid_spec,
        compiler_params=pltpu.CompilerParams(
            dimension_semantics=("arbitrary",), vmem_limit_bytes=VMEM_LIMIT_BYTES),
        name="moe_experts",
    )(tile_ea, tile_eb, n_used, s, wg, wu, wd)


def _unsort_kernel(qy_ref, y_ref, o_ref):
    i = pl.program_id(0)
    tn = o_ref.shape[0] // H_TILES
    group = SUBLANES * H_TILES

    def body(j, carry):
        dst = pl.multiple_of(j * group, SUBLANES)
        for k in range(SUBLANES):
            q = qy_ref[i * tn + j * SUBLANES + k]
            o_ref[pl.ds(dst + k, H_TILES, stride=SUBLANES), :] = (
                y_ref[pl.ds(q, H_TILES, stride=SUBLANES), :])
        return carry

    lax.fori_loop(0, tn // SUBLANES, body, 0)


def _unsort(qy, y):
    n = qy.shape[0]
    tn = UNSORT_TILE
    grid_spec = pltpu.PrefetchScalarGridSpec(
        num_scalar_prefetch=1,
        grid=(n // tn,),
        in_specs=[pl.BlockSpec(y.shape, lambda i, qy: (0, 0), pipeline_mode=pl.Buffered(1))],
        out_specs=pl.BlockSpec((H_TILES * tn, LANES), lambda i, qy: (i, 0)),
    )
    return pl.pallas_call(
        _unsort_kernel,
        out_shape=jax.ShapeDtypeStruct((H_TILES * n, LANES), jnp.uint32),
        grid_spec=grid_spec,
        compiler_params=pltpu.CompilerParams(
            dimension_semantics=("arbitrary",), vmem_limit_bytes=VMEM_LIMIT_RESIDENT_BYTES),
        name="moe_unsort",
    )(qy, y)


def _gather_kernel(qy_ref, x_ref, y_ref, gn_ref, o_ref, rows_ref):
    i = pl.program_id(0)
    tn = x_ref.shape[0]
    group = SUBLANES * H_TILES

    def body(j, carry):
        dst = pl.multiple_of(j * group, SUBLANES)
        for k in range(SUBLANES):
            q = qy_ref[i * tn + j * SUBLANES + k]
            rows_ref[pl.ds(dst + k, H_TILES, stride=SUBLANES), :] = (
                y_ref[pl.ds(q, H_TILES, stride=SUBLANES), :])
        return carry

    lax.fori_loop(0, tn // SUBLANES, body, 0)
    packed = jnp.concatenate(_from_tile_rows(rows_ref[...], H_TILES), axis=-1)
    o_ref[...] = _rmsnorm(x_ref[...] + _unpack_pair(packed), gn_ref[...])


def _gather(qy, x2, y, g_final):
    n, d = x2.shape
    tn = GATHER_TILE
    grid_spec = pltpu.PrefetchScalarGridSpec(
        num_scalar_prefetch=1,
        grid=(n // tn,),
        in_specs=[pl.BlockSpec((tn, d), lambda i, qy: (i, 0)),
                  pl.BlockSpec(y.shape, lambda i, qy: (0, 0), pipeline_mode=pl.Buffered(1)),
                  pl.BlockSpec(g_final.shape, lambda i, qy: (0, 0), pipeline_mode=pl.Buffered(1))],
        out_specs=pl.BlockSpec((tn, d), lambda i, qy: (i, 0)),
        scratch_shapes=[pltpu.VMEM((H_TILES * tn, LANES), jnp.uint32)],
    )
    return pl.pallas_call(
        _gather_kernel,
        out_shape=jax.ShapeDtypeStruct(x2.shape, F32),
        grid_spec=grid_spec,
        compiler_params=pltpu.CompilerParams(
            dimension_semantics=("arbitrary",), vmem_limit_bytes=VMEM_LIMIT_RESIDENT_BYTES),
        name="moe_gather",
    )(qy, x2, y, g_final)


def _row(v):
    return v.reshape(1, -1)


def _router_params(w_rg, b_rg, w_re, b_re):
    pad = ROUTER_ROWS - MOE_GROUPS - N_EXPERTS
    wt = jnp.pad(jnp.concatenate([w_rg, w_re], axis=1).T, ((0, pad), (0, 0)))
    w_hi = wt.astype(BF16)
    w_lo = (wt - w_hi.astype(F32)).astype(BF16)
    b_col = jnp.pad(jnp.concatenate([b_rg, b_re]), (0, pad)).reshape(ROUTER_ROWS, 1)
    return jnp.concatenate([w_hi, w_lo], axis=0), b_col


def kernel(x, mem, norm_mix_g, w_in, conv_a_w, conv_a_b, ln_a_g, ln_a_b, w_a_out, w_pool_grp, pool_scale, conv_c_w, w_c_out, w_o, norm_x_g, norm_mem_g, w_xq, w_xkv, w_xo, norm_ffn_g, w_rg, b_rg, w_re, b_re, w_e_gate, w_e_up, w_e_down, norm_f_g):
    bsz, seq, d = x.shape
    n = bsz * seq
    depth = w_in.shape[0]
    moe_delta = None
    for l in range(depth):
        caw = jnp.broadcast_to(conv_a_w[l][:, None, :], (CONV_A_K, SUBLANES, D_CONV))
        x = _mixer(x, moe_delta, l, _row(norm_mix_g[l]), w_in, caw, _row(conv_a_b[l]),
                   _row(ln_a_g[l]), _row(ln_a_b[l]), w_a_out, w_pool_grp, _row(pool_scale[l]),
                   conv_c_w[l], w_c_out, w_o)
        w_r, b_r = _router_params(w_rg[l], b_rg[l], w_re[l], b_re[l])
        x, hp, cls = _xattn(x, _row(norm_x_g[l]), w_xq, mem, _row(norm_mem_g[l]), w_xkv, w_xo,
                            _row(norm_ffn_g[l]), w_r, b_r, l)
        qs, qy, tile_ea, tile_eb, n_used = _plan(cls.reshape(n // LANES, LANES))
        y = _experts(tile_ea, tile_eb, n_used, _scatter(qs, hp), w_e_gate, w_e_up, w_e_down, l)
        if l < depth - 1:
            moe_delta = _unsort(qy, y)
    return _gather(qy, x.reshape(n, d), y, _row(norm_f_g)).reshape(bsz, seq, d)
```

```python
import functools

import jax
import jax.numpy as jnp
from jax import lax
from jax.experimental import pallas as pl
from jax.experimental.pallas import tpu as pltpu

D_MODEL = 1024
D_CONV = D_MODEL // 2
D_POOL = D_MODEL // 2
D_SC = D_MODEL // 2
CONV_A_K = 31
SC_K = 3
POOL_WINDOWS = (2, 4, 8, 16)
POOL_GROUP_DIM = D_POOL // len(POOL_WINDOWS)
XATTN_HEADS = 4
XATTN_HEAD_DIM = D_MODEL // XATTN_HEADS
MOE_GROUPS = 4
EXPERTS_PER_GROUP = 4
N_EXPERTS = MOE_GROUPS * EXPERTS_PER_GROUP
EPS = 1e-6

OFF_A = 0
OFF_POOL = 2 * D_CONV
OFF_C = OFF_POOL + D_POOL
OFF_G = OFF_C + 3 * D_SC
D_IN_PROJ = OFF_G + 3 * D_MODEL

SUBLANES = 8
LANES = 128
VMEM_LIMIT_BYTES = 56 * 1024 * 1024

HIST_A = 32
HIST_POOL = 16
HIST_C = 8

SEQ_TILE = 512
XATTN_SEQ_TILE = 1024
CONV_ROW_CHUNK = 16
W_STEPS = 8

PAIRS_PER_GROUP = EXPERTS_PER_GROUP * (EXPERTS_PER_GROUP - 1) // 2
N_CLASSES = MOE_GROUPS * PAIRS_PER_GROUP
EXPERT_ROW_TILE = 128
EXPERT_STEP_TILES = 8
ROUTER_ROWS = 32
SCATTER_TILE = 1024
UNSORT_TILE = 2048
GATHER_TILE = 512
VMEM_LIMIT_RESIDENT_BYTES = 58 * 1024 * 1024
H_TILES = D_MODEL // 2 // LANES
ROW_TILES = H_TILES + 1

BF16 = jnp.bfloat16
F32 = jnp.float32


def _sorted_rows(n_tokens):
    return n_tokens + N_CLASSES * EXPERT_ROW_TILE


def _sigmoid(v):
    return 0.5 * jnp.tanh(0.5 * v) + 0.5


def _rmsnorm(xf, g):
    return xf * lax.rsqrt(jnp.mean(xf * xf, axis=-1, keepdims=True) + EPS) * g


def _dot(a, b):
    return jnp.dot(a, b, preferred_element_type=F32)


def _const_spec(shape):
    nd = len(shape)
    return pl.BlockSpec(shape, lambda *_: (0,) * nd, pipeline_mode=pl.Buffered(1))


def _weight_slab_spec(w, layer, axis=0):
    block = list(w.shape[1:])
    block[axis] //= W_STEPS
    return pl.BlockSpec(
        (None, *block),
        lambda i, *_: (layer,) + tuple(
            jnp.minimum(i, W_STEPS - 1) if a == axis else 0 for a in range(len(block))))


def _store_weight_slab(step, src_ref, dst_ref, axis=0):
    rows = src_ref.shape[axis]
    at = pl.ds(pl.multiple_of(step * rows, rows), rows)
    idx = tuple(at if a == axis else slice(None) for a in range(len(src_ref.shape)))
    dst_ref[idx] = src_ref[...].astype(BF16)


def _unpack_pair(q):
    return jnp.concatenate(
        [pltpu.unpack_elementwise(q, index=0, packed_dtype=BF16, unpacked_dtype=F32),
         pltpu.unpack_elementwise(q, index=1, packed_dtype=BF16, unpacked_dtype=F32)], axis=-1)


def _pack_pair(v):
    half = v.shape[-1] // 2
    return pltpu.pack_elementwise([v[:, :half], v[:, half:]], packed_dtype=BF16)


def _to_tile_rows(v):
    t, w = v.shape
    nj = w // LANES
    parts = [v[:, j * LANES:(j + 1) * LANES].reshape(t // SUBLANES, 1, SUBLANES, LANES)
             for j in range(nj)]
    return jnp.concatenate(parts, axis=1).reshape(nj * t, LANES)


def _from_tile_rows(v2, nj):
    t = v2.shape[0] // nj
    v4 = v2.reshape(t // SUBLANES, nj, SUBLANES, LANES)
    return [v4[:, j].reshape(t, LANES) for j in range(nj)]


def _tile_row_offset(r, nj):
    hi = jnp.floor(r / SUBLANES)
    return hi * (SUBLANES * nj) + (r - hi * SUBLANES)


def _mixer_kernel(x_ref, *refs, has_moe_delta, tiles_per_seq):
    y_ref, refs = (refs[0], refs[1:]) if has_moe_delta else (None, refs)
    (g_ref, w_in32, caw_ref, cab_ref, lng_ref, lnb_ref, w_a_out32, w_pool32, pool_scale_ref,
     ccw_ref, w_c_out32, w_o32, o_ref, abuf, ubuf, vbuf, actbuf, pbuf,
     w_in_ref, w_a_out_ref, w_pool_ref, w_c_out_ref, w_o_ref) = refs
    step = pl.program_id(0)

    @pl.when(step < W_STEPS)
    def _():
        for src, dst in ((w_in32, w_in_ref), (w_a_out32, w_a_out_ref),
                         (w_c_out32, w_c_out_ref), (w_o32, w_o_ref)):
            _store_weight_slab(step, src, dst)
        _store_weight_slab(step, w_pool32, w_pool_ref, axis=1)

    @pl.when(step >= W_STEPS)
    def _():
        _mix_tile((step - W_STEPS) % tiles_per_seq, x_ref, y_ref, g_ref, w_in_ref, caw_ref,
                  cab_ref, lng_ref, lnb_ref, w_a_out_ref, w_pool_ref, pool_scale_ref, ccw_ref,
                  w_c_out_ref, w_o_ref, o_ref, abuf, ubuf, vbuf, actbuf, pbuf)


def _mix_tile(s, x_ref, y_ref, g_ref, w_in_ref, caw_ref, cab_ref, lng_ref, lnb_ref, w_a_out_ref,
              w_pool_ref, pool_scale_ref, ccw_ref, w_c_out_ref, w_o_ref,
              o_ref, abuf, ubuf, vbuf, actbuf, pbuf):
    ts = x_ref.shape[1]

    @pl.when(s == 0)
    def _():
        abuf[0, 0:HIST_A, :] = jnp.zeros((HIST_A, D_CONV), F32)
        ubuf[0:HIST_POOL, :] = jnp.zeros((HIST_POOL, D_POOL), F32)
        vbuf[0:HIST_C, :] = jnp.zeros((HIST_C, D_SC), F32)

    x = x_ref[0]
    if y_ref is not None:
        x = x + _unpack_pair(jnp.concatenate(_from_tile_rows(y_ref[...], H_TILES), axis=-1))
    h = _rmsnorm(x, g_ref[...]).astype(BF16)

    def proj(off, width):
        return _dot(h, w_in_ref[:, off:off + width])

    pa = proj(OFF_A, 2 * D_CONV)
    abuf[0, HIST_A:HIST_A + ts, :] = pa[:, :D_CONV] * _sigmoid(pa[:, D_CONV:])
    shifted_rows = HIST_A + ts - SUBLANES
    for r in range(1, SUBLANES):
        abuf[r, 0:shifted_rows, :] = abuf[0, r:r + shifted_rows, :]
    chunk3 = (CONV_ROW_CHUNK // SUBLANES, SUBLANES, D_CONV)
    for c0 in range(0, ts, CONV_ROW_CHUNK):
        acc = jnp.broadcast_to(cab_ref[...], chunk3)
        for k in range(CONV_A_K):
            q, r = divmod(HIST_A - (CONV_A_K - 1) + k, SUBLANES)
            lo = c0 + q * SUBLANES
            acc = acc + caw_ref[k] * abuf[r, lo:lo + CONV_ROW_CHUNK, :].reshape(chunk3)
        acc = acc.reshape(CONV_ROW_CHUNK, D_CONV)
        mu = jnp.mean(acc, axis=-1, keepdims=True)
        cen = acc - mu
        var = jnp.mean(cen * cen, axis=-1, keepdims=True)
        ln = cen * lax.rsqrt(var + EPS) * lng_ref[...] + lnb_ref[...]
        actbuf[c0:c0 + CONV_ROW_CHUNK, :] = (ln * _sigmoid(ln)).astype(BF16)
    abuf[0, 0:HIST_A, :] = abuf[0, ts:ts + HIST_A, :]

    pbuf[...] = proj(OFF_POOL, D_IN_PROJ - OFF_POOL)

    def pcol(off, width):
        return pbuf[:, off - OFF_POOL:off - OFF_POOL + width]

    pu = pcol(OFF_POOL, D_POOL)
    ubuf[HIST_POOL:HIST_POOL + ts, :] = pu
    merged = _sigmoid(pcol(OFF_G, D_MODEL)) * _dot(actbuf[...], w_a_out_ref[...])

    t_glob = s * ts + lax.broadcasted_iota(jnp.int32, (ts, 1), 0)
    p_parts = []
    for i, w in enumerate(POOL_WINDOWS):
        c0, c1 = i * POOL_GROUP_DIM, (i + 1) * POOL_GROUP_DIM
        tok = pu[:, c0:c1]
        win = ubuf[:, c0:c1]
        shift = 1
        while shift < w:
            win = win + pltpu.roll(win, shift, 0)
            shift *= 2
        win = win[HIST_POOL:]
        cnt = jnp.minimum(t_glob + 1, w).astype(F32)
        pin = (win / cnt - tok).astype(BF16)
        p_parts.append(_dot(pin, w_pool_ref[i]))
    ubuf[0:HIST_POOL, :] = ubuf[ts:ts + HIST_POOL, :]
    merged = merged + _sigmoid(pcol(OFF_G + D_MODEL, D_MODEL)) * (jnp.concatenate(p_parts, axis=-1) * pool_scale_ref[...])

    c_b = pcol(OFF_C + D_SC, D_SC)
    v = pcol(OFF_C + 2 * D_SC, D_SC) * pcol(OFF_C, D_SC)
    vbuf[HIST_C:HIST_C + ts, :] = v
    conv_c = ccw_ref[SC_K - 1:SC_K, :] * v
    for k in range(SC_K - 1):
        off = HIST_C - (SC_K - 1) + k
        conv_c = conv_c + ccw_ref[k:k + 1, :] * vbuf[off:off + ts, :]
    vbuf[0:HIST_C, :] = vbuf[ts:ts + HIST_C, :]
    merged = merged + _sigmoid(pcol(OFF_G + 2 * D_MODEL, D_MODEL)) * _dot((c_b * conv_c).astype(BF16), w_c_out_ref[...])

    o_ref[0] = x + _dot(merged.astype(BF16), w_o_ref[...])


def _mixer(x, moe_delta, layer, g, w_in, caw, cab, lng, lnb, w_a_out, w_pool, pool_scale, ccw,
           w_c_out, w_o):
    b, s, d = x.shape
    ts = SEQ_TILE
    nt = s // ts

    def tile(i):
        return jnp.maximum(i - W_STEPS, 0)

    def small(c):
        return _const_spec(c.shape)

    def slab_spec(w, axis):
        return _weight_slab_spec(w, layer, axis)

    delta = () if moe_delta is None else (moe_delta,)
    delta_specs = [pl.BlockSpec((H_TILES * ts, LANES), lambda i: (tile(i), 0))] * len(delta)
    x_spec = pl.BlockSpec((1, ts, d), lambda i: (tile(i) // nt, tile(i) % nt, 0))
    return pl.pallas_call(
        functools.partial(_mixer_kernel, has_moe_delta=moe_delta is not None, tiles_per_seq=nt),
        out_shape=jax.ShapeDtypeStruct(x.shape, F32),
        grid=(W_STEPS + b * nt,),
        in_specs=[x_spec] + delta_specs
        + [small(g), slab_spec(w_in, 0), small(caw), small(cab), small(lng), small(lnb),
           slab_spec(w_a_out, 0), slab_spec(w_pool, 1), small(pool_scale), small(ccw),
           slab_spec(w_c_out, 0), slab_spec(w_o, 0)],
        out_specs=x_spec,
        scratch_shapes=[pltpu.VMEM((SUBLANES, HIST_A + ts, D_CONV), F32),
                        pltpu.VMEM((HIST_POOL + ts, D_POOL), F32),
                        pltpu.VMEM((HIST_C + ts, D_SC), F32),
                        pltpu.VMEM((ts, D_CONV), BF16),
                        pltpu.VMEM((ts, D_IN_PROJ - OFF_POOL), F32)]
        + [pltpu.VMEM(w.shape[1:], BF16) for w in (w_in, w_a_out, w_pool, w_c_out, w_o)],
        compiler_params=pltpu.CompilerParams(
            dimension_semantics=("arbitrary",), vmem_limit_bytes=VMEM_LIMIT_BYTES),
        name="mixer",
    )(x, *delta, g, w_in, caw, cab, lng, lnb, w_a_out, w_pool, pool_scale, ccw, w_c_out, w_o)


def _route_rows(lt):
    def first_argmax(vals, vmax):
        idx = jnp.full(vmax.shape, len(vals) - 1, jnp.int32)
        for k in range(len(vals) - 2, -1, -1):
            idx = jnp.where(vals[k] == vmax, k, idx)
        return idx

    g = [lt[k:k + 1, :] for k in range(MOE_GROUPS)]
    gmax = functools.reduce(jnp.maximum, g)
    g_idx = first_argmax(g, gmax)
    g_val = 1.0 / sum(jnp.exp(v - gmax) for v in g)
    e = []
    for j in range(EXPERTS_PER_GROUP):
        ej = lt[MOE_GROUPS + j:MOE_GROUPS + j + 1, :]
        for grp in range(1, MOE_GROUPS):
            r = MOE_GROUPS + grp * EXPERTS_PER_GROUP + j
            ej = jnp.where(g_idx == grp, lt[r:r + 1, :], ej)
        e.append(ej)
    m1 = functools.reduce(jnp.maximum, e)
    i1 = first_argmax(e, m1)
    e2 = [jnp.where(i1 == j, -jnp.inf, e[j]) for j in range(EXPERTS_PER_GROUP)]
    m2 = functools.reduce(jnp.maximum, e2)
    i2 = first_argmax(e2, m2)
    esum = sum(jnp.exp(v - m1) for v in e)
    p1 = 1.0 / esum
    p2 = jnp.exp(m2 - m1) / esum
    w1 = g_val * (p1 / (p1 + p2))
    w2 = g_val * (p2 / (p1 + p2))
    lo = jnp.minimum(i1, i2)
    hi = jnp.maximum(i1, i2)
    pair = ((lo * (2 * EXPERTS_PER_GROUP - 1 - lo)) >> 1) + (hi - lo - 1)
    cls = g_idx * PAIRS_PER_GROUP + pair
    first_is_low = i1 < i2
    return cls, jnp.where(first_is_low, w1, w2), jnp.where(first_is_low, w2, w1)


def _xattn_kernel(x_ref, g_ref, w_q32, mem_ref, gm_ref, w_kv32, w_o32, gf_ref, w_r_ref, b_r_ref,
                  o_ref, hp_ref, cls_ref, xn_prev, w_q_ref, w_kv_ref, w_o_ref, kt_ref, v_ref,
                  *, tiles, tiles_per_seq):
    step = pl.program_id(0)

    @pl.when(step == 0)
    def _():
        xn_prev[...] = jnp.zeros(xn_prev.shape, F32)

    @pl.when(step < W_STEPS)
    def _():
        _store_weight_slab(step, w_q32, w_q_ref)
        _store_weight_slab(step, w_kv32, w_kv_ref)
        _store_weight_slab(step, w_o32, w_o_ref)

    @pl.when((step >= W_STEPS) & (step < W_STEPS + tiles)
             & ((step - W_STEPS) % tiles_per_seq == 0))
    def _():
        mn = _rmsnorm(mem_ref[0], gm_ref[...]).astype(BF16)
        kv = _dot(mn, w_kv_ref[...])
        kt_ref[...] = kv[:, :D_MODEL].T.astype(BF16)
        v_ref[...] = kv[:, D_MODEL:].astype(BF16)

    @pl.when(step >= W_STEPS)
    def _():
        _xattn_tile(x_ref, g_ref, w_q_ref, kt_ref, v_ref, w_o_ref, gf_ref, w_r_ref, b_r_ref,
                    o_ref, hp_ref, cls_ref, xn_prev)


def _xattn_tile(x_ref, g_ref, w_q_ref, kt_ref, v_ref, w_o_ref, gf_ref, w_r_ref, b_r_ref,
                o_ref, hp_ref, cls_ref, xn_prev):
    hf = _rmsnorm(xn_prev[...], gf_ref[...])
    h_hi = hf.astype(BF16)
    h_lo = (hf - h_hi.astype(F32)).astype(BF16)
    nt_dims = (((1,), (1,)), ((), ()))
    part = lax.dot_general(w_r_ref[...], h_hi, nt_dims, preferred_element_type=F32)
    lt = (part[:ROUTER_ROWS] + part[ROUTER_ROWS:]
          + lax.dot_general(w_r_ref[:ROUTER_ROWS, :], h_lo, nt_dims, preferred_element_type=F32)
          + b_r_ref[...])
    cls, wa, wb = _route_rows(lt)
    ts = cls.shape[1]
    for c in range(ts // LANES):
        cls_ref[0, c:c + 1, :] = cls[:, c * LANES:(c + 1) * LANES]
    row = lax.broadcasted_iota(jnp.int32, (LANES, ts), 0)

    def weight_lanes(w):
        w_hi = w.astype(BF16).astype(F32)
        return jnp.where(row == 0, w_hi, jnp.where(row == 1, w - w_hi, 0.0)).T

    wt_tile = pltpu.pack_elementwise([weight_lanes(wa), weight_lanes(wb)], packed_dtype=BF16)
    hp_ref[...] = _to_tile_rows(jnp.concatenate([_pack_pair(hf), wt_tile], axis=-1))

    x = x_ref[0]
    h = _rmsnorm(x, g_ref[...]).astype(BF16)
    q = (_dot(h, w_q_ref[...]) * (XATTN_HEAD_DIM ** -0.5)).astype(BF16)
    heads = []
    for hd in range(XATTN_HEADS):
        c0, c1 = hd * XATTN_HEAD_DIM, (hd + 1) * XATTN_HEAD_DIM
        sc = _dot(q[:, c0:c1], kt_ref[c0:c1, :])
        e = jnp.exp(sc - jnp.max(sc, axis=-1, keepdims=True))
        denom = jnp.sum(e, axis=-1, keepdims=True)
        heads.append(_dot(e.astype(BF16), v_ref[:, c0:c1]) / denom)
    o = jnp.concatenate(heads, axis=-1).astype(BF16)
    xn = x + _dot(o, w_o_ref[...])
    o_ref[0] = xn
    xn_prev[...] = xn


def _xattn(x, g, w_q, mem, g_mem, w_kv, w_o, g_ffn, w_r, b_r, layer):
    b, s, d = x.shape
    n_mem = mem.shape[1]
    ts = XATTN_SEQ_TILE
    nt = s // ts
    n = b * s
    tiles = b * nt

    def attn_tile(step):
        return jnp.clip(step - W_STEPS, 0, tiles - 1)

    def route_tile(step):
        return jnp.maximum(step - W_STEPS - 1, 0)

    return pl.pallas_call(
        functools.partial(_xattn_kernel, tiles=tiles, tiles_per_seq=nt),
        out_shape=(jax.ShapeDtypeStruct(x.shape, F32),
                   jax.ShapeDtypeStruct((ROW_TILES * n, LANES), jnp.uint32),
                   jax.ShapeDtypeStruct((n // ts, ts // LANES, LANES), jnp.int32)),
        grid=(W_STEPS + tiles + 1,),
        in_specs=[pl.BlockSpec((1, ts, d), lambda i: (attn_tile(i) // nt, attn_tile(i) % nt, 0)),
                  _const_spec(g.shape), _weight_slab_spec(w_q, layer),
                  pl.BlockSpec((1, n_mem, d), lambda i: (attn_tile(i) // nt, 0, 0)),
                  _const_spec(g_mem.shape), _weight_slab_spec(w_kv, layer),
                  _weight_slab_spec(w_o, layer), _const_spec(g_ffn.shape),
                  _const_spec(w_r.shape), _const_spec(b_r.shape)],
        out_specs=(pl.BlockSpec((1, ts, d), lambda i: (attn_tile(i) // nt, attn_tile(i) % nt, 0)),
                   pl.BlockSpec((ROW_TILES * ts, LANES), lambda i: (route_tile(i), 0)),
                   pl.BlockSpec((1, ts // LANES, LANES), lambda i: (route_tile(i), 0, 0))),
        scratch_shapes=[pltpu.VMEM((ts, d), F32), pltpu.VMEM(w_q.shape[1:], BF16),
                        pltpu.VMEM(w_kv.shape[1:], BF16), pltpu.VMEM(w_o.shape[1:], BF16),
                        pltpu.VMEM((d, n_mem), BF16), pltpu.VMEM((n_mem, d), BF16)],
        compiler_params=pltpu.CompilerParams(
            dimension_semantics=("arbitrary",), vmem_limit_bytes=VMEM_LIMIT_BYTES),
        name="xattn",
    )(x, g, w_q, mem, g_mem, w_kv, w_o, g_ffn, w_r, b_r)


def _plan_kernel(cls_ref, qs_ref, qy_ref, ea_ref, eb_ref, nu_ref):
    cls = cls_ref[...]
    nr, nl = cls.shape
    r = lax.broadcasted_iota(jnp.int32, (nl, nl), 0)
    c = lax.broadcasted_iota(jnp.int32, (nl, nl), 1)
    lanes_before = (r < c).astype(BF16)
    rr = lax.broadcasted_iota(jnp.int32, (nr, nr), 0)
    rc = lax.broadcasted_iota(jnp.int32, (nr, nr), 1)
    rows_before = (rc < rr).astype(BF16)
    ones_l = jnp.ones((nl, nl), BF16)
    ones_r = jnp.ones((nr, nr), BF16)
    n_tab = ea_ref.shape[1]
    tile_i = lax.broadcasted_iota(jnp.int32, (SUBLANES, n_tab), 1).astype(F32)
    run = jnp.zeros((nr, nl), F32)
    pos = jnp.zeros((nr, nl), F32)
    tile_cls = jnp.zeros((SUBLANES, n_tab), F32)
    for k in range(N_CLASSES):
        m = cls == k
        mb = jnp.where(m, 1.0, 0.0).astype(BF16)
        rowtot = _dot(mb, ones_l).astype(BF16)
        rank = _dot(mb, lanes_before) + _dot(rows_before, rowtot)
        total = _dot(ones_r, rowtot)
        pos = jnp.where(m, run * EXPERT_ROW_TILE + rank, pos)
        run = run + jnp.floor((total + (EXPERT_ROW_TILE - 1)) / EXPERT_ROW_TILE)
        end = jnp.concatenate([run[:SUBLANES, :]] * (n_tab // nl), axis=1)
        tile_cls = tile_cls + jnp.where(end <= tile_i, 1.0, 0.0)
    qs_ref[...] = _tile_row_offset(pos, ROW_TILES).astype(jnp.int32)
    qy_ref[...] = _tile_row_offset(pos, H_TILES).astype(jnp.int32)
    tc = jnp.minimum(tile_cls, N_CLASSES - 1.0).astype(jnp.int32)
    grp = sum(jnp.where(tc >= g * PAIRS_PER_GROUP, 1, 0) for g in range(1, MOE_GROUPS))
    pair = tc - grp * PAIRS_PER_GROUP
    pair_lo = jnp.where(pair >= 3, 1, 0) + jnp.where(pair >= 5, 1, 0)
    pair_hi = pair + 1 - jnp.where(pair >= 5, 3, jnp.where(pair >= 3, 2, 0))
    ea_ref[...] = grp * EXPERTS_PER_GROUP + pair_lo
    eb_ref[...] = grp * EXPERTS_PER_GROUP + pair_hi
    nu_ref[...] = run[:SUBLANES, :].astype(jnp.int32)


def _plan(cls2):
    nr, nl = cls2.shape
    n_tiles = _sorted_rows(nr * nl) // EXPERT_ROW_TILE
    n_tab = -(-n_tiles // nl) * nl
    vmem = pl.BlockSpec(memory_space=pltpu.VMEM)
    qs, qy, ea, eb, nu = pl.pallas_call(
        _plan_kernel,
        out_shape=(jax.ShapeDtypeStruct((nr, nl), jnp.int32),
                   jax.ShapeDtypeStruct((nr, nl), jnp.int32),
                   jax.ShapeDtypeStruct((SUBLANES, n_tab), jnp.int32),
                   jax.ShapeDtypeStruct((SUBLANES, n_tab), jnp.int32),
                   jax.ShapeDtypeStruct((SUBLANES, nl), jnp.int32)),
        in_specs=[vmem],
        out_specs=(vmem, vmem, vmem, vmem, vmem),
        name="moe_plan",
    )(cls2)
    return (qs.reshape(nr * nl), qy.reshape(nr * nl),
            ea[0, :n_tiles], eb[0, :n_tiles], nu[0, :1])


def _scatter_kernel(qs_ref, hp_ref, s_ref):
    i = pl.program_id(0)
    tn = hp_ref.shape[0] // ROW_TILES
    group = SUBLANES * ROW_TILES

    @pl.when(i == 0)
    def _():
        zero_rows = _pack_pair(jnp.zeros((hp_ref.shape[0], 2 * LANES), F32))

        def fill(c, carry):
            start = pl.multiple_of(c * hp_ref.shape[0], SUBLANES)
            s_ref[pl.ds(start, hp_ref.shape[0]), :] = zero_rows
            return carry

        lax.fori_loop(0, s_ref.shape[0] // hp_ref.shape[0], fill, 0)

    def body(j, carry):
        src = pl.multiple_of(j * group, SUBLANES)
        for k in range(SUBLANES):
            q = qs_ref[i * tn + j * SUBLANES + k]
            s_ref[pl.ds(q, ROW_TILES, stride=SUBLANES), :] = (
                hp_ref[pl.ds(src + k, ROW_TILES, stride=SUBLANES), :])
        return carry

    lax.fori_loop(0, tn // SUBLANES, body, 0)


def _scatter(qs, hp):
    n = qs.shape[0]
    tn = SCATTER_TILE
    rows = ROW_TILES * _sorted_rows(n)
    grid_spec = pltpu.PrefetchScalarGridSpec(
        num_scalar_prefetch=1,
        grid=(n // tn,),
        in_specs=[pl.BlockSpec((ROW_TILES * tn, LANES), lambda i, qs: (i, 0))],
        out_specs=pl.BlockSpec((rows, LANES), lambda i, qs: (0, 0), pipeline_mode=pl.Buffered(1)),
    )
    return pl.pallas_call(
        _scatter_kernel,
        out_shape=jax.ShapeDtypeStruct((rows, LANES), jnp.uint32),
        grid_spec=grid_spec,
        compiler_params=pltpu.CompilerParams(
            dimension_semantics=("arbitrary",), vmem_limit_bytes=VMEM_LIMIT_RESIDENT_BYTES),
        name="moe_scatter",
    )(qs, hp)


def _expert_kernel(ea_ref, eb_ref, nu_ref, s_ref, wg32_ref, wu32_ref, wd32_ref, y_ref,
                   wg_ref, wu_ref, wd_ref):
    step = pl.program_id(0)

    @pl.when(step < W_STEPS)
    def _():
        _store_weight_slab(step, wg32_ref, wg_ref)
        _store_weight_slab(step, wu32_ref, wu_ref)
        _store_weight_slab(step, wd32_ref, wd_ref)

    first = (step - W_STEPS) * EXPERT_STEP_TILES
    rec = EXPERT_ROW_TILE * ROW_TILES
    out = EXPERT_ROW_TILE * H_TILES

    @pl.when((step >= W_STEPS) & (first < nu_ref[0]))
    def _():
        acts = []
        for j in range(EXPERT_STEP_TILES):
            t = first + j
            tiles = _from_tile_rows(s_ref[j * rec:(j + 1) * rec, :], ROW_TILES)
            hb = _unpack_pair(jnp.concatenate(tiles[:H_TILES], axis=-1)).astype(BF16)
            for e_ref, idx in ((ea_ref, 0), (eb_ref, 1)):
                e = e_ref[t]
                wt = pltpu.unpack_elementwise(tiles[H_TILES], index=idx, packed_dtype=BF16,
                                              unpacked_dtype=F32)
                wt = wt[:, 0:1] + wt[:, 1:2]
                gt = _dot(hb, wg_ref[e])
                acts.append((gt * _sigmoid(gt) * _dot(hb, wu_ref[e]) * wt).astype(BF16))
        for j in range(EXPERT_STEP_TILES):
            t = first + j
            y = (_dot(acts[2 * j], wd_ref[ea_ref[t]])
                 + _dot(acts[2 * j + 1], wd_ref[eb_ref[t]]))
            y_ref[j * out:(j + 1) * out, :] = _to_tile_rows(_pack_pair(y))

    @pl.when((step >= W_STEPS) & (first >= nu_ref[0]))
    def _():
        y_ref[...] = _pack_pair(jnp.zeros((y_ref.shape[0], 2 * LANES), F32))


def _experts(tile_ea, tile_eb, n_used, s, wg, wu, wd, layer):
    rows = s.shape[0] // ROW_TILES
    tb = EXPERT_ROW_TILE * EXPERT_STEP_TILES

    def row_block(i, *_):
        return (jnp.maximum(i - W_STEPS, 0), 0)

    grid_spec = pltpu.PrefetchScalarGridSpec(
        num_scalar_prefetch=3,
        grid=(W_STEPS + rows // tb,),
        in_specs=[pl.BlockSpec((ROW_TILES * tb, LANES), row_block),
                  _weight_slab_spec(wg, layer), _weight_slab_spec(wu, layer),
                  _weight_slab_spec(wd, layer)],
        out_specs=pl.BlockSpec((H_TILES * tb, LANES), row_block),
        scratch_shapes=[pltpu.VMEM(wg.shape[1:], BF16), pltpu.VMEM(wu.shape[1:], BF16),
                        pltpu.VMEM(wd.shape[1:], BF16)],
    )
    return pl.pallas_call(
        _expert_kernel,
        out_shape=jax.ShapeDtypeStruct((H_TILES * rows, LANES), jnp.uint32),
        grid_spec=grid_spec,
        compiler_params=pltpu.CompilerParams(
            dimension_semantics=("arbitrary",), vmem_limit_bytes=VMEM_LIMIT_BYTES),
        name="moe_experts",
    )(tile_ea, tile_eb, n_used, s, wg, wu, wd)


def _unsort_kernel(qy_ref, y_ref, o_ref):
    i = pl.program_id(0)
    tn = o_ref.shape[0] // H_TILES
    group = SUBLANES * H_TILES

    def body(j, carry):
        dst = pl.multiple_of(j * group, SUBLANES)
        for k in range(SUBLANES):
            q = qy_ref[i * tn + j * SUBLANES + k]
            o_ref[pl.ds(dst + k, H_TILES, stride=SUBLANES), :] = (
                y_ref[pl.ds(q, H_TILES, stride=SUBLANES), :])
        return carry

    lax.fori_loop(0, tn // SUBLANES, body, 0)


def _unsort(qy, y):
    n = qy.shape[0]
    tn = UNSORT_TILE
    grid_spec = pltpu.PrefetchScalarGridSpec(
        num_scalar_prefetch=1,
        grid=(n // tn,),
        in_specs=[pl.BlockSpec(y.shape, lambda i, qy: (0, 0), pipeline_mode=pl.Buffered(1))],
        out_specs=pl.BlockSpec((H_TILES * tn, LANES), lambda i, qy: (i, 0)),
    )
    return pl.pallas_call(
        _unsort_kernel,
        out_shape=jax.ShapeDtypeStruct((H_TILES * n, LANES), jnp.uint32),
        grid_spec=grid_spec,
        compiler_params=pltpu.CompilerParams(
            dimension_semantics=("arbitrary",), vmem_limit_bytes=VMEM_LIMIT_RESIDENT_BYTES),
        name="moe_unsort",
    )(qy, y)


def _gather_kernel(qy_ref, x_ref, y_ref, gn_ref, o_ref, rows_ref):
    i = pl.program_id(0)
    tn = x_ref.shape[0]
    group = SUBLANES * H_TILES

    def body(j, carry):
        dst = pl.multiple_of(j * group, SUBLANES)
        for k in range(SUBLANES):
            q = qy_ref[i * tn + j * SUBLANES + k]
            rows_ref[pl.ds(dst + k, H_TILES, stride=SUBLANES), :] = (
                y_ref[pl.ds(q, H_TILES, stride=SUBLANES), :])
        return carry

    lax.fori_loop(0, tn // SUBLANES, body, 0)
    packed = jnp.concatenate(_from_tile_rows(rows_ref[...], H_TILES), axis=-1)
    o_ref[...] = _rmsnorm(x_ref[...] + _unpack_pair(packed), gn_ref[...])


def _gather(qy, x2, y, g_final):
    n, d = x2.shape
    tn = GATHER_TILE
    grid_spec = pltpu.PrefetchScalarGridSpec(
        num_scalar_prefetch=1,
        grid=(n // tn,),
        in_specs=[pl.BlockSpec((tn, d), lambda i, qy: (i, 0)),
                  pl.BlockSpec(y.shape, lambda i, qy: (0, 0), pipeline_mode=pl.Buffered(1)),
                  pl.BlockSpec(g_final.shape, lambda i, qy: (0, 0), pipeline_mode=pl.Buffered(1))],
        out_specs=pl.BlockSpec((tn, d), lambda i, qy: (i, 0)),
        scratch_shapes=[pltpu.VMEM((H_TILES * tn, LANES), jnp.uint32)],
    )
    return pl.pallas_call(
        _gather_kernel,
        out_shape=jax.ShapeDtypeStruct(x2.shape, F32),
        grid_spec=grid_spec,
        compiler_params=pltpu.CompilerParams(
            dimension_semantics=("arbitrary",), vmem_limit_bytes=VMEM_LIMIT_RESIDENT_BYTES),
        name="moe_gather",
    )(qy, x2, y, g_final)


def _row(v):
    return v.reshape(1, -1)


def _router_params(w_rg, b_rg, w_re, b_re):
    pad = ROUTER_ROWS - MOE_GROUPS - N_EXPERTS
    wt = jnp.pad(jnp.concatenate([w_rg, w_re], axis=1).T, ((0, pad), (0, 0)))
    w_hi = wt.astype(BF16)
    w_lo = (wt - w_hi.astype(F32)).astype(BF16)
    b_col = jnp.pad(jnp.concatenate([b_rg, b_re]), (0, pad)).reshape(ROUTER_ROWS, 1)
    return jnp.concatenate([w_hi, w_lo], axis=0), b_col


def kernel(x, mem, norm_mix_g, w_in, conv_a_w, conv_a_b, ln_a_g, ln_a_b, w_a_out, w_pool_grp, pool_scale, conv_c_w, w_c_out, w_o, norm_x_g, norm_mem_g, w_xq, w_xkv, w_xo, norm_ffn_g, w_rg, b_rg, w_re, b_re, w_e_gate, w_e_up, w_e_down, norm_f_g):
    bsz, seq, d = x.shape
    n = bsz * seq
    depth = w_in.shape[0]
    moe_delta = None
    for l in range(depth):
        caw = jnp.broadcast_to(conv_a_w[l][:, None, :], (CONV_A_K, SUBLANES, D_CONV))
        x = _mixer(x, moe_delta, l, _row(norm_mix_g[l]), w_in, caw, _row(conv_a_b[l]),
                   _row(ln_a_g[l]), _row(ln_a_b[l]), w_a_out, w_pool_grp, _row(pool_scale[l]),
                   conv_c_w[l], w_c_out, w_o)
        w_r, b_r = _router_params(w_rg[l], b_rg[l], w_re[l], b_re[l])
        x, hp, cls = _xattn(x, _row(norm_x_g[l]), w_xq, mem, _row(norm_mem_g[l]), w_xkv, w_xo,
                            _row(norm_ffn_g[l]), w_r, b_r, l)
        qs, qy, tile_ea, tile_eb, n_used = _plan(cls.reshape(n // LANES, LANES))
        y = _experts(tile_ea, tile_eb, n_used, _scatter(qs, hp), w_e_gate, w_e_up, w_e_down, l)
        if l < depth - 1:
            moe_delta = _unsort(qy, y)
    return _gather(qy, x.reshape(n, d), y, _row(norm_f_g)).reshape(bsz, seq, d)
```

```python
import functools

import jax
import jax.numpy as jnp
from jax import lax
from jax.experimental import pallas as pl
from jax.experimental.pallas import tpu as pltpu

D_MODEL = 1024
D_CONV = D_MODEL // 2
D_POOL = D_MODEL // 2
D_SC = D_MODEL // 2
CONV_A_K = 31
SC_K = 3
POOL_WINDOWS = (2, 4, 8, 16)
POOL_GROUP_DIM = D_POOL // len(POOL_WINDOWS)
XATTN_HEADS = 4
XATTN_HEAD_DIM = D_MODEL // XATTN_HEADS
MOE_GROUPS = 4
EXPERTS_PER_GROUP = 4
N_EXPERTS = MOE_GROUPS * EXPERTS_PER_GROUP
EPS = 1e-6

OFF_A = 0
OFF_POOL = 2 * D_CONV
OFF_C = OFF_POOL + D_POOL
OFF_G = OFF_C + 3 * D_SC
D_IN_PROJ = OFF_G + 3 * D_MODEL

SUBLANES = 8
LANES = 128
VMEM_LIMIT_BYTES = 56 * 1024 * 1024

HIST_A = 32
HIST_POOL = 16
HIST_C = 8

SEQ_TILE = 512
XATTN_SEQ_TILE = 1024
CONV_ROW_CHUNK = 16
W_STEPS = 8

PAIRS_PER_GROUP = EXPERTS_PER_GROUP * (EXPERTS_PER_GROUP - 1) // 2
N_CLASSES = MOE_GROUPS * PAIRS_PER_GROUP
EXPERT_ROW_TILE = 128
EXPERT_STEP_TILES = 8
ROUTER_ROWS = 32
SCATTER_TILE = 1024
UNSORT_TILE = 2048
GATHER_TILE = 512
ROW_LOOP_UNROLL = 4
VMEM_LIMIT_RESIDENT_BYTES = 58 * 1024 * 1024
H_TILES = D_MODEL // 2 // LANES
ROW_TILES = H_TILES + 1

BF16 = jnp.bfloat16
F32 = jnp.float32


def _sorted_rows(n_tokens):
    return n_tokens + N_CLASSES * EXPERT_ROW_TILE


def _sigmoid(v):
    return 0.5 * jnp.tanh(0.5 * v) + 0.5


def _rmsnorm(xf, g):
    return xf * lax.rsqrt(jnp.mean(xf * xf, axis=-1, keepdims=True) + EPS) * g


def _dot(a, b):
    return jnp.dot(a, b, preferred_element_type=F32)


def _const_spec(shape):
    nd = len(shape)
    return pl.BlockSpec(shape, lambda *_: (0,) * nd, pipeline_mode=pl.Buffered(1))


def _weight_slab_spec(w, layer, axis=0):
    block = list(w.shape[1:])
    block[axis] //= W_STEPS
    return pl.BlockSpec(
        (None, *block),
        lambda i, *_: (layer,) + tuple(
            jnp.minimum(i, W_STEPS - 1) if a == axis else 0 for a in range(len(block))))


def _store_weight_slab(step, src_ref, dst_ref, axis=0):
    rows = src_ref.shape[axis]
    at = pl.ds(pl.multiple_of(step * rows, rows), rows)
    idx = tuple(at if a == axis else slice(None) for a in range(len(src_ref.shape)))
    dst_ref[idx] = src_ref[...].astype(BF16)


def _unpack_pair(q):
    return jnp.concatenate(
        [pltpu.unpack_elementwise(q, index=0, packed_dtype=BF16, unpacked_dtype=F32),
         pltpu.unpack_elementwise(q, index=1, packed_dtype=BF16, unpacked_dtype=F32)], axis=-1)


def _pack_pair(v):
    half = v.shape[-1] // 2
    return pltpu.pack_elementwise([v[:, :half], v[:, half:]], packed_dtype=BF16)


def _to_tile_rows(v):
    t, w = v.shape
    nj = w // LANES
    parts = [v[:, j * LANES:(j + 1) * LANES].reshape(t // SUBLANES, 1, SUBLANES, LANES)
             for j in range(nj)]
    return jnp.concatenate(parts, axis=1).reshape(nj * t, LANES)


def _from_tile_rows(v2, nj):
    t = v2.shape[0] // nj
    v4 = v2.reshape(t // SUBLANES, nj, SUBLANES, LANES)
    return [v4[:, j].reshape(t, LANES) for j in range(nj)]


def _tile_row_offset(r, nj):
    hi = jnp.floor(r / SUBLANES)
    return hi * (SUBLANES * nj) + (r - hi * SUBLANES)


def _mixer_kernel(x_ref, *refs, has_moe_delta, tiles_per_seq):
    y_ref, refs = (refs[0], refs[1:]) if has_moe_delta else (None, refs)
    (g_ref, w_in32, caw_ref, cab_ref, lng_ref, lnb_ref, w_a_out32, w_pool32, pool_scale_ref,
     ccw_ref, w_c_out32, w_o32, o_ref, abuf, ubuf, vbuf, actbuf, pbuf,
     w_in_ref, w_a_out_ref, w_pool_ref, w_c_out_ref, w_o_ref) = refs
    step = pl.program_id(0)

    @pl.when(step < W_STEPS)
    def _():
        for src, dst in ((w_in32, w_in_ref), (w_a_out32, w_a_out_ref),
                         (w_c_out32, w_c_out_ref), (w_o32, w_o_ref)):
            _store_weight_slab(step, src, dst)
        _store_weight_slab(step, w_pool32, w_pool_ref, axis=1)

    @pl.when(step >= W_STEPS)
    def _():
        _mix_tile((step - W_STEPS) % tiles_per_seq, x_ref, y_ref, g_ref, w_in_ref, caw_ref,
                  cab_ref, lng_ref, lnb_ref, w_a_out_ref, w_pool_ref, pool_scale_ref, ccw_ref,
                  w_c_out_ref, w_o_ref, o_ref, abuf, ubuf, vbuf, actbuf, pbuf)


def _mix_tile(s, x_ref, y_ref, g_ref, w_in_ref, caw_ref, cab_ref, lng_ref, lnb_ref, w_a_out_ref,
              w_pool_ref, pool_scale_ref, ccw_ref, w_c_out_ref, w_o_ref,
              o_ref, abuf, ubuf, vbuf, actbuf, pbuf):
    ts = x_ref.shape[1]

    @pl.when(s == 0)
    def _():
        abuf[0, 0:HIST_A, :] = jnp.zeros((HIST_A, D_CONV), F32)
        ubuf[0:HIST_POOL, :] = jnp.zeros((HIST_POOL, D_POOL), F32)
        vbuf[0:HIST_C, :] = jnp.zeros((HIST_C, D_SC), F32)

    x = x_ref[0]
    if y_ref is not None:
        x = x + _unpack_pair(jnp.concatenate(_from_tile_rows(y_ref[...], H_TILES), axis=-1))
    h = _rmsnorm(x, g_ref[...]).astype(BF16)

    def proj(off, width):
        return _dot(h, w_in_ref[:, off:off + width])

    pa = proj(OFF_A, 2 * D_CONV)
    abuf[0, HIST_A:HIST_A + ts, :] = pa[:, :D_CONV] * _sigmoid(pa[:, D_CONV:])
    shifted_rows = HIST_A + ts - SUBLANES
    for r in range(1, SUBLANES):
        abuf[r, 0:shifted_rows, :] = abuf[0, r:r + shifted_rows, :]
    chunk3 = (CONV_ROW_CHUNK // SUBLANES, SUBLANES, D_CONV)
    for c0 in range(0, ts, CONV_ROW_CHUNK):
        acc = jnp.broadcast_to(cab_ref[...], chunk3)
        for k in range(CONV_A_K):
            q, r = divmod(HIST_A - (CONV_A_K - 1) + k, SUBLANES)
            lo = c0 + q * SUBLANES
            acc = acc + caw_ref[k] * abuf[r, lo:lo + CONV_ROW_CHUNK, :].reshape(chunk3)
        acc = acc.reshape(CONV_ROW_CHUNK, D_CONV)
        mu = jnp.mean(acc, axis=-1, keepdims=True)
        cen = acc - mu
        var = jnp.mean(cen * cen, axis=-1, keepdims=True)
        ln = cen * lax.rsqrt(var + EPS) * lng_ref[...] + lnb_ref[...]
        actbuf[c0:c0 + CONV_ROW_CHUNK, :] = (ln * _sigmoid(ln)).astype(BF16)
    abuf[0, 0:HIST_A, :] = abuf[0, ts:ts + HIST_A, :]

    pbuf[...] = proj(OFF_POOL, D_IN_PROJ - OFF_POOL)

    def pcol(off, width):
        return pbuf[:, off - OFF_POOL:off - OFF_POOL + width]

    pu = pcol(OFF_POOL, D_POOL)
    ubuf[HIST_POOL:HIST_POOL + ts, :] = pu
    merged = _sigmoid(pcol(OFF_G, D_MODEL)) * _dot(actbuf[...], w_a_out_ref[...])

    t_glob = s * ts + lax.broadcasted_iota(jnp.int32, (ts, 1), 0)
    p_parts = []
    for i, w in enumerate(POOL_WINDOWS):
        c0, c1 = i * POOL_GROUP_DIM, (i + 1) * POOL_GROUP_DIM
        tok = pu[:, c0:c1]
        win = ubuf[:, c0:c1]
        shift = 1
        while shift < w:
            win = win + pltpu.roll(win, shift, 0)
            shift *= 2
        win = win[HIST_POOL:]
        cnt = jnp.minimum(t_glob + 1, w).astype(F32)
        pin = (win / cnt - tok).astype(BF16)
        p_parts.append(_dot(pin, w_pool_ref[i]))
    ubuf[0:HIST_POOL, :] = ubuf[ts:ts + HIST_POOL, :]
    merged = merged + _sigmoid(pcol(OFF_G + D_MODEL, D_MODEL)) * (jnp.concatenate(p_parts, axis=-1) * pool_scale_ref[...])

    c_b = pcol(OFF_C + D_SC, D_SC)
    v = pcol(OFF_C + 2 * D_SC, D_SC) * pcol(OFF_C, D_SC)
    vbuf[HIST_C:HIST_C + ts, :] = v
    conv_c = ccw_ref[SC_K - 1:SC_K, :] * v
    for k in range(SC_K - 1):
        off = HIST_C - (SC_K - 1) + k
        conv_c = conv_c + ccw_ref[k:k + 1, :] * vbuf[off:off + ts, :]
    vbuf[0:HIST_C, :] = vbuf[ts:ts + HIST_C, :]
    merged = merged + _sigmoid(pcol(OFF_G + 2 * D_MODEL, D_MODEL)) * _dot((c_b * conv_c).astype(BF16), w_c_out_ref[...])

    o_ref[0] = x + _dot(merged.astype(BF16), w_o_ref[...])


def _mixer(x, moe_delta, layer, g, w_in, caw, cab, lng, lnb, w_a_out, w_pool, pool_scale, ccw,
           w_c_out, w_o):
    b, s, d = x.shape
    ts = SEQ_TILE
    nt = s // ts

    def tile(i):
        return jnp.maximum(i - W_STEPS, 0)

    def small(c):
        return _const_spec(c.shape)

    def slab_spec(w, axis):
        return _weight_slab_spec(w, layer, axis)

    delta = () if moe_delta is None else (moe_delta,)
    delta_specs = [pl.BlockSpec((H_TILES * ts, LANES), lambda i: (tile(i), 0))] * len(delta)
    x_spec = pl.BlockSpec((1, ts, d), lambda i: (tile(i) // nt, tile(i) % nt, 0))
    return pl.pallas_call(
        functools.partial(_mixer_kernel, has_moe_delta=moe_delta is not None, tiles_per_seq=nt),
        out_shape=jax.ShapeDtypeStruct(x.shape, F32),
        grid=(W_STEPS + b * nt,),
        in_specs=[x_spec] + delta_specs
        + [small(g), slab_spec(w_in, 0), small(caw), small(cab), small(lng), small(lnb),
           slab_spec(w_a_out, 0), slab_spec(w_pool, 1), small(pool_scale), small(ccw),
           slab_spec(w_c_out, 0), slab_spec(w_o, 0)],
        out_specs=x_spec,
        scratch_shapes=[pltpu.VMEM((SUBLANES, HIST_A + ts, D_CONV), F32),
                        pltpu.VMEM((HIST_POOL + ts, D_POOL), F32),
                        pltpu.VMEM((HIST_C + ts, D_SC), F32),
                        pltpu.VMEM((ts, D_CONV), BF16),
                        pltpu.VMEM((ts, D_IN_PROJ - OFF_POOL), F32)]
        + [pltpu.VMEM(w.shape[1:], BF16) for w in (w_in, w_a_out, w_pool, w_c_out, w_o)],
        compiler_params=pltpu.CompilerParams(
            dimension_semantics=("arbitrary",), vmem_limit_bytes=VMEM_LIMIT_BYTES),
        name="mixer",
    )(x, *delta, g, w_in, caw, cab, lng, lnb, w_a_out, w_pool, pool_scale, ccw, w_c_out, w_o)


def _route_rows(lt):
    def first_argmax(vals, vmax):
        idx = jnp.full(vmax.shape, len(vals) - 1, jnp.int32)
        for k in range(len(vals) - 2, -1, -1):
            idx = jnp.where(vals[k] == vmax, k, idx)
        return idx

    g = [lt[k:k + 1, :] for k in range(MOE_GROUPS)]
    gmax = functools.reduce(jnp.maximum, g)
    g_idx = first_argmax(g, gmax)
    g_val = 1.0 / sum(jnp.exp(v - gmax) for v in g)
    e = []
    for j in range(EXPERTS_PER_GROUP):
        ej = lt[MOE_GROUPS + j:MOE_GROUPS + j + 1, :]
        for grp in range(1, MOE_GROUPS):
            r = MOE_GROUPS + grp * EXPERTS_PER_GROUP + j
            ej = jnp.where(g_idx == grp, lt[r:r + 1, :], ej)
        e.append(ej)
    m1 = functools.reduce(jnp.maximum, e)
    i1 = first_argmax(e, m1)
    e2 = [jnp.where(i1 == j, -jnp.inf, e[j]) for j in range(EXPERTS_PER_GROUP)]
    m2 = functools.reduce(jnp.maximum, e2)
    i2 = first_argmax(e2, m2)
    esum = sum(jnp.exp(v - m1) for v in e)
    p1 = 1.0 / esum
    p2 = jnp.exp(m2 - m1) / esum
    w1 = g_val * (p1 / (p1 + p2))
    w2 = g_val * (p2 / (p1 + p2))
    lo = jnp.minimum(i1, i2)
    hi = jnp.maximum(i1, i2)
    pair = ((lo * (2 * EXPERTS_PER_GROUP - 1 - lo)) >> 1) + (hi - lo - 1)
    cls = g_idx * PAIRS_PER_GROUP + pair
    first_is_low = i1 < i2
    return cls, jnp.where(first_is_low, w1, w2), jnp.where(first_is_low, w2, w1)


def _xattn_kernel(x_ref, g_ref, w_q32, mem_ref, gm_ref, w_kv32, w_o32, gf_ref, w_r_ref, b_r_ref,
                  o_ref, hp_ref, cls_ref, xn_prev, w_q_ref, w_kv_ref, w_o_ref, kt_ref, v_ref,
                  *, tiles, tiles_per_seq):
    step = pl.program_id(0)

    @pl.when(step == 0)
    def _():
        xn_prev[...] = jnp.zeros(xn_prev.shape, F32)

    @pl.when(step < W_STEPS)
    def _():
        _store_weight_slab(step, w_q32, w_q_ref)
        _store_weight_slab(step, w_kv32, w_kv_ref)
        _store_weight_slab(step, w_o32, w_o_ref)

    @pl.when((step >= W_STEPS) & (step < W_STEPS + tiles)
             & ((step - W_STEPS) % tiles_per_seq == 0))
    def _():
        mn = _rmsnorm(mem_ref[0], gm_ref[...]).astype(BF16)
        kv = _dot(mn, w_kv_ref[...])
        kt_ref[...] = kv[:, :D_MODEL].T.astype(BF16)
        v_ref[...] = kv[:, D_MODEL:].astype(BF16)

    @pl.when(step >= W_STEPS)
    def _():
        _xattn_tile(x_ref, g_ref, w_q_ref, kt_ref, v_ref, w_o_ref, gf_ref, w_r_ref, b_r_ref,
                    o_ref, hp_ref, cls_ref, xn_prev)


def _xattn_tile(x_ref, g_ref, w_q_ref, kt_ref, v_ref, w_o_ref, gf_ref, w_r_ref, b_r_ref,
                o_ref, hp_ref, cls_ref, xn_prev):
    hf = _rmsnorm(xn_prev[...], gf_ref[...])
    h_hi = hf.astype(BF16)
    h_lo = (hf - h_hi.astype(F32)).astype(BF16)
    nt_dims = (((1,), (1,)), ((), ()))
    part = lax.dot_general(w_r_ref[...], h_hi, nt_dims, preferred_element_type=F32)
    lt = (part[:ROUTER_ROWS] + part[ROUTER_ROWS:]
          + lax.dot_general(w_r_ref[:ROUTER_ROWS, :], h_lo, nt_dims, preferred_element_type=F32)
          + b_r_ref[...])
    cls, wa, wb = _route_rows(lt)
    ts = cls.shape[1]
    for c in range(ts // LANES):
        cls_ref[0, c:c + 1, :] = cls[:, c * LANES:(c + 1) * LANES]
    row = lax.broadcasted_iota(jnp.int32, (LANES, ts), 0)

    def weight_lanes(w):
        w_hi = w.astype(BF16).astype(F32)
        return jnp.where(row == 0, w_hi, jnp.where(row == 1, w - w_hi, 0.0)).T

    wt_tile = pltpu.pack_elementwise([weight_lanes(wa), weight_lanes(wb)], packed_dtype=BF16)
    hp_ref[...] = _to_tile_rows(jnp.concatenate([_pack_pair(hf), wt_tile], axis=-1))

    x = x_ref[0]
    h = _rmsnorm(x, g_ref[...]).astype(BF16)
    q = (_dot(h, w_q_ref[...]) * (XATTN_HEAD_DIM ** -0.5)).astype(BF16)
    heads = []
    for hd in range(XATTN_HEADS):
        c0, c1 = hd * XATTN_HEAD_DIM, (hd + 1) * XATTN_HEAD_DIM
        sc = _dot(q[:, c0:c1], kt_ref[c0:c1, :])
        e = jnp.exp(sc - jnp.max(sc, axis=-1, keepdims=True))
        denom = jnp.sum(e, axis=-1, keepdims=True)
        heads.append(_dot(e.astype(BF16), v_ref[:, c0:c1]) / denom)
    o = jnp.concatenate(heads, axis=-1).astype(BF16)
    xn = x + _dot(o, w_o_ref[...])
    o_ref[0] = xn
    xn_prev[...] = xn


def _xattn(x, g, w_q, mem, g_mem, w_kv, w_o, g_ffn, w_r, b_r, layer):
    b, s, d = x.shape
    n_mem = mem.shape[1]
    ts = XATTN_SEQ_TILE
    nt = s // ts
    n = b * s
    tiles = b * nt

    def attn_tile(step):
        return jnp.clip(step - W_STEPS, 0, tiles - 1)

    def route_tile(step):
        return jnp.maximum(step - W_STEPS - 1, 0)

    return pl.pallas_call(
        functools.partial(_xattn_kernel, tiles=tiles, tiles_per_seq=nt),
        out_shape=(jax.ShapeDtypeStruct(x.shape, F32),
                   jax.ShapeDtypeStruct((ROW_TILES * n, LANES), jnp.uint32),
                   jax.ShapeDtypeStruct((n // ts, ts // LANES, LANES), jnp.int32)),
        grid=(W_STEPS + tiles + 1,),
        in_specs=[pl.BlockSpec((1, ts, d), lambda i: (attn_tile(i) // nt, attn_tile(i) % nt, 0)),
                  _const_spec(g.shape), _weight_slab_spec(w_q, layer),
                  pl.BlockSpec((1, n_mem, d), lambda i: (attn_tile(i) // nt, 0, 0)),
                  _const_spec(g_mem.shape), _weight_slab_spec(w_kv, layer),
                  _weight_slab_spec(w_o, layer), _const_spec(g_ffn.shape),
                  _const_spec(w_r.shape), _const_spec(b_r.shape)],
        out_specs=(pl.BlockSpec((1, ts, d), lambda i: (attn_tile(i) // nt, attn_tile(i) % nt, 0)),
                   pl.BlockSpec((ROW_TILES * ts, LANES), lambda i: (route_tile(i), 0)),
                   pl.BlockSpec((1, ts // LANES, LANES), lambda i: (route_tile(i), 0, 0))),
        scratch_shapes=[pltpu.VMEM((ts, d), F32), pltpu.VMEM(w_q.shape[1:], BF16),
                        pltpu.VMEM(w_kv.shape[1:], BF16), pltpu.VMEM(w_o.shape[1:], BF16),
                        pltpu.VMEM((d, n_mem), BF16), pltpu.VMEM((n_mem, d), BF16)],
        compiler_params=pltpu.CompilerParams(
            dimension_semantics=("arbitrary",), vmem_limit_bytes=VMEM_LIMIT_BYTES),
        name="xattn",
    )(x, g, w_q, mem, g_mem, w_kv, w_o, g_ffn, w_r, b_r)


def _plan_kernel(cls_ref, qs_ref, qy_ref, ea_ref, eb_ref, nu_ref):
    cls = cls_ref[...]
    nr, nl = cls.shape
    r = lax.broadcasted_iota(jnp.int32, (nl, nl), 0)
    c = lax.broadcasted_iota(jnp.int32, (nl, nl), 1)
    lanes_before = (r < c).astype(BF16)
    rr = lax.broadcasted_iota(jnp.int32, (nr, nr), 0)
    rc = lax.broadcasted_iota(jnp.int32, (nr, nr), 1)
    rows_before = (rc < rr).astype(BF16)
    ones_l = jnp.ones((nl, nl), BF16)
    ones_r = jnp.ones((nr, nr), BF16)
    n_tab = ea_ref.shape[1]
    tile_i = lax.broadcasted_iota(jnp.int32, (SUBLANES, n_tab), 1).astype(F32)
    run = jnp.zeros((nr, nl), F32)
    pos = jnp.zeros((nr, nl), F32)
    tile_cls = jnp.zeros((SUBLANES, n_tab), F32)
    for k in range(N_CLASSES):
        m = cls == k
        mb = jnp.where(m, 1.0, 0.0).astype(BF16)
        rowtot = _dot(mb, ones_l).astype(BF16)
        rank = _dot(mb, lanes_before) + _dot(rows_before, rowtot)
        total = _dot(ones_r, rowtot)
        pos = jnp.where(m, run * EXPERT_ROW_TILE + rank, pos)
        run = run + jnp.floor((total + (EXPERT_ROW_TILE - 1)) / EXPERT_ROW_TILE)
        end = jnp.concatenate([run[:SUBLANES, :]] * (n_tab // nl), axis=1)
        tile_cls = tile_cls + jnp.where(end <= tile_i, 1.0, 0.0)
    qs_ref[...] = _tile_row_offset(pos, ROW_TILES).astype(jnp.int32)
    qy_ref[...] = _tile_row_offset(pos, H_TILES).astype(jnp.int32)
    tc = jnp.minimum(tile_cls, N_CLASSES - 1.0).astype(jnp.int32)
    grp = sum(jnp.where(tc >= g * PAIRS_PER_GROUP, 1, 0) for g in range(1, MOE_GROUPS))
    pair = tc - grp * PAIRS_PER_GROUP
    pair_lo = jnp.where(pair >= 3, 1, 0) + jnp.where(pair >= 5, 1, 0)
    pair_hi = pair + 1 - jnp.where(pair >= 5, 3, jnp.where(pair >= 3, 2, 0))
    ea_ref[...] = grp * EXPERTS_PER_GROUP + pair_lo
    eb_ref[...] = grp * EXPERTS_PER_GROUP + pair_hi
    nu_ref[...] = run[:SUBLANES, :].astype(jnp.int32)


def _plan(cls2):
    nr, nl = cls2.shape
    n_tiles = _sorted_rows(nr * nl) // EXPERT_ROW_TILE
    n_tab = -(-n_tiles // nl) * nl
    vmem = pl.BlockSpec(memory_space=pltpu.VMEM)
    qs, qy, ea, eb, nu = pl.pallas_call(
        _plan_kernel,
        out_shape=(jax.ShapeDtypeStruct((nr, nl), jnp.int32),
                   jax.ShapeDtypeStruct((nr, nl), jnp.int32),
                   jax.ShapeDtypeStruct((SUBLANES, n_tab), jnp.int32),
                   jax.ShapeDtypeStruct((SUBLANES, n_tab), jnp.int32),
                   jax.ShapeDtypeStruct((SUBLANES, nl), jnp.int32)),
        in_specs=[vmem],
        out_specs=(vmem, vmem, vmem, vmem, vmem),
        name="moe_plan",
    )(cls2)
    return (qs.reshape(nr * nl), qy.reshape(nr * nl),
            ea[0, :n_tiles], eb[0, :n_tiles], nu[0, :1])


def _scatter_kernel(qs_ref, hp_ref, s_ref):
    i = pl.program_id(0)
    tn = hp_ref.shape[0] // ROW_TILES
    group = SUBLANES * ROW_TILES

    @pl.when(i == 0)
    def _():
        zero_rows = _pack_pair(jnp.zeros((hp_ref.shape[0], 2 * LANES), F32))

        def fill(c, carry):
            start = pl.multiple_of(c * hp_ref.shape[0], SUBLANES)
            s_ref[pl.ds(start, hp_ref.shape[0]), :] = zero_rows
            return carry

        lax.fori_loop(0, s_ref.shape[0] // hp_ref.shape[0], fill, 0)

    def body(j, carry):
        src = pl.multiple_of(j * group, SUBLANES)
        for k in range(SUBLANES):
            q = qs_ref[i * tn + j * SUBLANES + k]
            s_ref[pl.ds(q, ROW_TILES, stride=SUBLANES), :] = (
                hp_ref[pl.ds(src + k, ROW_TILES, stride=SUBLANES), :])
        return carry

    lax.fori_loop(0, tn // SUBLANES, body, 0, unroll=ROW_LOOP_UNROLL)


def _scatter(qs, hp):
    n = qs.shape[0]
    tn = SCATTER_TILE
    rows = ROW_TILES * _sorted_rows(n)
    grid_spec = pltpu.PrefetchScalarGridSpec(
        num_scalar_prefetch=1,
        grid=(n // tn,),
        in_specs=[pl.BlockSpec((ROW_TILES * tn, LANES), lambda i, qs: (i, 0))],
        out_specs=pl.BlockSpec((rows, LANES), lambda i, qs: (0, 0), pipeline_mode=pl.Buffered(1)),
    )
    return pl.pallas_call(
        _scatter_kernel,
        out_shape=jax.ShapeDtypeStruct((rows, LANES), jnp.uint32),
        grid_spec=grid_spec,
        compiler_params=pltpu.CompilerParams(
            dimension_semantics=("arbitrary",), vmem_limit_bytes=VMEM_LIMIT_RESIDENT_BYTES),
        name="moe_scatter",
    )(qs, hp)


def _expert_kernel(ea_ref, eb_ref, nu_ref, s_ref, wg32_ref, wu32_ref, wd32_ref, y_ref,
                   wg_ref, wu_ref, wd_ref):
    step = pl.program_id(0)

    @pl.when(step < W_STEPS)
    def _():
        _store_weight_slab(step, wg32_ref, wg_ref)
        _store_weight_slab(step, wu32_ref, wu_ref)
        _store_weight_slab(step, wd32_ref, wd_ref)

    first = (step - W_STEPS) * EXPERT_STEP_TILES
    rec = EXPERT_ROW_TILE * ROW_TILES
    out = EXPERT_ROW_TILE * H_TILES

    @pl.when((step >= W_STEPS) & (first < nu_ref[0]))
    def _():
        acts = []
        for j in range(EXPERT_STEP_TILES):
            t = first + j
            tiles = _from_tile_rows(s_ref[j * rec:(j + 1) * rec, :], ROW_TILES)
            hb = _unpack_pair(jnp.concatenate(tiles[:H_TILES], axis=-1)).astype(BF16)
            for e_ref, idx in ((ea_ref, 0), (eb_ref, 1)):
                e = e_ref[t]
                wt = pltpu.unpack_elementwise(tiles[H_TILES], index=idx, packed_dtype=BF16,
                                              unpacked_dtype=F32)
                wt = wt[:, 0:1] + wt[:, 1:2]
                gt = _dot(hb, wg_ref[e])
                acts.append((gt * _sigmoid(gt) * _dot(hb, wu_ref[e]) * wt).astype(BF16))
        for j in range(EXPERT_STEP_TILES):
            t = first + j
            y = (_dot(acts[2 * j], wd_ref[ea_ref[t]])
                 + _dot(acts[2 * j + 1], wd_ref[eb_ref[t]]))
            y_ref[j * out:(j + 1) * out, :] = _to_tile_rows(_pack_pair(y))

    @pl.when((step >= W_STEPS) & (first >= nu_ref[0]))
    def _():
        y_ref[...] = _pack_pair(jnp.zeros((y_ref.shape[0], 2 * LANES), F32))


def _experts(tile_ea, tile_eb, n_used, s, wg, wu, wd, layer):
    rows = s.shape[0] // ROW_TILES
    tb = EXPERT_ROW_TILE * EXPERT_STEP_TILES

    def row_block(i, *_):
        return (jnp.maximum(i - W_STEPS, 0), 0)

    grid_spec = pltpu.PrefetchScalarGridSpec(
        num_scalar_prefetch=3,
        grid=(W_STEPS + rows // tb,),
        in_specs=[pl.BlockSpec((ROW_TILES * tb, LANES), row_block),
                  _weight_slab_spec(wg, layer), _weight_slab_spec(wu, layer),
                  _weight_slab_spec(wd, layer)],
        out_specs=pl.BlockSpec((H_TILES * tb, LANES), row_block),
        scratch_shapes=[pltpu.VMEM(wg.shape[1:], BF16), pltpu.VMEM(wu.shape[1:], BF16),
                        pltpu.VMEM(wd.shape[1:], BF16)],
    )
    return pl.pallas_call(
        _expert_kernel,
        out_shape=jax.ShapeDtypeStruct((H_TILES * rows, LANES), jnp.uint32),
        grid_spec=grid_spec,
        compiler_params=pltpu.CompilerParams(
            dimension_semantics=("arbitrary",), vmem_limit_bytes=VMEM_LIMIT_BYTES),
        name="moe_experts",
    )(tile_ea, tile_eb, n_used, s, wg, wu, wd)


def _unsort_kernel(qy_ref, y_ref, o_ref):
    i = pl.program_id(0)
    tn = o_ref.shape[0] // H_TILES
    group = SUBLANES * H_TILES

    def body(j, carry):
        dst = pl.multiple_of(j * group, SUBLANES)
        for k in range(SUBLANES):
            q = qy_ref[i * tn + j * SUBLANES + k]
            o_ref[pl.ds(dst + k, H_TILES, stride=SUBLANES), :] = (
                y_ref[pl.ds(q, H_TILES, stride=SUBLANES), :])
        return carry

    lax.fori_loop(0, tn // SUBLANES, body, 0, unroll=ROW_LOOP_UNROLL)


def _unsort(qy, y):
    n = qy.shape[0]
    tn = UNSORT_TILE
    grid_spec = pltpu.PrefetchScalarGridSpec(
        num_scalar_prefetch=1,
        grid=(n // tn,),
        in_specs=[pl.BlockSpec(y.shape, lambda i, qy: (0, 0), pipeline_mode=pl.Buffered(1))],
        out_specs=pl.BlockSpec((H_TILES * tn, LANES), lambda i, qy: (i, 0)),
    )
    return pl.pallas_call(
        _unsort_kernel,
        out_shape=jax.ShapeDtypeStruct((H_TILES * n, LANES), jnp.uint32),
        grid_spec=grid_spec,
        compiler_params=pltpu.CompilerParams(
            dimension_semantics=("arbitrary",), vmem_limit_bytes=VMEM_LIMIT_RESIDENT_BYTES),
        name="moe_unsort",
    )(qy, y)


def _gather_kernel(qy_ref, x_ref, y_ref, gn_ref, o_ref, rows_ref):
    i = pl.program_id(0)
    tn = x_ref.shape[0]
    group = SUBLANES * H_TILES

    def body(j, carry):
        dst = pl.multiple_of(j * group, SUBLANES)
        for k in range(SUBLANES):
            q = qy_ref[i * tn + j * SUBLANES + k]
            rows_ref[pl.ds(dst + k, H_TILES, stride=SUBLANES), :] = (
                y_ref[pl.ds(q, H_TILES, stride=SUBLANES), :])
        return carry

    lax.fori_loop(0, tn // SUBLANES, body, 0, unroll=ROW_LOOP_UNROLL)
    packed = jnp.concatenate(_from_tile_rows(rows_ref[...], H_TILES), axis=-1)
    o_ref[...] = _rmsnorm(x_ref[...] + _unpack_pair(packed), gn_ref[...])


def _gather(qy, x2, y, g_final):
    n, d = x2.shape
    tn = GATHER_TILE
    grid_spec = pltpu.PrefetchScalarGridSpec(
        num_scalar_prefetch=1,
        grid=(n // tn,),
        in_specs=[pl.BlockSpec((tn, d), lambda i, qy: (i, 0)),
                  pl.BlockSpec(y.shape, lambda i, qy: (0, 0), pipeline_mode=pl.Buffered(1)),
                  pl.BlockSpec(g_final.shape, lambda i, qy: (0, 0), pipeline_mode=pl.Buffered(1))],
        out_specs=pl.BlockSpec((tn, d), lambda i, qy: (i, 0)),
        scratch_shapes=[pltpu.VMEM((H_TILES * tn, LANES), jnp.uint32)],
    )
    return pl.pallas_call(
        _gather_kernel,
        out_shape=jax.ShapeDtypeStruct(x2.shape, F32),
        grid_spec=grid_spec,
        compiler_params=pltpu.CompilerParams(
            dimension_semantics=("arbitrary",), vmem_limit_bytes=VMEM_LIMIT_RESIDENT_BYTES),
        name="moe_gather",
    )(qy, x2, y, g_final)


def _row(v):
    return v.reshape(1, -1)


def _router_params(w_rg, b_rg, w_re, b_re):
    pad = ROUTER_ROWS - MOE_GROUPS - N_EXPERTS
    wt = jnp.pad(jnp.concatenate([w_rg, w_re], axis=1).T, ((0, pad), (0, 0)))
    w_hi = wt.astype(BF16)
    w_lo = (wt - w_hi.astype(F32)).astype(BF16)
    b_col = jnp.pad(jnp.concatenate([b_rg, b_re]), (0, pad)).reshape(ROUTER_ROWS, 1)
    return jnp.concatenate([w_hi, w_lo], axis=0), b_col


def kernel(x, mem, norm_mix_g, w_in, conv_a_w, conv_a_b, ln_a_g, ln_a_b, w_a_out, w_pool_grp, pool_scale, conv_c_w, w_c_out, w_o, norm_x_g, norm_mem_g, w_xq, w_xkv, w_xo, norm_ffn_g, w_rg, b_rg, w_re, b_re, w_e_gate, w_e_up, w_e_down, norm_f_g):
    bsz, seq, d = x.shape
    n = bsz * seq
    depth = w_in.shape[0]
    moe_delta = None
    for l in range(depth):
        caw = jnp.broadcast_to(conv_a_w[l][:, None, :], (CONV_A_K, SUBLANES, D_CONV))
        x = _mixer(x, moe_delta, l, _row(norm_mix_g[l]), w_in, caw, _row(conv_a_b[l]),
                   _row(ln_a_g[l]), _row(ln_a_b[l]), w_a_out, w_pool_grp, _row(pool_scale[l]),
                   conv_c_w[l], w_c_out, w_o)
        w_r, b_r = _router_params(w_rg[l], b_rg[l], w_re[l], b_re[l])
        x, hp, cls = _xattn(x, _row(norm_x_g[l]), w_xq, mem, _row(norm_mem_g[l]), w_xkv, w_xo,
                            _row(norm_ffn_g[l]), w_r, b_r, l)
        qs, qy, tile_ea, tile_eb, n_used = _plan(cls.reshape(n // LANES, LANES))
        y = _experts(tile_ea, tile_eb, n_used, _scatter(qs, hp), w_e_gate, w_e_up, w_e_down, l)
        if l < depth - 1:
            moe_delta = _unsort(qy, y)
    return _gather(qy, x.reshape(n, d), y, _row(norm_f_g)).reshape(bsz, seq, d)
```

```python
import functools

import jax
import jax.numpy as jnp
from jax import lax
from jax.experimental import pallas as pl
from jax.experimental.pallas import tpu as pltpu

D_MODEL = 1024
D_CONV = D_MODEL // 2
D_POOL = D_MODEL // 2
D_SC = D_MODEL // 2
CONV_A_K = 31
SC_K = 3
POOL_WINDOWS = (2, 4, 8, 16)
POOL_GROUP_DIM = D_POOL // len(POOL_WINDOWS)
XATTN_HEADS = 4
XATTN_HEAD_DIM = D_MODEL // XATTN_HEADS
MOE_GROUPS = 4
EXPERTS_PER_GROUP = 4
N_EXPERTS = MOE_GROUPS * EXPERTS_PER_GROUP
EPS = 1e-6

OFF_A = 0
OFF_POOL = 2 * D_CONV
OFF_C = OFF_POOL + D_POOL
OFF_G = OFF_C + 3 * D_SC
D_IN_PROJ = OFF_G + 3 * D_MODEL

SUBLANES = 8
LANES = 128
VMEM_LIMIT_BYTES = 56 * 1024 * 1024

HIST_A = 32
HIST_POOL = 16
HIST_C = 8

SEQ_TILE = 512
XATTN_SEQ_TILE = 1024
CONV_ROW_CHUNK = 16
W_STEPS = 8

PAIRS_PER_GROUP = EXPERTS_PER_GROUP * (EXPERTS_PER_GROUP - 1) // 2
N_CLASSES = MOE_GROUPS * PAIRS_PER_GROUP
EXPERT_ROW_TILE = 128
EXPERT_STEP_TILES = 8
ROUTER_ROWS = 32
SCATTER_TILE = 1024
UNSORT_TILE = 4096
GATHER_TILE = 512
ROW_LOOP_UNROLL = 4
VMEM_LIMIT_RESIDENT_BYTES = 58 * 1024 * 1024
H_TILES = D_MODEL // 2 // LANES
ROW_TILES = H_TILES + 1

BF16 = jnp.bfloat16
F32 = jnp.float32


def _sorted_rows(n_tokens):
    return n_tokens + N_CLASSES * EXPERT_ROW_TILE


def _sigmoid(v):
    return 0.5 * jnp.tanh(0.5 * v) + 0.5


def _rmsnorm(xf, g):
    return xf * lax.rsqrt(jnp.mean(xf * xf, axis=-1, keepdims=True) + EPS) * g


def _dot(a, b):
    return jnp.dot(a, b, preferred_element_type=F32)


def _const_spec(shape):
    nd = len(shape)
    return pl.BlockSpec(shape, lambda *_: (0,) * nd, pipeline_mode=pl.Buffered(1))


def _weight_slab_spec(w, layer, axis=0):
    block = list(w.shape[1:])
    block[axis] //= W_STEPS
    return pl.BlockSpec(
        (None, *block),
        lambda i, *_: (layer,) + tuple(
            jnp.minimum(i, W_STEPS - 1) if a == axis else 0 for a in range(len(block))))


def _store_weight_slab(step, src_ref, dst_ref, axis=0):
    rows = src_ref.shape[axis]
    at = pl.ds(pl.multiple_of(step * rows, rows), rows)
    idx = tuple(at if a == axis else slice(None) for a in range(len(src_ref.shape)))
    dst_ref[idx] = src_ref[...].astype(BF16)


def _unpack_pair(q):
    return jnp.concatenate(
        [pltpu.unpack_elementwise(q, index=0, packed_dtype=BF16, unpacked_dtype=F32),
         pltpu.unpack_elementwise(q, index=1, packed_dtype=BF16, unpacked_dtype=F32)], axis=-1)


def _pack_pair(v):
    half = v.shape[-1] // 2
    return pltpu.pack_elementwise([v[:, :half], v[:, half:]], packed_dtype=BF16)


def _to_tile_rows(v):
    t, w = v.shape
    nj = w // LANES
    parts = [v[:, j * LANES:(j + 1) * LANES].reshape(t // SUBLANES, 1, SUBLANES, LANES)
             for j in range(nj)]
    return jnp.concatenate(parts, axis=1).reshape(nj * t, LANES)


def _from_tile_rows(v2, nj):
    t = v2.shape[0] // nj
    v4 = v2.reshape(t // SUBLANES, nj, SUBLANES, LANES)
    return [v4[:, j].reshape(t, LANES) for j in range(nj)]


def _tile_row_offset(r, nj):
    hi = jnp.floor(r / SUBLANES)
    return hi * (SUBLANES * nj) + (r - hi * SUBLANES)


def _mixer_kernel(x_ref, *refs, has_moe_delta, tiles_per_seq):
    y_ref, refs = (refs[0], refs[1:]) if has_moe_delta else (None, refs)
    (g_ref, w_in32, caw_ref, cab_ref, lng_ref, lnb_ref, w_a_out32, w_pool32, pool_scale_ref,
     ccw_ref, w_c_out32, w_o32, o_ref, abuf, ubuf, vbuf, actbuf, pbuf,
     w_in_ref, w_a_out_ref, w_pool_ref, w_c_out_ref, w_o_ref) = refs
    step = pl.program_id(0)

    @pl.when(step < W_STEPS)
    def _():
        for src, dst in ((w_in32, w_in_ref), (w_a_out32, w_a_out_ref),
                         (w_c_out32, w_c_out_ref), (w_o32, w_o_ref)):
            _store_weight_slab(step, src, dst)
        _store_weight_slab(step, w_pool32, w_pool_ref, axis=1)

    @pl.when(step >= W_STEPS)
    def _():
        _mix_tile((step - W_STEPS) % tiles_per_seq, x_ref, y_ref, g_ref, w_in_ref, caw_ref,
                  cab_ref, lng_ref, lnb_ref, w_a_out_ref, w_pool_ref, pool_scale_ref, ccw_ref,
                  w_c_out_ref, w_o_ref, o_ref, abuf, ubuf, vbuf, actbuf, pbuf)


def _mix_tile(s, x_ref, y_ref, g_ref, w_in_ref, caw_ref, cab_ref, lng_ref, lnb_ref, w_a_out_ref,
              w_pool_ref, pool_scale_ref, ccw_ref, w_c_out_ref, w_o_ref,
              o_ref, abuf, ubuf, vbuf, actbuf, pbuf):
    ts = x_ref.shape[1]

    @pl.when(s == 0)
    def _():
        abuf[0, 0:HIST_A, :] = jnp.zeros((HIST_A, D_CONV), F32)
        ubuf[0:HIST_POOL, :] = jnp.zeros((HIST_POOL, D_POOL), F32)
        vbuf[0:HIST_C, :] = jnp.zeros((HIST_C, D_SC), F32)

    x = x_ref[0]
    if y_ref is not None:
        x = x + _unpack_pair(jnp.concatenate(_from_tile_rows(y_ref[...], H_TILES), axis=-1))
    h = _rmsnorm(x, g_ref[...]).astype(BF16)

    def proj(off, width):
        return _dot(h, w_in_ref[:, off:off + width])

    pa = proj(OFF_A, 2 * D_CONV)
    abuf[0, HIST_A:HIST_A + ts, :] = pa[:, :D_CONV] * _sigmoid(pa[:, D_CONV:])
    shifted_rows = HIST_A + ts - SUBLANES
    for r in range(1, SUBLANES):
        abuf[r, 0:shifted_rows, :] = abuf[0, r:r + shifted_rows, :]
    chunk3 = (CONV_ROW_CHUNK // SUBLANES, SUBLANES, D_CONV)
    for c0 in range(0, ts, CONV_ROW_CHUNK):
        acc = jnp.broadcast_to(cab_ref[...], chunk3)
        for k in range(CONV_A_K):
            q, r = divmod(HIST_A - (CONV_A_K - 1) + k, SUBLANES)
            lo = c0 + q * SUBLANES
            acc = acc + caw_ref[k] * abuf[r, lo:lo + CONV_ROW_CHUNK, :].reshape(chunk3)
        acc = acc.reshape(CONV_ROW_CHUNK, D_CONV)
        mu = jnp.mean(acc, axis=-1, keepdims=True)
        cen = acc - mu
        var = jnp.mean(cen * cen, axis=-1, keepdims=True)
        ln = cen * lax.rsqrt(var + EPS) * lng_ref[...] + lnb_ref[...]
        actbuf[c0:c0 + CONV_ROW_CHUNK, :] = (ln * _sigmoid(ln)).astype(BF16)
    abuf[0, 0:HIST_A, :] = abuf[0, ts:ts + HIST_A, :]

    pbuf[...] = proj(OFF_POOL, D_IN_PROJ - OFF_POOL)

    def pcol(off, width):
        return pbuf[:, off - OFF_POOL:off - OFF_POOL + width]

    pu = pcol(OFF_POOL, D_POOL)
    ubuf[HIST_POOL:HIST_POOL + ts, :] = pu
    merged = _sigmoid(pcol(OFF_G, D_MODEL)) * _dot(actbuf[...], w_a_out_ref[...])

    t_glob = s * ts + lax.broadcasted_iota(jnp.int32, (ts, 1), 0)
    p_parts = []
    for i, w in enumerate(POOL_WINDOWS):
        c0, c1 = i * POOL_GROUP_DIM, (i + 1) * POOL_GROUP_DIM
        tok = pu[:, c0:c1]
        win = ubuf[:, c0:c1]
        shift = 1
        while shift < w:
            win = win + pltpu.roll(win, shift, 0)
            shift *= 2
        win = win[HIST_POOL:]
        cnt = jnp.minimum(t_glob + 1, w).astype(F32)
        pin = (win / cnt - tok).astype(BF16)
        p_parts.append(_dot(pin, w_pool_ref[i]))
    ubuf[0:HIST_POOL, :] = ubuf[ts:ts + HIST_POOL, :]
    merged = merged + _sigmoid(pcol(OFF_G + D_MODEL, D_MODEL)) * (jnp.concatenate(p_parts, axis=-1) * pool_scale_ref[...])

    c_b = pcol(OFF_C + D_SC, D_SC)
    v = pcol(OFF_C + 2 * D_SC, D_SC) * pcol(OFF_C, D_SC)
    vbuf[HIST_C:HIST_C + ts, :] = v
    conv_c = ccw_ref[SC_K - 1:SC_K, :] * v
    for k in range(SC_K - 1):
        off = HIST_C - (SC_K - 1) + k
        conv_c = conv_c + ccw_ref[k:k + 1, :] * vbuf[off:off + ts, :]
    vbuf[0:HIST_C, :] = vbuf[ts:ts + HIST_C, :]
    merged = merged + _sigmoid(pcol(OFF_G + 2 * D_MODEL, D_MODEL)) * _dot((c_b * conv_c).astype(BF16), w_c_out_ref[...])

    o_ref[0] = x + _dot(merged.astype(BF16), w_o_ref[...])


def _mixer(x, moe_delta, layer, g, w_in, caw, cab, lng, lnb, w_a_out, w_pool, pool_scale, ccw,
           w_c_out, w_o):
    b, s, d = x.shape
    ts = SEQ_TILE
    nt = s // ts

    def tile(i):
        return jnp.maximum(i - W_STEPS, 0)

    def small(c):
        return _const_spec(c.shape)

    def slab_spec(w, axis):
        return _weight_slab_spec(w, layer, axis)

    delta = () if moe_delta is None else (moe_delta,)
    delta_specs = [pl.BlockSpec((H_TILES * ts, LANES), lambda i: (tile(i), 0))] * len(delta)
    x_spec = pl.BlockSpec((1, ts, d), lambda i: (tile(i) // nt, tile(i) % nt, 0))
    return pl.pallas_call(
        functools.partial(_mixer_kernel, has_moe_delta=moe_delta is not None, tiles_per_seq=nt),
        out_shape=jax.ShapeDtypeStruct(x.shape, F32),
        grid=(W_STEPS + b * nt,),
        in_specs=[x_spec] + delta_specs
        + [small(g), slab_spec(w_in, 0), small(caw), small(cab), small(lng), small(lnb),
           slab_spec(w_a_out, 0), slab_spec(w_pool, 1), small(pool_scale), small(ccw),
           slab_spec(w_c_out, 0), slab_spec(w_o, 0)],
        out_specs=x_spec,
        scratch_shapes=[pltpu.VMEM((SUBLANES, HIST_A + ts, D_CONV), F32),
                        pltpu.VMEM((HIST_POOL + ts, D_POOL), F32),
                        pltpu.VMEM((HIST_C + ts, D_SC), F32),
                        pltpu.VMEM((ts, D_CONV), BF16),
                        pltpu.VMEM((ts, D_IN_PROJ - OFF_POOL), F32)]
        + [pltpu.VMEM(w.shape[1:], BF16) for w in (w_in, w_a_out, w_pool, w_c_out, w_o)],
        compiler_params=pltpu.CompilerParams(
            dimension_semantics=("arbitrary",), vmem_limit_bytes=VMEM_LIMIT_BYTES),
        name="mixer",
    )(x, *delta, g, w_in, caw, cab, lng, lnb, w_a_out, w_pool, pool_scale, ccw, w_c_out, w_o)


def _route_rows(lt):
    def first_argmax(vals, vmax):
        idx = jnp.full(vmax.shape, len(vals) - 1, jnp.int32)
        for k in range(len(vals) - 2, -1, -1):
            idx = jnp.where(vals[k] == vmax, k, idx)
        return idx

    g = [lt[k:k + 1, :] for k in range(MOE_GROUPS)]
    gmax = functools.reduce(jnp.maximum, g)
    g_idx = first_argmax(g, gmax)
    g_val = 1.0 / sum(jnp.exp(v - gmax) for v in g)
    e = []
    for j in range(EXPERTS_PER_GROUP):
        ej = lt[MOE_GROUPS + j:MOE_GROUPS + j + 1, :]
        for grp in range(1, MOE_GROUPS):
            r = MOE_GROUPS + grp * EXPERTS_PER_GROUP + j
            ej = jnp.where(g_idx == grp, lt[r:r + 1, :], ej)
        e.append(ej)
    m1 = functools.reduce(jnp.maximum, e)
    i1 = first_argmax(e, m1)
    e2 = [jnp.where(i1 == j, -jnp.inf, e[j]) for j in range(EXPERTS_PER_GROUP)]
    m2 = functools.reduce(jnp.maximum, e2)
    i2 = first_argmax(e2, m2)
    esum = sum(jnp.exp(v - m1) for v in e)
    p1 = 1.0 / esum
    p2 = jnp.exp(m2 - m1) / esum
    w1 = g_val * (p1 / (p1 + p2))
    w2 = g_val * (p2 / (p1 + p2))
    lo = jnp.minimum(i1, i2)
    hi = jnp.maximum(i1, i2)
    pair = ((lo * (2 * EXPERTS_PER_GROUP - 1 - lo)) >> 1) + (hi - lo - 1)
    cls = g_idx * PAIRS_PER_GROUP + pair
    first_is_low = i1 < i2
    return cls, jnp.where(first_is_low, w1, w2), jnp.where(first_is_low, w2, w1)


def _xattn_kernel(x_ref, g_ref, w_q32, mem_ref, gm_ref, w_kv32, w_o32, gf_ref, w_r_ref, b_r_ref,
                  o_ref, hp_ref, cls_ref, xn_prev, w_q_ref, w_kv_ref, w_o_ref, kt_ref, v_ref,
                  *, tiles, tiles_per_seq):
    step = pl.program_id(0)

    @pl.when(step == 0)
    def _():
        xn_prev[...] = jnp.zeros(xn_prev.shape, F32)

    @pl.when(step < W_STEPS)
    def _():
        _store_weight_slab(step, w_q32, w_q_ref)
        _store_weight_slab(step, w_kv32, w_kv_ref)
        _store_weight_slab(step, w_o32, w_o_ref)

    @pl.when((step >= W_STEPS) & (step < W_STEPS + tiles)
             & ((step - W_STEPS) % tiles_per_seq == 0))
    def _():
        mn = _rmsnorm(mem_ref[0], gm_ref[...]).astype(BF16)
        kv = _dot(mn, w_kv_ref[...])
        kt_ref[...] = kv[:, :D_MODEL].T.astype(BF16)
        v_ref[...] = kv[:, D_MODEL:].astype(BF16)

    @pl.when(step >= W_STEPS)
    def _():
        _xattn_tile(x_ref, g_ref, w_q_ref, kt_ref, v_ref, w_o_ref, gf_ref, w_r_ref, b_r_ref,
                    o_ref, hp_ref, cls_ref, xn_prev)


def _xattn_tile(x_ref, g_ref, w_q_ref, kt_ref, v_ref, w_o_ref, gf_ref, w_r_ref, b_r_ref,
                o_ref, hp_ref, cls_ref, xn_prev):
    hf = _rmsnorm(xn_prev[...], gf_ref[...])
    h_hi = hf.astype(BF16)
    h_lo = (hf - h_hi.astype(F32)).astype(BF16)
    nt_dims = (((1,), (1,)), ((), ()))
    part = lax.dot_general(w_r_ref[...], h_hi, nt_dims, preferred_element_type=F32)
    lt = (part[:ROUTER_ROWS] + part[ROUTER_ROWS:]
          + lax.dot_general(w_r_ref[:ROUTER_ROWS, :], h_lo, nt_dims, preferred_element_type=F32)
          + b_r_ref[...])
    cls, wa, wb = _route_rows(lt)
    ts = cls.shape[1]
    for c in range(ts // LANES):
        cls_ref[0, c:c + 1, :] = cls[:, c * LANES:(c + 1) * LANES]
    row = lax.broadcasted_iota(jnp.int32, (LANES, ts), 0)

    def weight_lanes(w):
        w_hi = w.astype(BF16).astype(F32)
        return jnp.where(row == 0, w_hi, jnp.where(row == 1, w - w_hi, 0.0)).T

    wt_tile = pltpu.pack_elementwise([weight_lanes(wa), weight_lanes(wb)], packed_dtype=BF16)
    hp_ref[...] = _to_tile_rows(jnp.concatenate([_pack_pair(hf), wt_tile], axis=-1))

    x = x_ref[0]
    h = _rmsnorm(x, g_ref[...]).astype(BF16)
    q = (_dot(h, w_q_ref[...]) * (XATTN_HEAD_DIM ** -0.5)).astype(BF16)
    heads = []
    for hd in range(XATTN_HEADS):
        c0, c1 = hd * XATTN_HEAD_DIM, (hd + 1) * XATTN_HEAD_DIM
        sc = _dot(q[:, c0:c1], kt_ref[c0:c1, :])
        e = jnp.exp(sc - jnp.max(sc, axis=-1, keepdims=True))
        denom = jnp.sum(e, axis=-1, keepdims=True)
        heads.append(_dot(e.astype(BF16), v_ref[:, c0:c1]) / denom)
    o = jnp.concatenate(heads, axis=-1).astype(BF16)
    xn = x + _dot(o, w_o_ref[...])
    o_ref[0] = xn
    xn_prev[...] = xn


def _xattn(x, g, w_q, mem, g_mem, w_kv, w_o, g_ffn, w_r, b_r, layer):
    b, s, d = x.shape
    n_mem = mem.shape[1]
    ts = XATTN_SEQ_TILE
    nt = s // ts
    n = b * s
    tiles = b * nt

    def attn_tile(step):
        return jnp.clip(step - W_STEPS, 0, tiles - 1)

    def route_tile(step):
        return jnp.maximum(step - W_STEPS - 1, 0)

    return pl.pallas_call(
        functools.partial(_xattn_kernel, tiles=tiles, tiles_per_seq=nt),
        out_shape=(jax.ShapeDtypeStruct(x.shape, F32),
                   jax.ShapeDtypeStruct((ROW_TILES * n, LANES), jnp.uint32),
                   jax.ShapeDtypeStruct((n // ts, ts // LANES, LANES), jnp.int32)),
        grid=(W_STEPS + tiles + 1,),
        in_specs=[pl.BlockSpec((1, ts, d), lambda i: (attn_tile(i) // nt, attn_tile(i) % nt, 0)),
                  _const_spec(g.shape), _weight_slab_spec(w_q, layer),
                  pl.BlockSpec((1, n_mem, d), lambda i: (attn_tile(i) // nt, 0, 0)),
                  _const_spec(g_mem.shape), _weight_slab_spec(w_kv, layer),
                  _weight_slab_spec(w_o, layer), _const_spec(g_ffn.shape),
                  _const_spec(w_r.shape), _const_spec(b_r.shape)],
        out_specs=(pl.BlockSpec((1, ts, d), lambda i: (attn_tile(i) // nt, attn_tile(i) % nt, 0)),
                   pl.BlockSpec((ROW_TILES * ts, LANES), lambda i: (route_tile(i), 0)),
                   pl.BlockSpec((1, ts // LANES, LANES), lambda i: (route_tile(i), 0, 0))),
        scratch_shapes=[pltpu.VMEM((ts, d), F32), pltpu.VMEM(w_q.shape[1:], BF16),
                        pltpu.VMEM(w_kv.shape[1:], BF16), pltpu.VMEM(w_o.shape[1:], BF16),
                        pltpu.VMEM((d, n_mem), BF16), pltpu.VMEM((n_mem, d), BF16)],
        compiler_params=pltpu.CompilerParams(
            dimension_semantics=("arbitrary",), vmem_limit_bytes=VMEM_LIMIT_BYTES),
        name="xattn",
    )(x, g, w_q, mem, g_mem, w_kv, w_o, g_ffn, w_r, b_r)


def _plan_kernel(cls_ref, qs_ref, qy_ref, ea_ref, eb_ref, nu_ref):
    cls = cls_ref[...]
    nr, nl = cls.shape
    r = lax.broadcasted_iota(jnp.int32, (nl, nl), 0)
    c = lax.broadcasted_iota(jnp.int32, (nl, nl), 1)
    lanes_before = (r < c).astype(BF16)
    rr = lax.broadcasted_iota(jnp.int32, (nr, nr), 0)
    rc = lax.broadcasted_iota(jnp.int32, (nr, nr), 1)
    rows_before = (rc < rr).astype(BF16)
    ones_l = jnp.ones((nl, nl), BF16)
    ones_r = jnp.ones((nr, nr), BF16)
    n_tab = ea_ref.shape[1]
    tile_i = lax.broadcasted_iota(jnp.int32, (SUBLANES, n_tab), 1).astype(F32)
    run = jnp.zeros((nr, nl), F32)
    pos = jnp.zeros((nr, nl), F32)
    tile_cls = jnp.zeros((SUBLANES, n_tab), F32)
    for k in range(N_CLASSES):
        m = cls == k
        mb = jnp.where(m, 1.0, 0.0).astype(BF16)
        rowtot = _dot(mb, ones_l).astype(BF16)
        rank = _dot(mb, lanes_before) + _dot(rows_before, rowtot)
        total = _dot(ones_r, rowtot)
        pos = jnp.where(m, run * EXPERT_ROW_TILE + rank, pos)
        run = run + jnp.floor((total + (EXPERT_ROW_TILE - 1)) / EXPERT_ROW_TILE)
        end = jnp.concatenate([run[:SUBLANES, :]] * (n_tab // nl), axis=1)
        tile_cls = tile_cls + jnp.where(end <= tile_i, 1.0, 0.0)
    qs_ref[...] = _tile_row_offset(pos, ROW_TILES).astype(jnp.int32)
    qy_ref[...] = _tile_row_offset(pos, H_TILES).astype(jnp.int32)
    tc = jnp.minimum(tile_cls, N_CLASSES - 1.0).astype(jnp.int32)
    grp = sum(jnp.where(tc >= g * PAIRS_PER_GROUP, 1, 0) for g in range(1, MOE_GROUPS))
    pair = tc - grp * PAIRS_PER_GROUP
    pair_lo = jnp.where(pair >= 3, 1, 0) + jnp.where(pair >= 5, 1, 0)
    pair_hi = pair + 1 - jnp.where(pair >= 5, 3, jnp.where(pair >= 3, 2, 0))
    ea_ref[...] = grp * EXPERTS_PER_GROUP + pair_lo
    eb_ref[...] = grp * EXPERTS_PER_GROUP + pair_hi
    nu_ref[...] = run[:SUBLANES, :].astype(jnp.int32)


def _plan(cls2):
    nr, nl = cls2.shape
    n_tiles = _sorted_rows(nr * nl) // EXPERT_ROW_TILE
    n_tab = -(-n_tiles // nl) * nl
    vmem = pl.BlockSpec(memory_space=pltpu.VMEM)
    qs, qy, ea, eb, nu = pl.pallas_call(
        _plan_kernel,
        out_shape=(jax.ShapeDtypeStruct((nr, nl), jnp.int32),
                   jax.ShapeDtypeStruct((nr, nl), jnp.int32),
                   jax.ShapeDtypeStruct((SUBLANES, n_tab), jnp.int32),
                   jax.ShapeDtypeStruct((SUBLANES, n_tab), jnp.int32),
                   jax.ShapeDtypeStruct((SUBLANES, nl), jnp.int32)),
        in_specs=[vmem],
        out_specs=(vmem, vmem, vmem, vmem, vmem),
        name="moe_plan",
    )(cls2)
    return (qs.reshape(nr * nl), qy.reshape(nr * nl),
            ea[0, :n_tiles], eb[0, :n_tiles], nu[0, :1])


def _scatter_kernel(qs_ref, hp_ref, s_ref):
    i = pl.program_id(0)
    tn = hp_ref.shape[0] // ROW_TILES
    group = SUBLANES * ROW_TILES

    @pl.when(i == 0)
    def _():
        zero_rows = _pack_pair(jnp.zeros((hp_ref.shape[0], 2 * LANES), F32))

        def fill(c, carry):
            start = pl.multiple_of(c * hp_ref.shape[0], SUBLANES)
            s_ref[pl.ds(start, hp_ref.shape[0]), :] = zero_rows
            return carry

        lax.fori_loop(0, s_ref.shape[0] // hp_ref.shape[0], fill, 0)

    def body(j, carry):
        src = pl.multiple_of(j * group, SUBLANES)
        for k in range(SUBLANES):
            q = qs_ref[i * tn + j * SUBLANES + k]
            s_ref[pl.ds(q, ROW_TILES, stride=SUBLANES), :] = (
                hp_ref[pl.ds(src + k, ROW_TILES, stride=SUBLANES), :])
        return carry

    lax.fori_loop(0, tn // SUBLANES, body, 0, unroll=ROW_LOOP_UNROLL)


def _scatter(qs, hp):
    n = qs.shape[0]
    tn = SCATTER_TILE
    rows = ROW_TILES * _sorted_rows(n)
    grid_spec = pltpu.PrefetchScalarGridSpec(
        num_scalar_prefetch=1,
        grid=(n // tn,),
        in_specs=[pl.BlockSpec((ROW_TILES * tn, LANES), lambda i, qs: (i, 0))],
        out_specs=pl.BlockSpec((rows, LANES), lambda i, qs: (0, 0), pipeline_mode=pl.Buffered(1)),
    )
    return pl.pallas_call(
        _scatter_kernel,
        out_shape=jax.ShapeDtypeStruct((rows, LANES), jnp.uint32),
        grid_spec=grid_spec,
        compiler_params=pltpu.CompilerParams(
            dimension_semantics=("arbitrary",), vmem_limit_bytes=VMEM_LIMIT_RESIDENT_BYTES),
        name="moe_scatter",
    )(qs, hp)


def _expert_kernel(ea_ref, eb_ref, nu_ref, s_ref, wg32_ref, wu32_ref, wd32_ref, y_ref,
                   wg_ref, wu_ref, wd_ref):
    step = pl.program_id(0)

    @pl.when(step < W_STEPS)
    def _():
        _store_weight_slab(step, wg32_ref, wg_ref)
        _store_weight_slab(step, wu32_ref, wu_ref)
        _store_weight_slab(step, wd32_ref, wd_ref)

    first = (step - W_STEPS) * EXPERT_STEP_TILES
    rec = EXPERT_ROW_TILE * ROW_TILES
    out = EXPERT_ROW_TILE * H_TILES

    @pl.when((step >= W_STEPS) & (first < nu_ref[0]))
    def _():
        acts = []
        for j in range(EXPERT_STEP_TILES):
            t = first + j
            tiles = _from_tile_rows(s_ref[j * rec:(j + 1) * rec, :], ROW_TILES)
            hb = _unpack_pair(jnp.concatenate(tiles[:H_TILES], axis=-1)).astype(BF16)
            for e_ref, idx in ((ea_ref, 0), (eb_ref, 1)):
                e = e_ref[t]
                wt = pltpu.unpack_elementwise(tiles[H_TILES], index=idx, packed_dtype=BF16,
                                              unpacked_dtype=F32)
                wt = wt[:, 0:1] + wt[:, 1:2]
                gt = _dot(hb, wg_ref[e])
                acts.append((gt * _sigmoid(gt) * _dot(hb, wu_ref[e]) * wt).astype(BF16))
        for j in range(EXPERT_STEP_TILES):
            t = first + j
            y = (_dot(acts[2 * j], wd_ref[ea_ref[t]])
                 + _dot(acts[2 * j + 1], wd_ref[eb_ref[t]]))
            y_ref[j * out:(j + 1) * out, :] = _to_tile_rows(_pack_pair(y))

    @pl.when((step >= W_STEPS) & (first >= nu_ref[0]))
    def _():
        y_ref[...] = _pack_pair(jnp.zeros((y_ref.shape[0], 2 * LANES), F32))


def _experts(tile_ea, tile_eb, n_used, s, wg, wu, wd, layer):
    rows = s.shape[0] // ROW_TILES
    tb = EXPERT_ROW_TILE * EXPERT_STEP_TILES

    def row_block(i, *_):
        return (jnp.maximum(i - W_STEPS, 0), 0)

    grid_spec = pltpu.PrefetchScalarGridSpec(
        num_scalar_prefetch=3,
        grid=(W_STEPS + rows // tb,),
        in_specs=[pl.BlockSpec((ROW_TILES * tb, LANES), row_block),
                  _weight_slab_spec(wg, layer), _weight_slab_spec(wu, layer),
                  _weight_slab_spec(wd, layer)],
        out_specs=pl.BlockSpec((H_TILES * tb, LANES), row_block),
        scratch_shapes=[pltpu.VMEM(wg.shape[1:], BF16), pltpu.VMEM(wu.shape[1:], BF16),
                        pltpu.VMEM(wd.shape[1:], BF16)],
    )
    return pl.pallas_call(
        _expert_kernel,
        out_shape=jax.ShapeDtypeStruct((H_TILES * rows, LANES), jnp.uint32),
        grid_spec=grid_spec,
        compiler_params=pltpu.CompilerParams(
            dimension_semantics=("arbitrary",), vmem_limit_bytes=VMEM_LIMIT_BYTES),
        name="moe_experts",
    )(tile_ea, tile_eb, n_used, s, wg, wu, wd)


def _unsort_kernel(qy_ref, y_ref, o_ref):
    i = pl.program_id(0)
    tn = o_ref.shape[0] // H_TILES
    group = SUBLANES * H_TILES

    def body(j, carry):
        dst = pl.multiple_of(j * group, SUBLANES)
        for k in range(SUBLANES):
            q = qy_ref[i * tn + j * SUBLANES + k]
            o_ref[pl.ds(dst + k, H_TILES, stride=SUBLANES), :] = (
                y_ref[pl.ds(q, H_TILES, stride=SUBLANES), :])
        return carry

    lax.fori_loop(0, tn // SUBLANES, body, 0, unroll=ROW_LOOP_UNROLL)


def _unsort(qy, y):
    n = qy.shape[0]
    tn = UNSORT_TILE
    grid_spec = pltpu.PrefetchScalarGridSpec(
        num_scalar_prefetch=1,
        grid=(n // tn,),
        in_specs=[pl.BlockSpec(y.shape, lambda i, qy: (0, 0), pipeline_mode=pl.Buffered(1))],
        out_specs=pl.BlockSpec((H_TILES * tn, LANES), lambda i, qy: (i, 0)),
    )
    return pl.pallas_call(
        _unsort_kernel,
        out_shape=jax.ShapeDtypeStruct((H_TILES * n, LANES), jnp.uint32),
        grid_spec=grid_spec,
        compiler_params=pltpu.CompilerParams(
            dimension_semantics=("arbitrary",), vmem_limit_bytes=VMEM_LIMIT_RESIDENT_BYTES),
        name="moe_unsort",
    )(qy, y)


def _gather_kernel(qy_ref, x_ref, y_ref, gn_ref, o_ref, rows_ref):
    i = pl.program_id(0)
    tn = x_ref.shape[0]
    group = SUBLANES * H_TILES

    def body(j, carry):
        dst = pl.multiple_of(j * group, SUBLANES)
        for k in range(SUBLANES):
            q = qy_ref[i * tn + j * SUBLANES + k]
            rows_ref[pl.ds(dst + k, H_TILES, stride=SUBLANES), :] = (
                y_ref[pl.ds(q, H_TILES, stride=SUBLANES), :])
        return carry

    lax.fori_loop(0, tn // SUBLANES, body, 0, unroll=ROW_LOOP_UNROLL)
    packed = jnp.concatenate(_from_tile_rows(rows_ref[...], H_TILES), axis=-1)
    o_ref[...] = _rmsnorm(x_ref[...] + _unpack_pair(packed), gn_ref[...])


def _gather(qy, x2, y, g_final):
    n, d = x2.shape
    tn = GATHER_TILE
    grid_spec = pltpu.PrefetchScalarGridSpec(
        num_scalar_prefetch=1,
        grid=(n // tn,),
        in_specs=[pl.BlockSpec((tn, d), lambda i, qy: (i, 0)),
                  pl.BlockSpec(y.shape, lambda i, qy: (0, 0), pipeline_mode=pl.Buffered(1)),
                  pl.BlockSpec(g_final.shape, lambda i, qy: (0, 0), pipeline_mode=pl.Buffered(1))],
        out_specs=pl.BlockSpec((tn, d), lambda i, qy: (i, 0)),
        scratch_shapes=[pltpu.VMEM((H_TILES * tn, LANES), jnp.uint32)],
    )
    return pl.pallas_call(
        _gather_kernel,
        out_shape=jax.ShapeDtypeStruct(x2.shape, F32),
        grid_spec=grid_spec,
        compiler_params=pltpu.CompilerParams(
            dimension_semantics=("arbitrary",), vmem_limit_bytes=VMEM_LIMIT_RESIDENT_BYTES),
        name="moe_gather",
    )(qy, x2, y, g_final)


def _row(v):
    return v.reshape(1, -1)


def _router_params(w_rg, b_rg, w_re, b_re):
    pad = ROUTER_ROWS - MOE_GROUPS - N_EXPERTS
    wt = jnp.pad(jnp.concatenate([w_rg, w_re], axis=1).T, ((0, pad), (0, 0)))
    w_hi = wt.astype(BF16)
    w_lo = (wt - w_hi.astype(F32)).astype(BF16)
    b_col = jnp.pad(jnp.concatenate([b_rg, b_re]), (0, pad)).reshape(ROUTER_ROWS, 1)
    return jnp.concatenate([w_hi, w_lo], axis=0), b_col


def kernel(x, mem, norm_mix_g, w_in, conv_a_w, conv_a_b, ln_a_g, ln_a_b, w_a_out, w_pool_grp, pool_scale, conv_c_w, w_c_out, w_o, norm_x_g, norm_mem_g, w_xq, w_xkv, w_xo, norm_ffn_g, w_rg, b_rg, w_re, b_re, w_e_gate, w_e_up, w_e_down, norm_f_g):
    bsz, seq, d = x.shape
    n = bsz * seq
    depth = w_in.shape[0]
    moe_delta = None
    for l in range(depth):
        caw = jnp.broadcast_to(conv_a_w[l][:, None, :], (CONV_A_K, SUBLANES, D_CONV))
        x = _mixer(x, moe_delta, l, _row(norm_mix_g[l]), w_in, caw, _row(conv_a_b[l]),
                   _row(ln_a_g[l]), _row(ln_a_b[l]), w_a_out, w_pool_grp, _row(pool_scale[l]),
                   conv_c_w[l], w_c_out, w_o)
        w_r, b_r = _router_params(w_rg[l], b_rg[l], w_re[l], b_re[l])
        x, hp, cls = _xattn(x, _row(norm_x_g[l]), w_xq, mem, _row(norm_mem_g[l]), w_xkv, w_xo,
                            _row(norm_ffn_g[l]), w_r, b_r, l)
        qs, qy, tile_ea, tile_eb, n_used = _plan(cls.reshape(n // LANES, LANES))
        y = _experts(tile_ea, tile_eb, n_used, _scatter(qs, hp), w_e_gate, w_e_up, w_e_down, l)
        if l < depth - 1:
            moe_delta = _unsort(qy, y)
    return _gather(qy, x.reshape(n, d), y, _row(norm_f_g)).reshape(bsz, seq, d)
```
